```python
import math
import jax
import jax.numpy as jnp
from jax import lax
import numpy as np

D_MODEL = 1024
BATCH = 8
SEQ = 2048
DEPTH = 4
DEC_BATCH = 32
DEC_SEQ = 1
PAST_LEN = 8192
PAGE_SIZE = 128

HEAD_DIM = 64
CONV_WIDTH = 4
SB_WIDTH = D_MODEL // 4
SB_HEADS = SB_WIDTH // HEAD_DIM
SB_BLOCK = 128
SB_BIAS_INIT = -6.0
GDN_WIDTH = D_MODEL // 4
GDN_HEADS = GDN_WIDTH // HEAD_DIM
GDN_DK = HEAD_DIM
GDN_DV = HEAD_DIM
GDN_QK_WIDTH = GDN_HEADS * GDN_DK
GDN_CONV_DIM = 2 * GDN_QK_WIDTH + GDN_WIDTH
GDN_CHUNK = 64
SSD_WIDTH = D_MODEL // 2
SSD_HEAD_DIM = HEAD_DIM
SSD_HEADS = SSD_WIDTH // SSD_HEAD_DIM
SSD_GROUPS = 2
SSD_STATE = 128
SSD_XBC = SSD_WIDTH + 2 * SSD_GROUPS * SSD_STATE
SSD_CHUNK = 128
MIX_WIDTH = SB_WIDTH + GDN_WIDTH + SSD_WIDTH
IN_WIDTHS = (SB_WIDTH, SB_WIDTH, SB_WIDTH, GDN_CONV_DIM, GDN_WIDTH, GDN_HEADS, GDN_HEADS, SSD_WIDTH, SSD_XBC, SSD_HEADS)
D_IN = sum(IN_WIDTHS)
N_EXPERTS = 32
TOP_K = 4
D_EXPERT = D_MODEL // 2
SWIGLU_ALPHA = 1.702
SWIGLU_LIMIT = 7.0
MOE_BLOCK = 128
MOE_MIN_BLOCK = 8
PLE_DIM = 256
DEEPNORM_ALPHA = (2 * DEPTH) ** 0.25
DEEPNORM_BETA = (8 * DEPTH) ** -0.25
LN_EPS = 1e-5
RMS_EPS = 1e-6

kernel_name = 'hybrid_stickbreak_gdn_ssd_moe_step'


def _in_splits():
    return [int(s) for s in np.cumsum(IN_WIDTHS)[:-1]]


def _layer_norm(x, g, b):
    xf = x.astype(jnp.float32)
    mu = jnp.mean(xf, axis=-1, keepdims=True)
    var = jnp.mean(jnp.square(xf - mu), axis=-1, keepdims=True)
    return ((xf - mu) * lax.rsqrt(var + LN_EPS) * g.astype(jnp.float32) + b.astype(jnp.float32)).astype(x.dtype)


def _rms_norm(x, g):
    xf = x.astype(jnp.float32)
    return xf * lax.rsqrt(jnp.mean(xf * xf, axis=-1, keepdims=True) + RMS_EPS) * g.astype(jnp.float32)


def _l2norm(x):
    xf = x.astype(jnp.float32)
    return xf * lax.rsqrt(jnp.sum(xf * xf, axis=-1, keepdims=True) + RMS_EPS)


def _causal_conv(u, buf, w, b=None):
    full = jnp.concatenate([buf.astype(u.dtype), u], axis=1)
    out = lax.conv_general_dilated(full, w[:, None, :].astype(u.dtype), window_strides=(1,), padding='VALID',
                                   dimension_numbers=('NWC', 'WIO', 'NWC'), feature_group_count=u.shape[-1])
    if b is not None:
        out = out + b
    return out, full[:, -(CONV_WIDTH - 1):]


def _to_chunks(u, size, pad):
    u = jnp.pad(u, [(0, 0), (0, pad)] + [(0, 0)] * (u.ndim - 2))
    bsz, tp = u.shape[:2]
    u = u.reshape(bsz, tp // size, size, *u.shape[2:])
    return u.transpose((1, 0, 3, 2) + tuple(range(4, u.ndim)))


def _from_chunks(y, t):
    nc, bsz, h, size, p = y.shape
    return y.transpose(1, 0, 3, 2, 4).reshape(bsz, nc * size, h, p)[:, :t]


def _gather_pages(pool, page_table):
    rows = pool[page_table]
    return rows.reshape(page_table.shape[0], -1, *pool.shape[2:])


def _stick_breaking(q, k, v, bias, n_past):
    bsz, t, h, dh = q.shape
    n_keys = k.shape[1]
    blk = SB_BLOCK if t % SB_BLOCK == 0 else t
    nb = t // blk
    qb = q.reshape(bsz, nb, blk, h, dh).transpose(1, 0, 3, 2, 4)
    kh = k.transpose(0, 2, 1, 3)
    vh = v.transpose(0, 2, 1, 3)
    key_pos = jnp.arange(n_keys)
    scale = dh ** -0.5
    head_bias = bias.astype(jnp.float32)[None, :, None, None]

    def block(args):
        qi, bi = args
        q_pos = n_past + bi * blk + jnp.arange(blk)
        z = jnp.einsum('bhqd,bhkd->bhqk', qi, kh).astype(jnp.float32) * scale + head_bias
        visible = key_pos[None, :] < q_pos[:, None]
        log_keep = jnp.where(visible, jax.nn.log_sigmoid(-z), 0.0)
        later = jnp.concatenate([lax.cumsum(log_keep, axis=3, reverse=True)[..., 1:],
                                 jnp.zeros_like(log_keep[..., :1])], axis=-1)
        w = jnp.where(visible, jnp.exp(jax.nn.log_sigmoid(z) + later), 0.0)
        return jnp.einsum('bhqk,bhkd->bhqd', w.astype(vh.dtype), vh)

    o = lax.map(block, (qb, jnp.arange(nb)))
    return o.transpose(1, 0, 3, 2, 4).reshape(bsz, t, h * dh)


def _gated_delta_chunked(q, k, v, g, beta, s0):
    f32 = jnp.float32
    t = q.shape[1]
    n_v = v.shape[-1]
    size = min(GDN_CHUNK, t)
    pad = (-t) % size
    qc, kc, vc, gc, bc = (_to_chunks(u.astype(f32), size, pad) for u in (q, k, v, g, beta))
    gcs = jnp.cumsum(gc, axis=-1)
    incl = jnp.tril(jnp.ones((size, size), bool))
    strict = jnp.tril(jnp.ones((size, size), bool), -1)
    decay = jnp.exp(jnp.where(incl, gcs[..., :, None] - gcs[..., None, :], -jnp.inf))
    kb = kc * bc[..., None]
    a_mat = jnp.where(strict, jnp.einsum('cbhlk,cbhsk->cbhls', kb, kc) * decay, 0.0)
    rhs = jnp.concatenate([vc * bc[..., None], kb * jnp.exp(gcs)[..., None]], axis=-1)
    sol = lax.linalg.triangular_solve(a_mat, rhs, left_side=True, lower=True, unit_diagonal=True)
    u_c, w_c = sol[..., :n_v], sol[..., n_v:]
    qk = jnp.einsum('cbhlk,cbhsk->cbhls', qc, kc) * decay
    q_dec = qc * jnp.exp(gcs)[..., None]
    k_dec = kc * jnp.exp(gcs[..., -1:] - gcs)[..., None]
    chunk_decay = jnp.exp(gcs[..., -1])

    def step(s, inp):
        qk_i, qd_i, kd_i, u_i, w_i, d_i = inp
        v_new = u_i - jnp.einsum('bhlk,bhkv->bhlv', w_i, s)
        o = jnp.einsum('bhlk,bhkv->bhlv', qd_i, s) + jnp.einsum('bhls,bhsv->bhlv', qk_i, v_new)
        return s * d_i[..., None, None] + jnp.einsum('bhlk,bhlv->bhkv', kd_i, v_new), o

    s_last, o = lax.scan(step, s0.astype(f32), (qk, q_dec, k_dec, u_c, w_c, chunk_decay))
    return _from_chunks(o, t), s_last


def _ssd_chunked(x, dt, a, b, c, h0):
    f32 = jnp.float32
    t = x.shape[1]
    size = min(SSD_CHUNK, t)
    pad = (-t) % size
    xc, ac, bc, cc = (_to_chunks(u.astype(f32), size, pad) for u in (x * dt[..., None], dt * a, b, c))
    acs = jnp.cumsum(ac, axis=-1)
    incl = jnp.tril(jnp.ones((size, size), bool))
    decay = jnp.exp(jnp.where(incl, acs[..., :, None] - acs[..., None, :], -jnp.inf))
    scores = jnp.einsum('cbhln,cbhsn->cbhls', cc, bc) * decay
    y_diag = jnp.einsum('cbhls,cbhsp->cbhlp', scores, xc)
    chunk_in = jnp.einsum('cbhsn,cbhsp->cbhpn', bc * jnp.exp(acs[..., -1:] - acs)[..., None], xc)

    def step(h, inp):
        d_i, in_i = inp
        return h * d_i[..., None, None] + in_i, h

    h_last, h_enter = lax.scan(step, h0.astype(f32), (jnp.exp(acs[..., -1]), chunk_in))
    y_off = jnp.einsum('cbhln,cbhpn->cbhlp', cc * jnp.exp(acs)[..., None], h_enter)
    return _from_chunks(y_diag + y_off, t), h_last


def _moe(h, router_w, router_b, w_gate, b_gate, w_up, b_up, w_down, b_down):
    n_tok, d = h.shape
    logits = (h @ router_w).astype(jnp.float32) + router_b.astype(jnp.float32)
    top_logit, top_idx = lax.top_k(logits, TOP_K)
    gate = jax.nn.softmax(top_logit, axis=-1)
    n_assign = n_tok * TOP_K
    blk = MOE_BLOCK if n_assign >= MOE_BLOCK * N_EXPERTS else MOE_MIN_BLOCK
    n_blocks = -(-(n_assign + N_EXPERTS * (blk - 1)) // blk)
    expert = top_idx.reshape(-1)
    order = jnp.argsort(expert)
    sorted_expert = expert[order]
    token = order // TOP_K
    counts = jnp.zeros((N_EXPERTS,), jnp.int32).at[expert].add(1)
    padded = (counts + blk - 1) // blk * blk
    padded_end = jnp.cumsum(padded)
    rank = jnp.arange(n_assign) - (jnp.cumsum(counts) - counts)[sorted_expert]
    row = (padded_end - padded)[sorted_expert] + rank
    rows_in = jnp.zeros((n_blocks * blk, d), h.dtype).at[row].set(h[token])
    block_expert = jnp.minimum(jnp.searchsorted(padded_end, jnp.arange(n_blocks) * blk, side='right'), N_EXPERTS - 1)

    def expert_block(args):
        xb, e = args
        glu = jnp.minimum(xb @ w_gate[e] + b_gate[e], SWIGLU_LIMIT)
        lin = jnp.clip(xb @ w_up[e] + b_up[e], -SWIGLU_LIMIT, SWIGLU_LIMIT)
        act = glu * jax.nn.sigmoid(SWIGLU_ALPHA * glu) * (lin + 1.0)
        return act @ w_down[e] + b_down[e]

    rows_out = lax.map(expert_block, (rows_in.reshape(n_blocks, blk, d), block_expert)).reshape(-1, d)
    contrib = rows_out[row] * gate.reshape(-1)[order][:, None].astype(rows_out.dtype)
    return jax.ops.segment_sum(contrib, token, num_segments=n_tok)


def _trunk_layer(x, p, k_past, v_past, conv_b, delta_b, conv_c, ssm_c,
                 w_in, sb_bias, w_conv_b, a_log_b, dt_bias_b, g_norm_b,
                 w_conv_c, b_conv_c, a_log_c, dt_bias_c, d_skip_c, g_norm_c,
                 w_out, ln1_g, ln1_b, router_w, router_b, w_gate, b_gate, w_up, b_up,
                 w_down, b_down, ln2_g, ln2_b, ple_gate_w, ple_gate_b, ple_proj_w):
    f32 = jnp.float32
    bsz, t, _ = x.shape
    heads = lambda u, n: u.reshape(bsz, t, n, -1)
    (q_a, k_a, v_a, qkv_b, z_b, beta_b, decay_b, z_c, xbc_c, dt_c) = jnp.split(x @ w_in, _in_splits(), axis=-1)

    k_a = heads(k_a, SB_HEADS)
    v_a = heads(v_a, SB_HEADS)
    o_a = _stick_breaking(heads(q_a, SB_HEADS),
                          jnp.concatenate([k_past.astype(k_a.dtype), k_a], axis=1),
                          jnp.concatenate([v_past.astype(v_a.dtype), v_a], axis=1),
                          sb_bias, k_past.shape[1])

    u_b, conv_b_new = _causal_conv(qkv_b, conv_b, w_conv_b)
    q_b, k_b, v_b = jnp.split(jax.nn.silu(u_b), [GDN_QK_WIDTH, 2 * GDN_QK_WIDTH], axis=-1)
    beta = jax.nn.sigmoid(beta_b.astype(f32))
    g = -jnp.exp(a_log_b.astype(f32)) * jax.nn.softplus(decay_b.astype(f32) + dt_bias_b.astype(f32))
    o_b, delta_new = _gated_delta_chunked(_l2norm(heads(q_b, GDN_HEADS)) * GDN_DK ** -0.5,
                                          _l2norm(heads(k_b, GDN_HEADS)), heads(v_b, GDN_HEADS), g, beta, delta_b)
    o_b = _rms_norm(o_b, g_norm_b) * jax.nn.silu(heads(z_b, GDN_HEADS).astype(f32))
    o_b = o_b.reshape(bsz, t, GDN_WIDTH).astype(x.dtype)

    u_c, conv_c_new = _causal_conv(xbc_c, conv_c, w_conv_c, b_conv_c)
    x_c, b_c, c_c = jnp.split(jax.nn.silu(u_c), [SSD_WIDTH, SSD_WIDTH + SSD_GROUPS * SSD_STATE], axis=-1)
    rep = SSD_HEADS // SSD_GROUPS
    b_c = jnp.repeat(heads(b_c, SSD_GROUPS), rep, axis=2)
    c_c = jnp.repeat(heads(c_c, SSD_GROUPS), rep, axis=2)
    x_c = heads(x_c, SSD_HEADS).astype(f32)
    dt = jax.nn.softplus(dt_c.astype(f32) + dt_bias_c.astype(f32))
    y_c, ssm_new = _ssd_chunked(x_c, dt, -jnp.exp(a_log_c.astype(f32)), b_c, c_c, ssm_c)
    y_c = (y_c + d_skip_c.astype(f32)[:, None] * x_c).reshape(bsz, t, SSD_WIDTH) * jax.nn.silu(z_c.astype(f32))
    o_c = _rms_norm(y_c.reshape(bsz, t, SSD_GROUPS, -1), g_norm_c.reshape(SSD_GROUPS, -1))
    o_c = o_c.reshape(bsz, t, SSD_WIDTH).astype(x.dtype)

    mix = jnp.concatenate([o_a, o_b, o_c], axis=-1) @ w_out
    x = _layer_norm(DEEPNORM_ALPHA * x + mix, ln1_g, ln1_b)
    ffn = _moe(x.reshape(bsz * t, -1), router_w, router_b, w_gate, b_gate, w_up, b_up, w_down, b_down).reshape(x.shape)
    x = _layer_norm(DEEPNORM_ALPHA * x + ffn, ln2_g, ln2_b)
    x = x + jax.nn.sigmoid(x @ ple_gate_w + ple_gate_b) * (p @ ple_proj_w)
    return x, (k_a, v_a, conv_b_new, delta_new.astype(delta_b.dtype), conv_c_new, ssm_new.astype(ssm_c.dtype))


def _dt_bias(key, shape):
    dt = jnp.exp(jax.random.uniform(key, shape, jnp.float32, math.log(1e-3), math.log(1e-1)))
    return dt + jnp.log(-jnp.expm1(-dt))


def setup_inputs(seed: int = 0) -> dict:
    key = jax.random.key(seed)
    ks = iter(jax.random.split(key, 48))
    f32 = jnp.float32

    def nrm(shape, scale):
        return jax.random.normal(next(ks), shape, f32) * scale

    n_pages = PAST_LEN // PAGE_SIZE
    n_used = DEC_BATCH * n_pages
    n_pool = n_used + max(1, n_used // 4)
    page_table = jax.random.permutation(next(ks), n_pool)[:n_used].reshape(DEC_BATCH, n_pages).astype(jnp.int32)
    L = DEPTH
    return {
        'x_prompt': nrm((BATCH, SEQ, D_MODEL), 1.0),
        'x_sample': nrm((DEC_BATCH, DEC_SEQ, D_MODEL), 1.0),
        'cache_k_a': nrm((L, n_pool, PAGE_SIZE, SB_HEADS, HEAD_DIM), 1.0),
        'cache_v_a': nrm((L, n_pool, PAGE_SIZE, SB_HEADS, HEAD_DIM), 1.0),
        'state_conv_b': nrm((L, DEC_BATCH, CONV_WIDTH - 1, GDN_CONV_DIM), 1.0),
        'state_delta_b': nrm((L, DEC_BATCH, GDN_HEADS, GDN_DK, GDN_DV), 0.1),
        'state_conv_c': nrm((L, DEC_BATCH, CONV_WIDTH - 1, SSD_XBC), 1.0),
        'state_ssm_c': nrm((L, DEC_BATCH, SSD_HEADS, SSD_HEAD_DIM, SSD_STATE), 0.1),
        'page_table': page_table,
        'p_prompt': nrm((L, BATCH, SEQ, PLE_DIM), 1.0),
        'p_sample': nrm((L, DEC_BATCH, DEC_SEQ, PLE_DIM), 1.0),
        'ln_in_g': 1.0 + nrm((D_MODEL,), 0.02),
        'ln_in_b': nrm((D_MODEL,), 0.02),
        'w_in': nrm((L, D_MODEL, D_IN), D_MODEL ** -0.5),
        'sb_bias': SB_BIAS_INIT + nrm((L, SB_HEADS), 0.1),
        'w_conv_b': nrm((L, CONV_WIDTH, GDN_CONV_DIM), CONV_WIDTH ** -0.5),
        'a_log_b': jnp.log(jax.random.uniform(next(ks), (L, GDN_HEADS), f32, 1.0, 16.0)),
        'dt_bias_b': _dt_bias(next(ks), (L, GDN_HEADS)),
        'g_norm_b': 1.0 + nrm((L, GDN_DV), 0.02),
        'w_conv_c': nrm((L, CONV_WIDTH, SSD_XBC), CONV_WIDTH ** -0.5),
        'b_conv_c': nrm((L, SSD_XBC), 0.02),
        'a_log_c': jnp.log(jax.random.uniform(next(ks), (L, SSD_HEADS), f32, 1.0, 16.0)),
        'dt_bias_c': _dt_bias(next(ks), (L, SSD_HEADS)),
        'd_skip_c': 1.0 + nrm((L, SSD_HEADS), 0.1),
        'g_norm_c': 1.0 + nrm((L, SSD_WIDTH), 0.02),
        'w_out': nrm((L, MIX_WIDTH, D_MODEL), MIX_WIDTH ** -0.5 * DEEPNORM_BETA),
        'ln1_g': 1.0 + nrm((L, D_MODEL), 0.02),
        'ln1_b': nrm((L, D_MODEL), 0.02),
        'router_w': nrm((L, D_MODEL, N_EXPERTS), D_MODEL ** -0.5),
        'router_b': nrm((L, N_EXPERTS), 0.01),
        'w_gate': nrm((L, N_EXPERTS, D_MODEL, D_EXPERT), D_MODEL ** -0.5),
        'b_gate': nrm((L, N_EXPERTS, D_EXPERT), 0.02),
        'w_up': nrm((L, N_EXPERTS, D_MODEL, D_EXPERT), D_MODEL ** -0.5),
        'b_up': nrm((L, N_EXPERTS, D_EXPERT), 0.02),
        'w_down': nrm((L, N_EXPERTS, D_EXPERT, D_MODEL), D_EXPERT ** -0.5 * DEEPNORM_BETA),
        'b_down': nrm((L, N_EXPERTS, D_MODEL), 0.02),
        'ln2_g': 1.0 + nrm((L, D_MODEL), 0.02),
        'ln2_b': nrm((L, D_MODEL), 0.02),
        'ple_gate_w': nrm((L, D_MODEL, D_MODEL), D_MODEL ** -0.5),
        'ple_gate_b': nrm((L, D_MODEL), 0.02),
        'ple_proj_w': nrm((L, PLE_DIM, D_MODEL), PLE_DIM ** -0.5 * DEEPNORM_BETA),
    }


def reference(x_prompt, x_sample, cache_k_a, cache_v_a, state_conv_b, state_delta_b, state_conv_c, state_ssm_c,
              page_table, p_prompt, p_sample, ln_in_g, ln_in_b, w_in, sb_bias, w_conv_b, a_log_b, dt_bias_b, g_norm_b,
              w_conv_c, b_conv_c, a_log_c, dt_bias_c, d_skip_c, g_norm_c, w_out, ln1_g, ln1_b,
              router_w, router_b, w_gate, b_gate, w_up, b_up, w_down, b_down, ln2_g, ln2_b,
              ple_gate_w, ple_gate_b, ple_proj_w):
    f32 = jnp.float32

    def layer_weights(i):
        return (w_in[i], sb_bias[i], w_conv_b[i], a_log_b[i], dt_bias_b[i], g_norm_b[i],
                w_conv_c[i], b_conv_c[i], a_log_c[i], dt_bias_c[i], d_skip_c[i], g_norm_c[i],
                w_out[i], ln1_g[i], ln1_b[i], router_w[i], router_b[i], w_gate[i], b_gate[i], w_up[i], b_up[i],
                w_down[i], b_down[i], ln2_g[i], ln2_b[i], ple_gate_w[i], ple_gate_b[i], ple_proj_w[i])

    def run_group(x, p, past_of_layer):
        x = _layer_norm(x, ln_in_g, ln_in_b)
        new = []
        for i in range(DEPTH):
            x, st = _trunk_layer(x, p[i], *past_of_layer(i), *layer_weights(i))
            new.append(st)
        return x, [jnp.stack([s[j] for s in new]) for j in range(6)]

    nb = x_prompt.shape[0]
    pdt = x_prompt.dtype

    def prompt_past(i):
        return (jnp.zeros((nb, 0, SB_HEADS, HEAD_DIM), pdt), jnp.zeros((nb, 0, SB_HEADS, HEAD_DIM), pdt),
                jnp.zeros((nb, CONV_WIDTH - 1, GDN_CONV_DIM), pdt), jnp.zeros((nb, GDN_HEADS, GDN_DK, GDN_DV), f32),
                jnp.zeros((nb, CONV_WIDTH - 1, SSD_XBC), pdt), jnp.zeros((nb, SSD_HEADS, SSD_HEAD_DIM, SSD_STATE), f32))

    def sample_past(i):
        return (_gather_pages(cache_k_a[i], page_table), _gather_pages(cache_v_a[i], page_table),
                state_conv_b[i], state_delta_b[i], state_conv_c[i], state_ssm_c[i])

    y_prompt, new_prompt = run_group(x_prompt, p_prompt, prompt_past)
    y_sample, new_sample = run_group(x_sample, p_sample, sample_past)
    k_a_p, v_a_p, conv_b_p, delta_b_p, conv_c_p, ssm_c_p = new_prompt
    k_a_s, v_a_s, conv_b_s, delta_b_s, conv_c_s, ssm_c_s = new_sample
    return (y_prompt, y_sample, k_a_p, v_a_p, conv_b_p, delta_b_p, conv_c_p, ssm_c_p,
            k_a_s, v_a_s, conv_b_s, delta_b_s, conv_c_s, ssm_c_s)
```

```python
import functools
import math

import jax
import jax.numpy as jnp
import numpy as np
from jax import lax
from jax.experimental import pallas as pl
from jax.experimental.pallas import tpu as pltpu

F32 = jnp.float32
BF16 = jnp.bfloat16

D_MODEL = 1024
DEPTH = 4
PAGE_SIZE = 128
HEAD_DIM = 64
CONV_WIDTH = 4
SB_WIDTH = 256
SB_HEADS = 4
GDN_WIDTH = 256
GDN_HEADS = 4
GDN_DK = 64
GDN_QK_WIDTH = 256
GDN_CONV_DIM = 768
GDN_CHUNK = 64
SSD_WIDTH = 512
SSD_HEAD_DIM = 64
SSD_HEADS = 8
SSD_GROUPS = 2
SSD_STATE = 128
SSD_XBC = 1024
SSD_CHUNK = 128
N_EXPERTS = 32
TOP_K = 4
D_EXPERT = 512
SWIGLU_ALPHA = 1.702
SWIGLU_LIMIT = 7.0
PLE_DIM = 256
DEEPNORM_ALPHA = (2 * DEPTH) ** 0.25
LN_EPS = 1e-5
RMS_EPS = 1e-6

_REF_WIDTHS = (256, 256, 256, 768, 256, 4, 4, 512, 1024, 8)
_REF_OFF = tuple(int(v) for v in np.cumsum((0,) + _REF_WIDTHS))
OFF_Q, OFF_K, OFF_V, OFF_QKVB, OFF_ZB, OFF_ZC, OFF_XBC, OFF_SMALL = 0, 256, 512, 768, 1536, 1792, 2304, 3328
U_WIDTH = 3456
SMALL_WIDTH = 128
_IN_PERM = np.concatenate([
    np.arange(_REF_OFF[0], _REF_OFF[5]),
    np.arange(_REF_OFF[7], _REF_OFF[9]),
    np.arange(_REF_OFF[5], _REF_OFF[7]),
    np.arange(_REF_OFF[9], _REF_OFF[10]),
])

LANE = 128
VMEM_LIMIT = 48 * 1024 * 1024
MOE_BLOCK_PROMPT = 256
SB_BLOCK = 256
DEC_PAGES_PER_STEP = 8


def _params(sem):
    return pltpu.CompilerParams(dimension_semantics=sem, vmem_limit_bytes=VMEM_LIMIT)


def _row_block(n):
    return n if n < 512 else 512


def _layer_norm_f32(x, g, b):
    mu = jnp.mean(x, axis=-1, keepdims=True)
    xc = x - mu
    var = jnp.mean(xc * xc, axis=-1, keepdims=True)
    return xc * lax.rsqrt(var + LN_EPS) * g + b


def _softplus(z):
    return jnp.maximum(z, 0.0) + jnp.log1p(jnp.exp(-jnp.abs(z)))


def _split_dot(x, w_bf16):
    hi = x.astype(BF16)
    lo = (x - hi.astype(F32)).astype(BF16)
    return (jnp.dot(hi, w_bf16, preferred_element_type=F32) + jnp.dot(lo, w_bf16, preferred_element_type=F32))


def _ln_body(x_ref, g_ref, b_ref, o_ref):
    o_ref[...] = _layer_norm_f32(x_ref[...], g_ref[...], b_ref[...])


def _ln_call(x, g, b):
    n, d = x.shape
    bm = _row_block(n)
    return pl.pallas_call(
        _ln_body, grid=(n // bm,),
        in_specs=[pl.BlockSpec((bm, d), lambda i: (i, 0)),
                  pl.BlockSpec((1, d), lambda i: (0, 0)), pl.BlockSpec((1, d), lambda i: (0, 0))],
        out_specs=pl.BlockSpec((bm, d), lambda i: (i, 0)),
        out_shape=jax.ShapeDtypeStruct((n, d), F32),
        compiler_params=_params(("parallel",)), name="ln_in",
    )(x, g.reshape(1, d), b.reshape(1, d))


_PROJ_CHUNK = 384


def _proj_in_body(x_ref, w_ref, o_ref):
    xb = x_ref[...].astype(BF16)
    for j in range(0, U_WIDTH, _PROJ_CHUNK):
        o_ref[:, j:j + _PROJ_CHUNK] = jnp.dot(xb, w_ref[:, j:j + _PROJ_CHUNK], preferred_element_type=F32)


def _proj_in_call(x, w_bf16):
    n, d = x.shape
    bm = _row_block(n)
    return pl.pallas_call(
        _proj_in_body, grid=(n // bm,),
        in_specs=[pl.BlockSpec((bm, d), lambda i: (i, 0)),
                  pl.BlockSpec((d, U_WIDTH), lambda i: (0, 0))],
        out_specs=pl.BlockSpec((bm, U_WIDTH), lambda i: (i, 0)),
        out_shape=jax.ShapeDtypeStruct((n, U_WIDTH), F32),
        compiler_params=_params(("parallel",)), name="proj_in",
    )(x, w_bf16)


def _cumsum_matrix(n):
    j = np.arange(n)[:, None]
    s = np.arange(n)[None, :]
    return jnp.asarray(np.concatenate([(j > s), np.ones((n, LANE), bool)], axis=1), BF16)


def _sb_prefill_body(bias_ref, q_ref, k_ref, v_ref, u_ref, o_ref, qh_ref, acc_ref, carry_ref):
    blk = q_ref.shape[0]
    i = pl.program_id(1)
    q = q_ref[...] * (HEAD_DIM ** -0.5)
    lane_head = lax.broadcasted_iota(jnp.int32, (1, SB_WIDTH), 1) // HEAD_DIM
    for h in range(SB_HEADS):
        qh_ref[h] = jnp.where(lane_head == h, q, 0.0).astype(BF16)
    acc_ref[...] = jnp.zeros_like(acc_ref)
    carry_ref[...] = jnp.zeros_like(carry_ref)
    rows = lax.broadcasted_iota(jnp.int32, (blk, blk), 0)
    cols = lax.broadcasted_iota(jnp.int32, (blk, blk), 1)

    def key_block(j, diagonal):
        start = pl.multiple_of(j * blk, blk)
        kj = k_ref[pl.ds(start, blk), :].astype(BF16)
        vj = v_ref[pl.ds(start, blk), :].astype(BF16)
        for h in range(SB_HEADS):
            z = lax.dot_general(qh_ref[h], kj, (((1,), (1,)), ((), ())), preferred_element_type=F32) + bias_ref[h]
            log_keep = -_softplus(z)
            logit = z + log_keep
            if diagonal:
                visible = cols < rows
                log_keep = jnp.where(visible, log_keep, 0.0)
            lt = _split_dot(log_keep, u_ref[...])
            c = carry_ref[h]
            later = lt[:, :blk] + jnp.concatenate([c] * (blk // LANE), axis=1)
            w = jnp.exp(logit + later)
            if diagonal:
                w = jnp.where(visible, w, 0.0)
            acc_ref[h] += jnp.dot(w.astype(BF16), vj, preferred_element_type=F32)
            carry_ref[h] = c + lt[:, blk:]

    key_block(i, True)

    def earlier(t, carry):
        key_block(i - 1 - t, False)
        return carry

    lax.fori_loop(0, i, earlier, 0)
    out = jnp.zeros((blk, SB_WIDTH), F32)
    for h in range(SB_HEADS):
        out = out + jnp.where(lane_head == h, acc_ref[h], 0.0)
    o_ref[...] = out


def _sb_prefill_call(u, bias, bsz, t):
    blk = SB_BLOCK
    nq = t // blk
    grid_spec = pltpu.PrefetchScalarGridSpec(
        num_scalar_prefetch=0, grid=(bsz, nq),
        in_specs=[pl.BlockSpec(memory_space=pltpu.SMEM),
                  pl.BlockSpec((blk, SB_WIDTH), lambda b, i: (b * nq + i, OFF_Q // SB_WIDTH)),
                  pl.BlockSpec((t, SB_WIDTH), lambda b, i: (b, OFF_K // SB_WIDTH)),
                  pl.BlockSpec((t, SB_WIDTH), lambda b, i: (b, OFF_V // SB_WIDTH)),
                  pl.BlockSpec((blk, blk + LANE), lambda b, i: (0, 0))],
        out_specs=pl.BlockSpec((blk, SB_WIDTH), lambda b, i: (b * nq + i, 0)),
        scratch_shapes=[pltpu.VMEM((SB_HEADS, blk, SB_WIDTH), BF16),
                        pltpu.VMEM((SB_HEADS, blk, SB_WIDTH), F32),
                        pltpu.VMEM((SB_HEADS, blk, LANE), F32)])
    return pl.pallas_call(
        _sb_prefill_body, grid_spec=grid_spec,
        out_shape=jax.ShapeDtypeStruct((bsz * t, SB_WIDTH), F32),
        compiler_params=_params(("parallel", "arbitrary")), name="sb_prefill",
    )(bias, u, u, u, _cumsum_matrix(blk))


def _sb_decode_body(pt_ref, q_ref, bias_ref, u_ref, *refs):
    g = DEC_PAGES_PER_STEP
    k_refs, v_refs = refs[:g], refs[g:2 * g]
    o_ref, acc_ref, carry_ref = refs[2 * g], refs[2 * g + 1], refs[2 * g + 2]
    s = pl.program_id(1)

    @pl.when(s == 0)
    def _():
        acc_ref[...] = jnp.zeros_like(acc_ref)
        carry_ref[...] = jnp.zeros_like(carry_ref)

    row = lax.broadcasted_iota(jnp.int32, (8, SB_WIDTH), 0)
    lane_head = lax.broadcasted_iota(jnp.int32, (8, SB_WIDTH), 1) // HEAD_DIM
    own = row == lane_head
    qm = jnp.where(own, q_ref[...] * (HEAD_DIM ** -0.5), 0.0).astype(BF16)
    acc = acc_ref[...]
    carry = carry_ref[...]
    for c in range(g):
        kc = k_refs[c][...].astype(BF16)
        vc = v_refs[c][...].astype(BF16)
        z = lax.dot_general(qm, kc, (((1,), (1,)), ((), ())), preferred_element_type=F32) + bias_ref[...]
        log_keep = -_softplus(z)
        lt = _split_dot(log_keep, u_ref[...])
        w = jnp.exp(z + log_keep + lt[:, :LANE] + carry)
        acc = acc + jnp.dot(w.astype(BF16), vc, preferred_element_type=F32)
        carry = carry + lt[:, LANE:]
    acc_ref[...] = acc
    carry_ref[...] = carry

    @pl.when(s == pl.num_programs(1) - 1)
    def _():
        o_ref[...] = jnp.sum(jnp.where(own, acc, 0.0), axis=0, keepdims=True)


def _sb_decode_call(q, bias, cache_k, cache_v, page_table, layer):
    bsz = q.shape[0]
    n_pages = page_table.shape[1]
    g = DEC_PAGES_PER_STEP
    n_steps = n_pages // g
    bias8 = jnp.zeros((8, LANE), F32).at[:SB_HEADS].set(jnp.broadcast_to(bias.astype(F32)[:, None], (SB_HEADS, LANE)))

    def page_spec(c):
        return pl.BlockSpec((None, None, PAGE_SIZE, SB_WIDTH),
                            lambda b, s, pt: (layer, pt[b, n_pages - 1 - (s * g + c)], 0, 0))

    grid_spec = pltpu.PrefetchScalarGridSpec(
        num_scalar_prefetch=1, grid=(bsz, n_steps),
        in_specs=[pl.BlockSpec((None, 1, SB_WIDTH), lambda b, s, pt: (b, 0, 0)),
                  pl.BlockSpec((8, LANE), lambda b, s, pt: (0, 0)),
                  pl.BlockSpec((PAGE_SIZE, 2 * LANE), lambda b, s, pt: (0, 0))]
                 + [page_spec(c) for c in range(g)] + [page_spec(c) for c in range(g)],
        out_specs=pl.BlockSpec((None, 1, SB_WIDTH), lambda b, s, pt: (b, 0, 0)),
        scratch_shapes=[pltpu.VMEM((8, SB_WIDTH), F32), pltpu.VMEM((8, LANE), F32)])
    out = pl.pallas_call(
        _sb_decode_body, grid_spec=grid_spec,
        out_shape=jax.ShapeDtypeStruct((bsz, 1, SB_WIDTH), F32),
        compiler_params=_params(("parallel", "arbitrary")), name="sb_decode",
    )(page_table, q.reshape(bsz, 1, SB_WIDTH), bias8, _cumsum_matrix(PAGE_SIZE),
      *([cache_k] * g), *([cache_v] * g))
    return out.reshape(bsz, SB_WIDTH)


def _proj_out_body(x_ref, oa_ref, ob_ref, oc_ref, w_ref, g_ref, b_ref, rw_ref, rb_ref, x1_ref, lg_ref):
    mix = jnp.dot(oa_ref[...].astype(BF16), w_ref[0:256, :], preferred_element_type=F32)
    mix += jnp.dot(ob_ref[...].astype(BF16), w_ref[256:512, :], preferred_element_type=F32)
    mix += jnp.dot(oc_ref[...].astype(BF16), w_ref[512:1024, :], preferred_element_type=F32)
    x1 = _layer_norm_f32(DEEPNORM_ALPHA * x_ref[...] + mix, g_ref[...], b_ref[...])
    x1_ref[...] = x1
    lg_ref[...] = jnp.dot(x1, rw_ref[...], preferred_element_type=F32, precision=lax.Precision.HIGHEST) + rb_ref[...]


def _proj_out_call(x, o_a, o_b, o_c, w_bf16, g, b, router_w, router_b):
    n, d = x.shape
    bm = _row_block(n)
    rw = jnp.pad(router_w, ((0, 0), (0, LANE - N_EXPERTS)))
    rb = jnp.pad(router_b, (0, LANE - N_EXPERTS)).reshape(1, LANE)
    row = lambda w: pl.BlockSpec((bm, w), lambda i: (i, 0))
    full = lambda r, c: pl.BlockSpec((r, c), lambda i: (0, 0))
    return pl.pallas_call(
        _proj_out_body, grid=(n // bm,),
        in_specs=[row(d), row(SB_WIDTH), row(GDN_WIDTH), row(SSD_WIDTH), full(d, d), full(1, d), full(1, d),
                  full(d, LANE), full(1, LANE)],
        out_specs=[row(d), row(LANE)],
        out_shape=[jax.ShapeDtypeStruct((n, d), F32), jax.ShapeDtypeStruct((n, LANE), F32)],
        compiler_params=_params(("parallel",)), name="proj_out",
    )(x, o_a, o_b, o_c, w_bf16, g.reshape(1, d), b.reshape(1, d), rw, rb)


def _moe_body(be_ref, nb_ref, rows_ref, wg_ref, bg_ref, wu_ref, bu_ref, wd_ref, bd_ref, o_ref, wg_s, wu_s, wd_s):
    i = pl.program_id(0)
    prev = be_ref[jnp.maximum(i - 1, 0)]

    @pl.when(jnp.logical_or(i == 0, be_ref[i] != prev))
    def _():
        wg_s[...] = wg_ref[...].astype(BF16)
        wu_s[...] = wu_ref[...].astype(BF16)
        wd_s[...] = wd_ref[...].astype(BF16)

    @pl.when(i < nb_ref[0])
    def _():
        xb = rows_ref[...].astype(BF16)
        glu = jnp.minimum(jnp.dot(xb, wg_s[...], preferred_element_type=F32) + bg_ref[...], SWIGLU_LIMIT)
        lin = jnp.clip(jnp.dot(xb, wu_s[...], preferred_element_type=F32) + bu_ref[...], -SWIGLU_LIMIT, SWIGLU_LIMIT)
        act = glu * jax.nn.sigmoid(SWIGLU_ALPHA * glu) * (lin + 1.0)
        o_ref[...] = jnp.dot(act.astype(BF16), wd_s[...], preferred_element_type=F32) + bd_ref[...]

    @pl.when(i >= nb_ref[0])
    def _():
        o_ref[...] = jnp.zeros_like(o_ref)


def _moe_call(rows_in, block_expert, n_used, w_gate, b_gate, w_up, b_up, w_down, b_down, layer, blk):
    n_rows, d = rows_in.shape
    n_blocks = n_rows // blk
    wspec = lambda r, c: pl.BlockSpec((None, None, r, c), lambda i, be, nb: (layer, be[i], 0, 0))
    grid_spec = pltpu.PrefetchScalarGridSpec(
        num_scalar_prefetch=2, grid=(n_blocks,),
        in_specs=[pl.BlockSpec((blk, d), lambda i, be, nb: (i, 0)),
                  wspec(d, D_EXPERT), wspec(1, D_EXPERT), wspec(d, D_EXPERT), wspec(1, D_EXPERT),
                  wspec(D_EXPERT, d), wspec(1, d)],
        out_specs=pl.BlockSpec((blk, d), lambda i, be, nb: (i, 0)),
        scratch_shapes=[pltpu.VMEM((d, D_EXPERT), BF16), pltpu.VMEM((d, D_EXPERT), BF16),
                        pltpu.VMEM((D_EXPERT, d), BF16)])
    depth = w_gate.shape[0]
    return pl.pallas_call(
        _moe_body, grid_spec=grid_spec,
        out_shape=jax.ShapeDtypeStruct((n_rows, d), F32),
        compiler_params=_params(("arbitrary",)), name="moe_experts",
    )(block_expert, n_used, rows_in, w_gate, b_gate.reshape(depth, N_EXPERTS, 1, D_EXPERT),
      w_up, b_up.reshape(depth, N_EXPERTS, 1, D_EXPERT), w_down, b_down.reshape(depth, N_EXPERTS, 1, d))


def _route(logits, blk):
    n_tok = logits.shape[0]
    top_logit, top_idx = lax.top_k(logits, TOP_K)
    gate = jax.nn.softmax(top_logit, axis=-1)
    n_assign = n_tok * TOP_K
    n_blocks = -(-(n_assign + N_EXPERTS * (blk - 1)) // blk)
    expert = top_idx.reshape(-1)
    order = jnp.argsort(expert)
    sorted_expert = expert[order]
    token = order // TOP_K
    counts = jnp.zeros((N_EXPERTS,), jnp.int32).at[expert].add(1)
    padded = (counts + blk - 1) // blk * blk
    padded_end = jnp.cumsum(padded)
    rank = jnp.arange(n_assign) - (jnp.cumsum(counts) - counts)[sorted_expert]
    row = (padded_end - padded)[sorted_expert] + rank
    src = jnp.zeros((n_blocks * blk,), jnp.int32).at[row].set(token.astype(jnp.int32))
    dest = jnp.zeros((n_assign,), jnp.int32).at[order].set(row.astype(jnp.int32)).reshape(n_tok, TOP_K)
    block_expert = jnp.minimum(jnp.searchsorted(padded_end, jnp.arange(n_blocks) * blk, side='right'),
                               N_EXPERTS - 1).astype(jnp.int32)
    n_used = (padded_end[-1] // blk).astype(jnp.int32).reshape(1)
    return gate, src, dest, block_expert, n_used


def _post_body(x1_ref, rows_ref, gate_ref, p_ref, g_ref, b_ref, wg_ref, bg_ref, wp_ref, o_ref):
    d = x1_ref.shape[1]
    gate = gate_ref[...]
    ffn = gate[:, 0:1] * rows_ref[:, 0:d]
    for k in range(1, TOP_K):
        ffn += gate[:, k:k + 1] * rows_ref[:, k * d:(k + 1) * d]
    x2 = _layer_norm_f32(DEEPNORM_ALPHA * x1_ref[...] + ffn, g_ref[...], b_ref[...])
    gate_logit = jnp.dot(x2.astype(BF16), wg_ref[...], preferred_element_type=F32) + bg_ref[...]
    emb = jnp.dot(p_ref[...].astype(BF16), wp_ref[...], preferred_element_type=F32)
    o_ref[...] = x2 + jax.nn.sigmoid(gate_logit) * emb


def _post_call(x1, rows4, gate, p, g, b, wg_bf16, bg, wp_bf16):
    n, d = x1.shape
    bm = _row_block(n)
    row = lambda w: pl.BlockSpec((bm, w), lambda i: (i, 0))
    full = lambda r, c: pl.BlockSpec((r, c), lambda i: (0, 0))
    return pl.pallas_call(
        _post_body, grid=(n // bm,),
        in_specs=[row(d), row(TOP_K * d), row(TOP_K), row(PLE_DIM), full(1, d), full(1, d),
                  full(d, d), full(1, d), full(PLE_DIM, d)],
        out_specs=row(d),
        out_shape=jax.ShapeDtypeStruct((n, d), F32),
        compiler_params=_params(("parallel",)), name="post_ffn",
    )(x1, rows4, gate, p, g.reshape(1, d), b.reshape(1, d), wg_bf16, bg.reshape(1, d), wp_bf16)


def _rms_norm(x, g):
    return x * lax.rsqrt(jnp.mean(x * x, axis=-1, keepdims=True) + RMS_EPS) * g


def _l2norm(x):
    return x * lax.rsqrt(jnp.sum(x * x, axis=-1, keepdims=True) + RMS_EPS)


def _causal_conv(u, buf, w, b=None):
    full = jnp.concatenate([buf.astype(u.dtype), u], axis=1)
    out = lax.conv_general_dilated(full, w[:, None, :].astype(u.dtype), window_strides=(1,), padding='VALID',
                                   dimension_numbers=('NWC', 'WIO', 'NWC'), feature_group_count=u.shape[-1])
    if b is not None:
        out = out + b
    return out, full[:, -(CONV_WIDTH - 1):]


def _to_chunks(u, size, pad):
    u = jnp.pad(u, [(0, 0), (0, pad)] + [(0, 0)] * (u.ndim - 2))
    bsz, tp = u.shape[:2]
    u = u.reshape(bsz, tp // size, size, *u.shape[2:])
    return u.transpose((1, 0, 3, 2) + tuple(range(4, u.ndim)))


def _from_chunks(y, t):
    nc, bsz, h, size, p = y.shape
    return y.transpose(1, 0, 3, 2, 4).reshape(bsz, nc * size, h, p)[:, :t]


def _gated_delta_chunked(q, k, v, g, beta, s0):
    t = q.shape[1]
    n_v = v.shape[-1]
    size = min(GDN_CHUNK, t)
    pad = (-t) % size
    qc, kc, vc, gc, bc = (_to_chunks(u.astype(F32), size, pad) for u in (q, k, v, g, beta))
    gcs = jnp.cumsum(gc, axis=-1)
    incl = jnp.tril(jnp.ones((size, size), bool))
    strict = jnp.tril(jnp.ones((size, size), bool), -1)
    decay = jnp.exp(jnp.where(incl, gcs[..., :, None] - gcs[..., None, :], -jnp.inf))
    kb = kc * bc[..., None]
    a_mat = jnp.where(strict, jnp.einsum('cbhlk,cbhsk->cbhls', kb, kc) * decay, 0.0)
    rhs = jnp.concatenate([vc * bc[..., None], kb * jnp.exp(gcs)[..., None]], axis=-1)
    sol = lax.linalg.triangular_solve(a_mat, rhs, left_side=True, lower=True, unit_diagonal=True)
    u_c, w_c = sol[..., :n_v], sol[..., n_v:]
    qk = jnp.einsum('cbhlk,cbhsk->cbhls', qc, kc) * decay
    q_dec = qc * jnp.exp(gcs)[..., None]
    k_dec = kc * jnp.exp(gcs[..., -1:] - gcs)[..., None]
    chunk_decay = jnp.exp(gcs[..., -1])

    def step(s, inp):
        qk_i, qd_i, kd_i, u_i, w_i, d_i = inp
        v_new = u_i - jnp.einsum('bhlk,bhkv->bhlv', w_i, s)
        o = jnp.einsum('bhlk,bhkv->bhlv', qd_i, s) + jnp.einsum('bhls,bhsv->bhlv', qk_i, v_new)
        return s * d_i[..., None, None] + jnp.einsum('bhlk,bhlv->bhkv', kd_i, v_new), o

    s_last, o = lax.scan(step, s0.astype(F32), (qk, q_dec, k_dec, u_c, w_c, chunk_decay))
    return _from_chunks(o, t), s_last


def _ssd_chunked(x, dt, a, b, c, h0):
    t = x.shape[1]
    size = min(SSD_CHUNK, t)
    pad = (-t) % size
    xc, ac, bc, cc = (_to_chunks(u.astype(F32), size, pad) for u in (x * dt[..., None], dt * a, b, c))
    acs = jnp.cumsum(ac, axis=-1)
    incl = jnp.tril(jnp.ones((size, size), bool))
    decay = jnp.exp(jnp.where(incl, acs[..., :, None] - acs[..., None, :], -jnp.inf))
    scores = jnp.einsum('cbhln,cbhsn->cbhls', cc, bc) * decay
    y_diag = jnp.einsum('cbhls,cbhsp->cbhlp', scores, xc)
    chunk_in = jnp.einsum('cbhsn,cbhsp->cbhpn', bc * jnp.exp(acs[..., -1:] - acs)[..., None], xc)

    def step(h, inp):
        d_i, in_i = inp
        return h * d_i[..., None, None] + in_i, h

    h_last, h_enter = lax.scan(step, h0.astype(F32), (jnp.exp(acs[..., -1]), chunk_in))
    y_off = jnp.einsum('cbhln,cbhpn->cbhlp', cc * jnp.exp(acs)[..., None], h_enter)
    return _from_chunks(y_diag + y_off, t), h_last


def _mixer_b(u, bsz, t, conv_b, delta_b, w_conv_b, a_log_b, dt_bias_b, g_norm_b):
    heads = lambda v, n: v.reshape(bsz, t, n, -1)
    qkv_b = u[:, OFF_QKVB:OFF_QKVB + GDN_CONV_DIM].reshape(bsz, t, -1)
    z_b = u[:, OFF_ZB:OFF_ZB + GDN_WIDTH].reshape(bsz, t, -1)
    beta_b = u[:, OFF_SMALL:OFF_SMALL + GDN_HEADS].reshape(bsz, t, -1)
    decay_b = u[:, OFF_SMALL + GDN_HEADS:OFF_SMALL + 2 * GDN_HEADS].reshape(bsz, t, -1)
    u_b, conv_b_new = _causal_conv(qkv_b, conv_b, w_conv_b)
    q_b, k_b, v_b = jnp.split(jax.nn.silu(u_b), [GDN_QK_WIDTH, 2 * GDN_QK_WIDTH], axis=-1)
    beta = jax.nn.sigmoid(beta_b)
    g = -jnp.exp(a_log_b) * jax.nn.softplus(decay_b + dt_bias_b)
    o_b, delta_new = _gated_delta_chunked(_l2norm(heads(q_b, GDN_HEADS)) * GDN_DK ** -0.5,
                                          _l2norm(heads(k_b, GDN_HEADS)), heads(v_b, GDN_HEADS), g, beta, delta_b)
    o_b = _rms_norm(o_b, g_norm_b) * jax.nn.silu(heads(z_b, GDN_HEADS))
    return o_b.reshape(bsz * t, GDN_WIDTH), conv_b_new, delta_new


def _mixer_c(u, bsz, t, conv_c, ssm_c, w_conv_c, b_conv_c, a_log_c, dt_bias_c, d_skip_c, g_norm_c):
    heads = lambda v, n: v.reshape(bsz, t, n, -1)
    z_c = u[:, OFF_ZC:OFF_ZC + SSD_WIDTH].reshape(bsz, t, -1)
    xbc_c = u[:, OFF_XBC:OFF_XBC + SSD_XBC].reshape(bsz, t, -1)
    dt_c = u[:, OFF_SMALL + 2 * GDN_HEADS:OFF_SMALL + 2 * GDN_HEADS + SSD_HEADS].reshape(bsz, t, -1)
    u_c, conv_c_new = _causal_conv(xbc_c, conv_c, w_conv_c, b_conv_c)
    x_c, b_c, c_c = jnp.split(jax.nn.silu(u_c), [SSD_WIDTH, SSD_WIDTH + SSD_GROUPS * SSD_STATE], axis=-1)
    rep = SSD_HEADS // SSD_GROUPS
    b_c = jnp.repeat(heads(b_c, SSD_GROUPS), rep, axis=2)
    c_c = jnp.repeat(heads(c_c, SSD_GROUPS), rep, axis=2)
    x_c = heads(x_c, SSD_HEADS)
    dt = jax.nn.softplus(dt_c + dt_bias_c)
    y_c, ssm_new = _ssd_chunked(x_c, dt, -jnp.exp(a_log_c), b_c, c_c, ssm_c)
    y_c = (y_c + d_skip_c[:, None] * x_c).reshape(bsz, t, SSD_WIDTH) * jax.nn.silu(z_c)
    o_c = _rms_norm(y_c.reshape(bsz, t, SSD_GROUPS, -1), g_norm_c.reshape(SSD_GROUPS, -1))
    return o_c.reshape(bsz * t, SSD_WIDTH), conv_c_new, ssm_new


def kernel(x_prompt, x_sample, cache_k_a, cache_v_a, state_conv_b, state_delta_b, state_conv_c, state_ssm_c, page_table, p_prompt, p_sample, ln_in_g, ln_in_b, w_in, sb_bias, w_conv_b, a_log_b, dt_bias_b, g_norm_b, w_conv_c, b_conv_c, a_log_c, dt_bias_c, d_skip_c, g_norm_c, w_out, ln1_g, ln1_b, router_w, router_b, w_gate, b_gate, w_up, b_up, w_down, b_down, ln2_g, ln2_b, ple_gate_w, ple_gate_b, ple_proj_w):
    depth = w_in.shape[0]
    w_in_r = jnp.pad(w_in[:, :, _IN_PERM], ((0, 0), (0, 0), (0, U_WIDTH - _IN_PERM.size))).astype(BF16)
    w_out_b = w_out.astype(BF16)
    ple_gate_w_b = ple_gate_w.astype(BF16)
    ple_proj_w_b = ple_proj_w.astype(BF16)
    n_pool = cache_k_a.shape[1]
    cache_k = cache_k_a.reshape(depth, n_pool, PAGE_SIZE, SB_WIDTH)
    cache_v = cache_v_a.reshape(depth, n_pool, PAGE_SIZE, SB_WIDTH)

    def run_group(x, p, sample):
        bsz, t, d = x.shape
        n = bsz * t
        x = _ln_call(x.reshape(n, d), ln_in_g, ln_in_b)
        new = []
        for i in range(depth):
            u = _proj_in_call(x, w_in_r[i])
            k_a = u[:, OFF_K:OFF_K + SB_WIDTH].reshape(bsz, t, SB_HEADS, HEAD_DIM)
            v_a = u[:, OFF_V:OFF_V + SB_WIDTH].reshape(bsz, t, SB_HEADS, HEAD_DIM)
            if sample:
                o_a = _sb_decode_call(u[:, OFF_Q:OFF_Q + SB_WIDTH], sb_bias[i], cache_k, cache_v, page_table, i)
                conv_b, delta_b, conv_c, ssm_c = state_conv_b[i], state_delta_b[i], state_conv_c[i], state_ssm_c[i]
            else:
                o_a = _sb_prefill_call(u, sb_bias[i], bsz, t)
                conv_b = jnp.zeros((bsz, CONV_WIDTH - 1, GDN_CONV_DIM), F32)
                delta_b = jnp.zeros((bsz, GDN_HEADS, GDN_DK, HEAD_DIM), F32)
                conv_c = jnp.zeros((bsz, CONV_WIDTH - 1, SSD_XBC), F32)
                ssm_c = jnp.zeros((bsz, SSD_HEADS, SSD_HEAD_DIM, SSD_STATE), F32)
            o_b, conv_b_new, delta_new = _mixer_b(u, bsz, t, conv_b, delta_b, w_conv_b[i], a_log_b[i], dt_bias_b[i],
                                                  g_norm_b[i])
            o_c, conv_c_new, ssm_new = _mixer_c(u, bsz, t, conv_c, ssm_c, w_conv_c[i], b_conv_c[i], a_log_c[i],
                                                dt_bias_c[i], d_skip_c[i], g_norm_c[i])
            x1, logits = _proj_out_call(x, o_a, o_b, o_c, w_out_b[i], ln1_g[i], ln1_b[i], router_w[i], router_b[i])
            blk = MOE_BLOCK_PROMPT if n * TOP_K >= MOE_BLOCK_PROMPT * N_EXPERTS else 8
            gate, src, dest, block_expert, n_used = _route(logits[:, :N_EXPERTS], blk)
            rows_out = _moe_call(x1[src], block_expert, n_used, w_gate, b_gate, w_up, b_up, w_down, b_down, i, blk)
            rows4 = rows_out[dest.reshape(-1)].reshape(n, TOP_K * d)
            x = _post_call(x1, rows4, gate, p[i].reshape(n, PLE_DIM), ln2_g[i], ln2_b[i], ple_gate_w_b[i],
                           ple_gate_b[i], ple_proj_w_b[i])
            new.append((k_a, v_a, conv_b_new, delta_new, conv_c_new, ssm_new))
        return x.reshape(bsz, t, d), [jnp.stack([s[j] for s in new]) for j in range(6)]

    y_prompt, new_prompt = run_group(x_prompt, p_prompt, False)
    y_sample, new_sample = run_group(x_sample, p_sample, True)
    return (y_prompt, y_sample, *new_prompt, *new_sample)
```

```python
import functools

import jax
import jax.numpy as jnp
import numpy as np
from jax import lax
from jax.experimental import pallas as pl
from jax.experimental.pallas import tpu as pltpu

F32 = jnp.float32
BF16 = jnp.bfloat16

D_MODEL = 1024
DEPTH = 4
PAGE_SIZE = 128
HEAD_DIM = 64
CONV_WIDTH = 4
SB_WIDTH = 256
SB_HEADS = 4
GDN_WIDTH = 256
GDN_HEADS = 4
GDN_DK = 64
GDN_CONV_DIM = 768
GDN_CHUNK = 64
SSD_WIDTH = 512
SSD_HEADS = 8
SSD_GROUPS = 2
SSD_STATE = 128
SSD_XBC = 1024
SSD_CHUNK = 128
N_EXPERTS = 32
TOP_K = 4
D_EXPERT = 512
SWIGLU_ALPHA = 1.702
SWIGLU_LIMIT = 7.0
PLE_DIM = 256
DEEPNORM_ALPHA = (2 * DEPTH) ** 0.25
LN_EPS = 1e-5
RMS_EPS = 1e-6

_REF_WIDTHS = (256, 256, 256, 768, 256, 4, 4, 512, 1024, 8)
_REF_OFF = tuple(int(v) for v in np.cumsum((0,) + _REF_WIDTHS))
OFF_XBC, OFF_ZC, OFF_QKVB, OFF_Q, OFF_K, OFF_V, OFF_ZB, OFF_SMALL = 0, 1024, 1536, 2304, 2560, 2816, 3072, 3328
U_WIDTH = 3456
LANE = 128
SMALL_BETA, SMALL_DECAY, SMALL_DT = 0, GDN_HEADS, 2 * GDN_HEADS
_IN_PERM = np.concatenate([
    np.arange(_REF_OFF[8], _REF_OFF[9]),
    np.arange(_REF_OFF[7], _REF_OFF[8]),
    np.arange(_REF_OFF[3], _REF_OFF[4]),
    np.arange(_REF_OFF[0], _REF_OFF[3]),
    np.arange(_REF_OFF[4], _REF_OFF[5]),
    np.arange(_REF_OFF[5], _REF_OFF[7]),
    np.arange(_REF_OFF[9], _REF_OFF[10]),
])

VMEM_LIMIT = 48 * 1024 * 1024
MOE_BLOCK_PROMPT = 256
SB_BLOCK = 256
DEC_PAGES_PER_STEP = 8
CONV_PAD = 8
NEG_BIG = -1e30


def _params(sem):
    return pltpu.CompilerParams(dimension_semantics=sem, vmem_limit_bytes=VMEM_LIMIT)


def _row_block(n):
    return n if n < 512 else 512


def _layer_norm_f32(x, g, b):
    mu = jnp.mean(x, axis=-1, keepdims=True)
    xc = x - mu
    var = jnp.mean(xc * xc, axis=-1, keepdims=True)
    return xc * lax.rsqrt(var + LN_EPS) * g + b


def _softplus(z):
    return jnp.maximum(z, 0.0) + jnp.log1p(jnp.exp(-jnp.abs(z)))


def _silu(x):
    return x * jax.nn.sigmoid(x)


def _dot(a, b):
    return jnp.dot(a, b, preferred_element_type=F32)


def _dot_nt(a, b):
    return lax.dot_general(a, b, (((1,), (1,)), ((), ())), preferred_element_type=F32)


def _dot_tn(a, b):
    return lax.dot_general(a, b, (((0,), (0,)), ((), ())), preferred_element_type=F32)


def _split2(x):
    hi = x.astype(BF16)
    return hi, (x - hi.astype(F32)).astype(BF16)


def _split3(x):
    hi = x.astype(BF16)
    r = x - hi.astype(F32)
    mid = r.astype(BF16)
    return hi, mid, (r - mid.astype(F32)).astype(BF16)


def _split_dot(x, w_bf16):
    hi, lo = _split2(x)
    return _dot(hi, w_bf16) + _dot(lo, w_bf16)


def _dot3_left(x, w_bf16):
    hi, mid, lo = _split3(x)
    return _dot(hi, w_bf16) + (_dot(mid, w_bf16) + _dot(lo, w_bf16))


def _dot3_right(w_bf16, x):
    hi, mid, lo = _split3(x)
    return _dot(w_bf16, hi) + (_dot(w_bf16, mid) + _dot(w_bf16, lo))


def _mm3(a, b):
    ah, al = _split2(a)
    bh, bl = _split2(b)
    return _dot(ah, bh) + (_dot(ah, bl) + _dot(al, bh))


def _cumsum_matrix(n):
    j = np.arange(n)[:, None]
    s = np.arange(n)[None, :]
    return jnp.asarray(np.concatenate([(j > s), np.ones((n, LANE), bool)], axis=1), BF16)


def _tril_incl(n):
    return jnp.asarray(np.tril(np.ones((n, n), bool)), BF16)


def _expand_matrix(first_lane, heads, width):
    m = np.zeros((LANE, heads * width), bool)
    for h in range(heads):
        m[first_lane + h, h * width:(h + 1) * width] = True
    return jnp.asarray(m, BF16)


def _block_diag_ones(n, blk):
    i = np.arange(n)
    return jnp.asarray((i[:, None] // blk) == (i[None, :] // blk), BF16)


def _fold_matrix(heads, width):
    i = np.arange(heads * width)
    return jnp.asarray((i[:, None] % width) == np.arange(width)[None, :], BF16)


def _ln_body(x_ref, g_ref, b_ref, o_ref):
    o_ref[...] = _layer_norm_f32(x_ref[...], g_ref[...], b_ref[...])


def _ln_call(x, g, b):
    n, d = x.shape
    bm = _row_block(n)
    return pl.pallas_call(
        _ln_body, grid=(n // bm,),
        in_specs=[pl.BlockSpec((bm, d), lambda i: (i, 0)),
                  pl.BlockSpec((1, d), lambda i: (0, 0)), pl.BlockSpec((1, d), lambda i: (0, 0))],
        out_specs=pl.BlockSpec((bm, d), lambda i: (i, 0)),
        out_shape=jax.ShapeDtypeStruct((n, d), F32),
        compiler_params=_params(("parallel",)), name="ln_in",
    )(x, g.reshape(1, d), b.reshape(1, d))


_PROJ_CHUNK = 384


def _proj_in_body(x_ref, w_ref, o_ref):
    xb = x_ref[...].astype(BF16)
    for j in range(0, U_WIDTH, _PROJ_CHUNK):
        o_ref[:, j:j + _PROJ_CHUNK] = _dot(xb, w_ref[:, j:j + _PROJ_CHUNK])


def _proj_in_call(x, w_bf16):
    n, d = x.shape
    bm = _row_block(n)
    return pl.pallas_call(
        _proj_in_body, grid=(n // bm,),
        in_specs=[pl.BlockSpec((bm, d), lambda i: (i, 0)),
                  pl.BlockSpec((d, U_WIDTH), lambda i: (0, 0))],
        out_specs=pl.BlockSpec((bm, U_WIDTH), lambda i: (i, 0)),
        out_shape=jax.ShapeDtypeStruct((n, U_WIDTH), F32),
        compiler_params=_params(("parallel",)), name="proj_in",
    )(x, w_bf16)


def _sb_prefill_body(bias_ref, q_ref, k_ref, v_ref, u_ref, o_ref, qh_ref, acc_ref, carry_ref):
    blk = q_ref.shape[0]
    i = pl.program_id(1)
    q = q_ref[...] * (HEAD_DIM ** -0.5)
    lane_head = lax.broadcasted_iota(jnp.int32, (1, SB_WIDTH), 1) // HEAD_DIM
    for h in range(SB_HEADS):
        qh_ref[h] = jnp.where(lane_head == h, q, 0.0).astype(BF16)
    acc_ref[...] = jnp.zeros_like(acc_ref)
    carry_ref[...] = jnp.zeros_like(carry_ref)
    rows = lax.broadcasted_iota(jnp.int32, (blk, blk), 0)
    cols = lax.broadcasted_iota(jnp.int32, (blk, blk), 1)

    def key_block(j, diagonal):
        start = pl.multiple_of(j * blk, blk)
        kj = k_ref[pl.ds(start, blk), :].astype(BF16)
        vj = v_ref[pl.ds(start, blk), :].astype(BF16)
        for h in range(SB_HEADS):
            z = _dot_nt(qh_ref[h], kj) + bias_ref[h]
            log_keep = -_softplus(z)
            logit = z + log_keep
            if diagonal:
                visible = cols < rows
                log_keep = jnp.where(visible, log_keep, 0.0)
            lt = _split_dot(log_keep, u_ref[...])
            c = carry_ref[h]
            later = lt[:, :blk] + jnp.concatenate([c] * (blk // LANE), axis=1)
            w = jnp.exp(logit + later)
            if diagonal:
                w = jnp.where(visible, w, 0.0)
            acc_ref[h] += _dot(w.astype(BF16), vj)
            carry_ref[h] = c + lt[:, blk:]

    key_block(i, True)

    def earlier(t, carry):
        key_block(i - 1 - t, False)
        return carry

    lax.fori_loop(0, i, earlier, 0)
    out = jnp.zeros((blk, SB_WIDTH), F32)
    for h in range(SB_HEADS):
        out = out + jnp.where(lane_head == h, acc_ref[h], 0.0)
    o_ref[...] = out


def _sb_prefill_call(u, bias, bsz, t):
    blk = SB_BLOCK
    nq = t // blk
    grid_spec = pltpu.PrefetchScalarGridSpec(
        num_scalar_prefetch=0, grid=(bsz, nq),
        in_specs=[pl.BlockSpec(memory_space=pltpu.SMEM),
                  pl.BlockSpec((blk, SB_WIDTH), lambda b, i: (b * nq + i, OFF_Q // SB_WIDTH)),
                  pl.BlockSpec((t, SB_WIDTH), lambda b, i: (b, OFF_K // SB_WIDTH)),
                  pl.BlockSpec((t, SB_WIDTH), lambda b, i: (b, OFF_V // SB_WIDTH)),
                  pl.BlockSpec((blk, blk + LANE), lambda b, i: (0, 0))],
        out_specs=pl.BlockSpec((blk, SB_WIDTH), lambda b, i: (b * nq + i, 0)),
        scratch_shapes=[pltpu.VMEM((SB_HEADS, blk, SB_WIDTH), BF16),
                        pltpu.VMEM((SB_HEADS, blk, SB_WIDTH), F32),
                        pltpu.VMEM((SB_HEADS, blk, LANE), F32)])
    return pl.pallas_call(
        _sb_prefill_body, grid_spec=grid_spec,
        out_shape=jax.ShapeDtypeStruct((bsz * t, SB_WIDTH), F32),
        compiler_params=_params(("parallel", "arbitrary")), name="sb_prefill",
    )(bias, u, u, u, _cumsum_matrix(blk))


def _sb_decode_body(pt_ref, q_ref, bias_ref, u_ref, *refs):
    g = DEC_PAGES_PER_STEP
    k_refs, v_refs = refs[:g], refs[g:2 * g]
    o_ref, acc_ref, carry_ref = refs[2 * g], refs[2 * g + 1], refs[2 * g + 2]
    s = pl.program_id(1)

    @pl.when(s == 0)
    def _():
        acc_ref[...] = jnp.zeros_like(acc_ref)
        carry_ref[...] = jnp.zeros_like(carry_ref)

    row = lax.broadcasted_iota(jnp.int32, (8, SB_WIDTH), 0)
    lane_head = lax.broadcasted_iota(jnp.int32, (8, SB_WIDTH), 1) // HEAD_DIM
    own = row == lane_head
    qm = jnp.where(own, q_ref[...] * (HEAD_DIM ** -0.5), 0.0).astype(BF16)
    acc = acc_ref[...]
    carry = carry_ref[...]
    for c in range(g):
        kc = k_refs[c][...].astype(BF16)
        vc = v_refs[c][...].astype(BF16)
        z = _dot_nt(qm, kc) + bias_ref[...]
        log_keep = -_softplus(z)
        lt = _split_dot(log_keep, u_ref[...])
        w = jnp.exp(z + log_keep + lt[:, :LANE] + carry)
        acc = acc + _dot(w.astype(BF16), vc)
        carry = carry + lt[:, LANE:]
    acc_ref[...] = acc
    carry_ref[...] = carry

    @pl.when(s == pl.num_programs(1) - 1)
    def _():
        o_ref[...] = jnp.sum(jnp.where(own, acc, 0.0), axis=0, keepdims=True)


def _sb_decode_call(q, bias, cache_k, cache_v, page_table, layer):
    bsz = q.shape[0]
    n_pages = page_table.shape[1]
    g = DEC_PAGES_PER_STEP
    n_steps = n_pages // g
    bias8 = jnp.zeros((8, LANE), F32).at[:SB_HEADS].set(jnp.broadcast_to(bias.astype(F32)[:, None], (SB_HEADS, LANE)))

    def page_spec(c):
        return pl.BlockSpec((None, None, PAGE_SIZE, SB_WIDTH),
                            lambda b, s, pt: (layer, pt[b, n_pages - 1 - (s * g + c)], 0, 0))

    grid_spec = pltpu.PrefetchScalarGridSpec(
        num_scalar_prefetch=1, grid=(bsz, n_steps),
        in_specs=[pl.BlockSpec((None, 1, SB_WIDTH), lambda b, s, pt: (b, 0, 0)),
                  pl.BlockSpec((8, LANE), lambda b, s, pt: (0, 0)),
                  pl.BlockSpec((PAGE_SIZE, 2 * LANE), lambda b, s, pt: (0, 0))]
                 + [page_spec(c) for c in range(g)] + [page_spec(c) for c in range(g)],
        out_specs=pl.BlockSpec((None, 1, SB_WIDTH), lambda b, s, pt: (b, 0, 0)),
        scratch_shapes=[pltpu.VMEM((8, SB_WIDTH), F32), pltpu.VMEM((8, LANE), F32)])
    out = pl.pallas_call(
        _sb_decode_body, grid_spec=grid_spec,
        out_shape=jax.ShapeDtypeStruct((bsz, 1, SB_WIDTH), F32),
        compiler_params=_params(("parallel", "arbitrary")), name="sb_decode",
    )(page_table, q.reshape(bsz, 1, SB_WIDTH), bias8, _cumsum_matrix(PAGE_SIZE),
      *([cache_k] * g), *([cache_v] * g))
    return out.reshape(bsz, SB_WIDTH)


def _chunk_conv(x_ref, xfull_ref, wconv_ref, rows):
    xfull_ref[CONV_PAD:CONV_PAD + rows, :] = x_ref[...]
    acc = wconv_ref[CONV_WIDTH - 1:CONV_WIDTH, :] * xfull_ref[CONV_PAD:CONV_PAD + rows, :]
    for j in range(1, CONV_WIDTH):
        acc = acc + wconv_ref[CONV_WIDTH - 1 - j:CONV_WIDTH - j, :] * xfull_ref[CONV_PAD - j:CONV_PAD - j + rows, :]
    return acc


def _gdn_body(qkv_ref, z_ref, small_ref, wconv_ref, alog_ref, dtb_ref, gnorm_ref, conv0_ref, s0_ref,
              tril_ref, expb_ref, expg_ref, bd_ref, fold_ref, foldt_ref,
              o_ref, convn_ref, sn_ref, xfull_ref, s_ref, *, n_chunks, tv_last):
    L = GDN_CHUNK
    W = GDN_WIDTH
    c = pl.program_id(1)
    r2 = lax.broadcasted_iota(jnp.int32, (W, W), 0)
    c2 = lax.broadcasted_iota(jnp.int32, (W, W), 1)
    bd = (r2 // L) == (c2 // L)

    @pl.when(c == 0)
    def _():
        xfull_ref[0:CONV_PAD, :] = conv0_ref[...]
        s_ref[...] = jnp.where(bd, _dot3_left(s0_ref[...], foldt_ref[...]), 0.0)

    xs = _silu(_chunk_conv(qkv_ref, xfull_ref, wconv_ref, L))
    q, k, v = xs[:, 0:W], xs[:, W:2 * W], xs[:, 2 * W:3 * W]
    small = small_ref[...]
    lane = lax.broadcasted_iota(jnp.int32, (L, LANE), 1)
    beta_s = jnp.where(lane < SMALL_DECAY, jax.nn.sigmoid(small), 0.0)
    g_s = jnp.where((lane >= SMALL_DECAY) & (lane < SMALL_DT),
                    -jnp.exp(alog_ref[...]) * _softplus(small + dtb_ref[...]), 0.0)
    if tv_last < L:
        last_rows = jnp.where(c == n_chunks - 1, tv_last, L)
        keep_w = lax.broadcasted_iota(jnp.int32, (L, W), 0) < last_rows
        keep_s = lax.broadcasted_iota(jnp.int32, (L, LANE), 0) < last_rows
        q, k, v = jnp.where(keep_w, q, 0.0), jnp.where(keep_w, k, 0.0), jnp.where(keep_w, v, 0.0)
        beta_s, g_s = jnp.where(keep_s, beta_s, 0.0), jnp.where(keep_s, g_s, 0.0)
    bdm = bd_ref[...]
    qn = q * lax.rsqrt(_dot3_left(q * q, bdm) + RMS_EPS) * (GDN_DK ** -0.5)
    kn = k * lax.rsqrt(_dot3_left(k * k, bdm) + RMS_EPS)
    gcs_s = _dot3_right(tril_ref[...], g_s)
    beta_e = _dot3_left(beta_s, expb_ref[...])
    gcs_e = _dot3_left(gcs_s, expg_ref[...])
    eg = jnp.exp(gcs_e)
    kb = kn * beta_e
    tile4 = lambda a: jnp.concatenate([a] * GDN_HEADS, axis=0)
    stack = lambda a: jnp.where(bd, tile4(a), 0.0)
    rl = lax.broadcasted_iota(jnp.int32, (L, W), 0)
    cl = lax.broadcasted_iota(jnp.int32, (L, W), 1)
    grow = jnp.sum(jnp.where((cl % L) == rl, gcs_e, 0.0), axis=0, keepdims=True)
    incl = bd & ((r2 % L) >= (c2 % L))
    decay = jnp.exp(jnp.where(incl, tile4(gcs_e) - grow, NEG_BIG))
    kns = stack(kn).astype(BF16)
    a_mat = jnp.where((r2 % L) != (c2 % L), _dot_nt(stack(kb).astype(BF16), kns) * decay, 0.0)
    qk = _dot_nt(stack(qn).astype(BF16), kns) * decay
    p = jnp.where(r2 == c2, 1.0, 0.0) - a_mat
    x = _mm3(a_mat, a_mat)
    p = p + _mm3(p, x)
    for _ in range(4):
        x = _mm3(x, x)
        p = p + _mm3(p, x)
    collapse = lambda m: (m[0:L] + m[L:2 * L]) + (m[2 * L:3 * L] + m[3 * L:4 * L])
    u_c = collapse(_mm3(p, stack(v * beta_e)))
    w_c = collapse(_mm3(p, stack(kb * eg)))
    s = s_ref[...]
    sb = s.astype(BF16)
    v_new = u_c - _dot(w_c.astype(BF16), sb)
    o = _dot((qn * eg).astype(BF16), sb) + collapse(_dot(qk.astype(BF16), stack(v_new).astype(BF16)))
    glast = gcs_e[L - 1:L, :]
    k_dec = kn * jnp.exp(glast - gcs_e)
    s_new = s * jnp.exp(glast) + jnp.where(bd, _dot_tn(k_dec.astype(BF16), v_new.astype(BF16)), 0.0)
    s_ref[...] = s_new
    on = o * lax.rsqrt(_dot3_left(o * o, bdm) * (1.0 / HEAD_DIM) + RMS_EPS) * gnorm_ref[...]
    o_ref[...] = on * _silu(z_ref[...])

    @pl.when(c == n_chunks - 1)
    def _():
        convn_ref[...] = xfull_ref[CONV_PAD + tv_last - (CONV_WIDTH - 1):CONV_PAD + tv_last, :]
        sn_ref[...] = _dot3_left(s_new, fold_ref[...])

    xfull_ref[0:CONV_PAD, :] = xfull_ref[L:L + CONV_PAD, :]


def _gdn_call(u, n_seq, rows_per_seq, t_valid, conv0, s0, w_conv, a_log, dt_bias, g_norm):
    L = GDN_CHUNK
    n_chunks = -(-t_valid // L)
    tv_last = t_valid - (n_chunks - 1) * L
    rb = rows_per_seq // L
    lane_row = lambda vals, off: jnp.zeros((1, LANE), F32).at[0, off:off + vals.shape[0]].set(vals)
    conv0_p = jnp.pad(conv0, ((0, 0), (CONV_PAD - (CONV_WIDTH - 1), 0), (0, 0)))
    row = lambda w, col: pl.BlockSpec((L, w), lambda b, c: (b * rb + c, col))
    full = lambda r, cc: pl.BlockSpec((r, cc), lambda b, c: (0, 0))
    per_seq = lambda r, cc: pl.BlockSpec((None, r, cc), lambda b, c: (b, 0, 0))
    out = pl.pallas_call(
        functools.partial(_gdn_body, n_chunks=n_chunks, tv_last=tv_last),
        grid=(n_seq, n_chunks),
        in_specs=[row(GDN_CONV_DIM, OFF_QKVB // GDN_CONV_DIM), row(GDN_WIDTH, OFF_ZB // GDN_WIDTH),
                  row(LANE, OFF_SMALL // LANE),
                  full(CONV_WIDTH, GDN_CONV_DIM), full(1, LANE), full(1, LANE), full(1, GDN_WIDTH),
                  per_seq(CONV_PAD, GDN_CONV_DIM), per_seq(GDN_WIDTH, HEAD_DIM),
                  full(L, L), full(LANE, GDN_WIDTH), full(LANE, GDN_WIDTH), full(GDN_WIDTH, GDN_WIDTH),
                  full(GDN_WIDTH, HEAD_DIM), full(HEAD_DIM, GDN_WIDTH)],
        out_specs=[pl.BlockSpec((L, GDN_WIDTH), lambda b, c: (b * n_chunks + c, 0)),
                   per_seq(CONV_WIDTH - 1, GDN_CONV_DIM), per_seq(GDN_WIDTH, HEAD_DIM)],
        out_shape=[jax.ShapeDtypeStruct((n_seq * n_chunks * L, GDN_WIDTH), F32),
                   jax.ShapeDtypeStruct((n_seq, CONV_WIDTH - 1, GDN_CONV_DIM), F32),
                   jax.ShapeDtypeStruct((n_seq, GDN_WIDTH, HEAD_DIM), F32)],
        scratch_shapes=[pltpu.VMEM((L + CONV_PAD, GDN_CONV_DIM), F32), pltpu.VMEM((GDN_WIDTH, GDN_WIDTH), F32)],
        compiler_params=_params(("parallel", "arbitrary")), name="gdn",
    )(u, u, u, w_conv, lane_row(a_log, SMALL_DECAY), lane_row(dt_bias, SMALL_DECAY),
      jnp.tile(g_norm, GDN_HEADS).reshape(1, GDN_WIDTH), conv0_p, s0.reshape(n_seq, GDN_WIDTH, HEAD_DIM),
      _tril_incl(L), _expand_matrix(SMALL_BETA, GDN_HEADS, HEAD_DIM), _expand_matrix(SMALL_DECAY, GDN_HEADS, HEAD_DIM),
      _block_diag_ones(GDN_WIDTH, HEAD_DIM), _fold_matrix(GDN_HEADS, HEAD_DIM), _fold_matrix(GDN_HEADS, HEAD_DIM).T)
    o, conv_new, s_new = out
    return o, conv_new, s_new.reshape(n_seq, GDN_HEADS, GDN_DK, HEAD_DIM)


def _ssd_body(xbc_ref, z_ref, small_ref, wconv_ref, bconv_ref, alog_ref, dtb_ref, dskip_ref, gnorm_ref,
              conv0_ref, h0_ref, tril_ref, expc_ref,
              o_ref, convn_ref, hn_ref, xfull_ref, ht_ref, *, n_chunks, tv_last):
    L = SSD_CHUNK
    W = SSD_WIDTH
    GW = W // SSD_GROUPS
    c = pl.program_id(1)

    @pl.when(c == 0)
    def _():
        xfull_ref[0:CONV_PAD, :] = conv0_ref[...]
        for j in range(W // LANE):
            ht_ref[:, j * LANE:(j + 1) * LANE] = h0_ref[j * LANE:(j + 1) * LANE, :].T

    xs = _silu(_chunk_conv(xbc_ref, xfull_ref, wconv_ref, L) + bconv_ref[...])
    x_c, bm, cm = xs[:, 0:W], xs[:, W:W + GW], xs[:, W + GW:W + 2 * GW]
    small = small_ref[...]
    lane = lax.broadcasted_iota(jnp.int32, (L, LANE), 1)
    dt_s = jnp.where((lane >= SMALL_DT) & (lane < SMALL_DT + SSD_HEADS), _softplus(small + dtb_ref[...]), 0.0)
    if tv_last < L:
        last_rows = jnp.where(c == n_chunks - 1, tv_last, L)
        keep_s = lax.broadcasted_iota(jnp.int32, (L, LANE), 0) < last_rows
        keep_g = lax.broadcasted_iota(jnp.int32, (L, GW), 0) < last_rows
        dt_s = jnp.where(keep_s, dt_s, 0.0)
        bm, cm = jnp.where(keep_g, bm, 0.0), jnp.where(keep_g, cm, 0.0)
    ac_s = dt_s * (-jnp.exp(alog_ref[...]))
    acs_s = _dot3_right(tril_ref[...], ac_s)
    acs_e = _dot3_left(acs_s, expc_ref[...])
    dt_e = _dot3_left(dt_s, expc_ref[...])
    xdt = x_c * dt_e
    e = jnp.exp(acs_e)
    last = acs_e[L - 1:L, :]
    xw = xdt * jnp.exp(last - acs_e)
    dchunk = jnp.exp(last)
    acs_t = acs_s.T
    incl = lax.broadcasted_iota(jnp.int32, (L, L), 0) >= lax.broadcasted_iota(jnp.int32, (L, L), 1)
    first_half = lax.broadcasted_iota(jnp.int32, (L, LANE), 1) < HEAD_DIM
    ht = ht_ref[...]
    ys = []
    for g in range(SSD_GROUPS):
        gl = slice(g * GW, (g + 1) * GW)
        bg = bm[:, g * SSD_STATE:(g + 1) * SSD_STATE].astype(BF16)
        cg = cm[:, g * SSD_STATE:(g + 1) * SSD_STATE].astype(BF16)
        scores = _dot_nt(cg, bg)
        y_off = _dot(cg, ht[:, gl].astype(BF16)) * e[:, gl]
        ht_ref[:, gl] = ht[:, gl] * dchunk[:, gl] + _dot_tn(bg, xw[:, gl].astype(BF16))
        pairs = []
        for pp in range(GW // LANE):
            xp = xdt[:, g * GW + pp * LANE:g * GW + (pp + 1) * LANE]
            yp = jnp.zeros((L, LANE), F32)
            for hh in range(2):
                h = g * (SSD_HEADS // SSD_GROUPS) + pp * 2 + hh
                col = acs_s[:, SMALL_DT + h:SMALL_DT + h + 1]
                rowv = acs_t[SMALL_DT + h:SMALL_DT + h + 1, :]
                decay = jnp.exp(jnp.where(incl, col - rowv, NEG_BIG))
                xm = jnp.where(first_half if hh == 0 else jnp.logical_not(first_half), xp, 0.0)
                yp = yp + _dot((scores * decay).astype(BF16), xm.astype(BF16))
            pairs.append(yp)
        ys.append(jnp.concatenate(pairs, axis=1) + y_off)
    y = (jnp.concatenate(ys, axis=1) + dskip_ref[...] * x_c) * _silu(z_ref[...])
    outs = []
    for g in range(SSD_GROUPS):
        yg = y[:, g * GW:(g + 1) * GW]
        outs.append(yg * lax.rsqrt(jnp.mean(yg * yg, axis=-1, keepdims=True) + RMS_EPS))
    o_ref[...] = jnp.concatenate(outs, axis=1) * gnorm_ref[...]

    @pl.when(c == n_chunks - 1)
    def _():
        convn_ref[...] = xfull_ref[CONV_PAD + tv_last - (CONV_WIDTH - 1):CONV_PAD + tv_last, :]
        for j in range(W // LANE):
            hn_ref[j * LANE:(j + 1) * LANE, :] = ht_ref[:, j * LANE:(j + 1) * LANE].T

    xfull_ref[0:CONV_PAD, :] = xfull_ref[L:L + CONV_PAD, :]


def _ssd_call(u, n_seq, rows_per_seq, t_valid, conv0, h0, w_conv, b_conv, a_log, dt_bias, d_skip, g_norm):
    L = SSD_CHUNK
    n_chunks = -(-t_valid // L)
    tv_last = t_valid - (n_chunks - 1) * L
    rb = rows_per_seq // L
    lane_row = lambda vals, off: jnp.zeros((1, LANE), F32).at[0, off:off + vals.shape[0]].set(vals)
    conv0_p = jnp.pad(conv0, ((0, 0), (CONV_PAD - (CONV_WIDTH - 1), 0), (0, 0)))
    row = lambda w, col: pl.BlockSpec((L, w), lambda b, c: (b * rb + c, col))
    full = lambda r, cc: pl.BlockSpec((r, cc), lambda b, c: (0, 0))
    per_seq = lambda r, cc: pl.BlockSpec((None, r, cc), lambda b, c: (b, 0, 0))
    out = pl.pallas_call(
        functools.partial(_ssd_body, n_chunks=n_chunks, tv_last=tv_last),
        grid=(n_seq, n_chunks),
        in_specs=[row(SSD_XBC, OFF_XBC // SSD_XBC), row(SSD_WIDTH, OFF_ZC // SSD_WIDTH), row(LANE, OFF_SMALL // LANE),
                  full(CONV_WIDTH, SSD_XBC), full(1, SSD_XBC), full(1, LANE), full(1, LANE),
                  full(1, SSD_WIDTH), full(1, SSD_WIDTH),
                  per_seq(CONV_PAD, SSD_XBC), per_seq(SSD_WIDTH, SSD_STATE),
                  full(L, L), full(LANE, SSD_WIDTH)],
        out_specs=[pl.BlockSpec((L, SSD_WIDTH), lambda b, c: (b * n_chunks + c, 0)),
                   per_seq(CONV_WIDTH - 1, SSD_XBC), per_seq(SSD_WIDTH, SSD_STATE)],
        out_shape=[jax.ShapeDtypeStruct((n_seq * n_chunks * L, SSD_WIDTH), F32),
                   jax.ShapeDtypeStruct((n_seq, CONV_WIDTH - 1, SSD_XBC), F32),
                   jax.ShapeDtypeStruct((n_seq, SSD_WIDTH, SSD_STATE), F32)],
        scratch_shapes=[pltpu.VMEM((L + CONV_PAD, SSD_XBC), F32), pltpu.VMEM((SSD_STATE, SSD_WIDTH), F32)],
        compiler_params=_params(("parallel", "arbitrary")), name="ssd",
    )(u, u, u, w_conv, b_conv.reshape(1, SSD_XBC), lane_row(a_log, SMALL_DT), lane_row(dt_bias, SMALL_DT),
      jnp.repeat(d_skip, HEAD_DIM).reshape(1, SSD_WIDTH), g_norm.reshape(1, SSD_WIDTH),
      conv0_p, h0.reshape(n_seq, SSD_WIDTH, SSD_STATE), _tril_incl(L), _expand_matrix(SMALL_DT, SSD_HEADS, HEAD_DIM))
    o, conv_new, h_new = out
    return o, conv_new, h_new.reshape(n_seq, SSD_HEADS, HEAD_DIM, SSD_STATE)


def _proj_out_body(x_ref, oa_ref, ob_ref, oc_ref, w_ref, g_ref, b_ref, rw_ref, rb_ref, x1_ref, lg_ref):
    mix = _dot(oa_ref[...].astype(BF16), w_ref[0:256, :])
    mix += _dot(ob_ref[...].astype(BF16), w_ref[256:512, :])
    mix += _dot(oc_ref[...].astype(BF16), w_ref[512:1024, :])
    x1 = _layer_norm_f32(DEEPNORM_ALPHA * x_ref[...] + mix, g_ref[...], b_ref[...])
    x1_ref[...] = x1
    lg_ref[...] = jnp.dot(x1, rw_ref[...], preferred_element_type=F32, precision=lax.Precision.HIGHEST) + rb_ref[...]


def _proj_out_call(x, o_a, o_b, o_c, w_bf16, g, b, router_w, router_b):
    n, d = x.shape
    bm = _row_block(n)
    rw = jnp.pad(router_w, ((0, 0), (0, LANE - N_EXPERTS)))
    rb = jnp.pad(router_b, (0, LANE - N_EXPERTS)).reshape(1, LANE)
    row = lambda w: pl.BlockSpec((bm, w), lambda i: (i, 0))
    full = lambda r, c: pl.BlockSpec((r, c), lambda i: (0, 0))
    return pl.pallas_call(
        _proj_out_body, grid=(n // bm,),
        in_specs=[row(d), row(SB_WIDTH), row(GDN_WIDTH), row(SSD_WIDTH), full(d, d), full(1, d), full(1, d),
                  full(d, LANE), full(1, LANE)],
        out_specs=[row(d), row(LANE)],
        out_shape=[jax.ShapeDtypeStruct((n, d), F32), jax.ShapeDtypeStruct((n, LANE), F32)],
        compiler_params=_params(("parallel",)), name="proj_out",
    )(x, o_a, o_b, o_c, w_bf16, g.reshape(1, d), b.reshape(1, d), rw, rb)


def _moe_body(be_ref, nb_ref, rows_ref, wg_ref, bg_ref, wu_ref, bu_ref, wd_ref, bd_ref, o_ref, wg_s, wu_s, wd_s):
    i = pl.program_id(0)
    prev = be_ref[jnp.maximum(i - 1, 0)]

    @pl.when(jnp.logical_or(i == 0, be_ref[i] != prev))
    def _():
        wg_s[...] = wg_ref[...].astype(BF16)
        wu_s[...] = wu_ref[...].astype(BF16)
        wd_s[...] = wd_ref[...].astype(BF16)

    @pl.when(i < nb_ref[0])
    def _():
        xb = rows_ref[...].astype(BF16)
        glu = jnp.minimum(_dot(xb, wg_s[...]) + bg_ref[...], SWIGLU_LIMIT)
        lin = jnp.clip(_dot(xb, wu_s[...]) + bu_ref[...], -SWIGLU_LIMIT, SWIGLU_LIMIT)
        act = glu * jax.nn.sigmoid(SWIGLU_ALPHA * glu) * (lin + 1.0)
        o_ref[...] = _dot(act.astype(BF16), wd_s[...]) + bd_ref[...]

    @pl.when(i >= nb_ref[0])
    def _():
        o_ref[...] = jnp.zeros_like(o_ref)


def _moe_call(rows_in, block_expert, n_used, w_gate, b_gate, w_up, b_up, w_down, b_down, layer, blk):
    n_rows, d = rows_in.shape
    n_blocks = n_rows // blk
    wspec = lambda r, c: pl.BlockSpec((None, None, r, c), lambda i, be, nb: (layer, be[i], 0, 0))
    grid_spec = pltpu.PrefetchScalarGridSpec(
        num_scalar_prefetch=2, grid=(n_blocks,),
        in_specs=[pl.BlockSpec((blk, d), lambda i, be, nb: (i, 0)),
                  wspec(d, D_EXPERT), wspec(1, D_EXPERT), wspec(d, D_EXPERT), wspec(1, D_EXPERT),
                  wspec(D_EXPERT, d), wspec(1, d)],
        out_specs=pl.BlockSpec((blk, d), lambda i, be, nb: (i, 0)),
        scratch_shapes=[pltpu.VMEM((d, D_EXPERT), BF16), pltpu.VMEM((d, D_EXPERT), BF16),
                        pltpu.VMEM((D_EXPERT, d), BF16)])
    depth = w_gate.shape[0]
    return pl.pallas_call(
        _moe_body, grid_spec=grid_spec,
        out_shape=jax.ShapeDtypeStruct((n_rows, d), F32),
        compiler_params=_params(("arbitrary",)), name="moe_experts",
    )(block_expert, n_used, rows_in, w_gate, b_gate.reshape(depth, N_EXPERTS, 1, D_EXPERT),
      w_up, b_up.reshape(depth, N_EXPERTS, 1, D_EXPERT), w_down, b_down.reshape(depth, N_EXPERTS, 1, d))


def _route(logits, blk):
    n_tok = logits.shape[0]
    top_logit, top_idx = lax.top_k(logits, TOP_K)
    gate = jax.nn.softmax(top_logit, axis=-1)
    n_assign = n_tok * TOP_K
    n_blocks = -(-(n_assign + N_EXPERTS * (blk - 1)) // blk)
    expert = top_idx.reshape(-1).astype(jnp.int32)
    assign = jnp.arange(n_assign, dtype=jnp.int32)
    sorted_expert, order = lax.sort((expert, assign), num_keys=1)
    counts = jnp.sum(expert[:, None] == jnp.arange(N_EXPERTS, dtype=jnp.int32)[None, :], axis=0, dtype=jnp.int32)
    padded = (counts + blk - 1) // blk * blk
    padded_end = jnp.cumsum(padded)
    start = jnp.cumsum(counts) - counts
    row_sorted = (padded_end - padded)[sorted_expert] + assign - start[sorted_expert]
    _, dest = lax.sort((order, row_sorted), num_keys=1)
    block_expert = jnp.minimum(jnp.searchsorted(padded_end, jnp.arange(n_blocks) * blk, side='right'),
                               N_EXPERTS - 1).astype(jnp.int32)
    r = jnp.arange(n_blocks * blk, dtype=jnp.int32)
    e_r = block_expert[r // blk]
    pos = jnp.clip(start[e_r] + r - (padded_end - padded)[e_r], 0, n_assign - 1)
    src = order[pos] // TOP_K
    dest_t = dest.reshape(n_tok, TOP_K).T.reshape(-1)
    n_used = (padded_end[-1] // blk).astype(jnp.int32).reshape(1)
    return gate, src, dest_t, block_expert, n_used


def _post_body(x1_ref, r0_ref, r1_ref, r2_ref, r3_ref, gate_ref, p_ref, g_ref, b_ref, wg_ref, bg_ref, wp_ref, o_ref):
    gate = gate_ref[...]
    ffn = gate[:, 0:1] * r0_ref[...]
    for k, r_ref in ((1, r1_ref), (2, r2_ref), (3, r3_ref)):
        ffn += gate[:, k:k + 1] * r_ref[...]
    x2 = _layer_norm_f32(DEEPNORM_ALPHA * x1_ref[...] + ffn, g_ref[...], b_ref[...])
    gate_logit = _dot(x2.astype(BF16), wg_ref[...]) + bg_ref[...]
    emb = _dot(p_ref[...].astype(BF16), wp_ref[...])
    o_ref[...] = x2 + jax.nn.sigmoid(gate_logit) * emb


def _post_call(x1, rows_k, gate, p, g, b, wg_bf16, bg, wp_bf16):
    n, d = x1.shape
    bm = _row_block(n)
    nb = n // bm
    row = lambda w: pl.BlockSpec((bm, w), lambda i: (i, 0))
    plane = lambda k: pl.BlockSpec((bm, d), lambda i: (k * nb + i, 0))
    full = lambda r, c: pl.BlockSpec((r, c), lambda i: (0, 0))
    return pl.pallas_call(
        _post_body, grid=(nb,),
        in_specs=[row(d)] + [plane(k) for k in range(TOP_K)] + [row(TOP_K), row(PLE_DIM), full(1, d), full(1, d),
                                                                 full(d, d), full(1, d), full(PLE_DIM, d)],
        out_specs=row(d),
        out_shape=jax.ShapeDtypeStruct((n, d), F32),
        compiler_params=_params(("parallel",)), name="post_ffn",
    )(x1, rows_k, rows_k, rows_k, rows_k, gate, p, g.reshape(1, d), b.reshape(1, d), wg_bf16, bg.reshape(1, d),
      wp_bf16)


def kernel(x_prompt, x_sample, cache_k_a, cache_v_a, state_conv_b, state_delta_b, state_conv_c, state_ssm_c, page_table, p_prompt, p_sample, ln_in_g, ln_in_b, w_in, sb_bias, w_conv_b, a_log_b, dt_bias_b, g_norm_b, w_conv_c, b_conv_c, a_log_c, dt_bias_c, d_skip_c, g_norm_c, w_out, ln1_g, ln1_b, router_w, router_b, w_gate, b_gate, w_up, b_up, w_down, b_down, ln2_g, ln2_b, ple_gate_w, ple_gate_b, ple_proj_w):
    depth = w_in.shape[0]
    w_in_r = jnp.pad(w_in[:, :, _IN_PERM], ((0, 0), (0, 0), (0, U_WIDTH - _IN_PERM.size))).astype(BF16)
    w_out_b = w_out.astype(BF16)
    ple_gate_w_b = ple_gate_w.astype(BF16)
    ple_proj_w_b = ple_proj_w.astype(BF16)
    n_pool = cache_k_a.shape[1]
    cache_k = cache_k_a.reshape(depth, n_pool, PAGE_SIZE, SB_WIDTH)
    cache_v = cache_v_a.reshape(depth, n_pool, PAGE_SIZE, SB_WIDTH)

    def run_group(x, p, sample):
        bsz, t, d = x.shape
        n = bsz * t
        x = _ln_call(x.reshape(n, d), ln_in_g, ln_in_b)
        new = []
        for i in range(depth):
            u = _proj_in_call(x, w_in_r[i])
            k_a = u[:, OFF_K:OFF_K + SB_WIDTH].reshape(bsz, t, SB_HEADS, HEAD_DIM)
            v_a = u[:, OFF_V:OFF_V + SB_WIDTH].reshape(bsz, t, SB_HEADS, HEAD_DIM)
            if sample:
                o_a = _sb_decode_call(u[:, OFF_Q:OFF_Q + SB_WIDTH], sb_bias[i], cache_k, cache_v, page_table, i)
                conv_b, delta_b, conv_c, ssm_c = state_conv_b[i], state_delta_b[i], state_conv_c[i], state_ssm_c[i]
                rows_per_seq = SSD_CHUNK
                u_mix = jnp.pad(u.reshape(bsz, t, U_WIDTH), ((0, 0), (0, rows_per_seq - t), (0, 0)))
                u_mix = u_mix.reshape(bsz * rows_per_seq, U_WIDTH)
            else:
                o_a = _sb_prefill_call(u, sb_bias[i], bsz, t)
                conv_b = jnp.zeros((bsz, CONV_WIDTH - 1, GDN_CONV_DIM), F32)
                delta_b = jnp.zeros((bsz, GDN_HEADS, GDN_DK, HEAD_DIM), F32)
                conv_c = jnp.zeros((bsz, CONV_WIDTH - 1, SSD_XBC), F32)
                ssm_c = jnp.zeros((bsz, SSD_HEADS, HEAD_DIM, SSD_STATE), F32)
                rows_per_seq = t
                u_mix = u
            o_b, conv_b_new, delta_new = _gdn_call(u_mix, bsz, rows_per_seq, t, conv_b, delta_b, w_conv_b[i],
                                                   a_log_b[i], dt_bias_b[i], g_norm_b[i])
            o_c, conv_c_new, ssm_new = _ssd_call(u_mix, bsz, rows_per_seq, t, conv_c, ssm_c, w_conv_c[i], b_conv_c[i],
                                                 a_log_c[i], dt_bias_c[i], d_skip_c[i], g_norm_c[i])
            if sample:
                o_b = o_b.reshape(bsz, -1, GDN_WIDTH)[:, :t].reshape(n, GDN_WIDTH)
                o_c = o_c.reshape(bsz, -1, SSD_WIDTH)[:, :t].reshape(n, SSD_WIDTH)
            x1, logits = _proj_out_call(x, o_a, o_b, o_c, w_out_b[i], ln1_g[i], ln1_b[i], router_w[i], router_b[i])
            blk = MOE_BLOCK_PROMPT if n * TOP_K >= MOE_BLOCK_PROMPT * N_EXPERTS else 8
            gate, src, dest_t, block_expert, n_used = _route(logits[:, :N_EXPERTS], blk)
            rows_out = _moe_call(x1[src], block_expert, n_used, w_gate, b_gate, w_up, b_up, w_down, b_down, i, blk)
            x = _post_call(x1, rows_out[dest_t], gate, p[i].reshape(n, PLE_DIM), ln2_g[i], ln2_b[i], ple_gate_w_b[i],
                           ple_gate_b[i], ple_proj_w_b[i])
            new.append((k_a, v_a, conv_b_new, delta_new, conv_c_new, ssm_new))
        return x.reshape(bsz, t, d), [jnp.stack([s[j] for s in new]) for j in range(6)]

    y_prompt, new_prompt = run_group(x_prompt, p_prompt, False)
    y_sample, new_sample = run_group(x_sample, p_sample, True)
    return (y_prompt, y_sample, *new_prompt, *new_sample)
```

```python
import functools

import jax
import jax.numpy as jnp
import numpy as np
from jax import lax
from jax.experimental import pallas as pl
from jax.experimental.pallas import tpu as pltpu

F32 = jnp.float32
BF16 = jnp.bfloat16

D_MODEL = 1024
DEPTH = 4
PAGE_SIZE = 128
HEAD_DIM = 64
CONV_WIDTH = 4
SB_WIDTH = 256
SB_HEADS = 4
GDN_WIDTH = 256
GDN_HEADS = 4
GDN_DK = 64
GDN_CONV_DIM = 768
GDN_CHUNK = 64
SSD_WIDTH = 512
SSD_HEADS = 8
SSD_GROUPS = 2
SSD_STATE = 128
SSD_XBC = 1024
SSD_CHUNK = 128
N_EXPERTS = 32
TOP_K = 4
D_EXPERT = 512
SWIGLU_ALPHA = 1.702
SWIGLU_LIMIT = 7.0
PLE_DIM = 256
DEEPNORM_ALPHA = (2 * DEPTH) ** 0.25
LN_EPS = 1e-5
RMS_EPS = 1e-6

_REF_WIDTHS = (256, 256, 256, 768, 256, 4, 4, 512, 1024, 8)
_REF_OFF = tuple(int(v) for v in np.cumsum((0,) + _REF_WIDTHS))
OFF_XBC, OFF_ZC, OFF_QKVB, OFF_Q, OFF_K, OFF_V, OFF_ZB, OFF_SMALL = 0, 1024, 1536, 2304, 2560, 2816, 3072, 3328
U_WIDTH = 3456
LANE = 128
SMALL_BETA, SMALL_DECAY, SMALL_DT = 0, GDN_HEADS, 2 * GDN_HEADS
_IN_PERM = np.concatenate([
    np.arange(_REF_OFF[8], _REF_OFF[9]),
    np.arange(_REF_OFF[7], _REF_OFF[8]),
    np.arange(_REF_OFF[3], _REF_OFF[4]),
    np.arange(_REF_OFF[0], _REF_OFF[3]),
    np.arange(_REF_OFF[4], _REF_OFF[5]),
    np.arange(_REF_OFF[5], _REF_OFF[7]),
    np.arange(_REF_OFF[9], _REF_OFF[10]),
])

VMEM_LIMIT = 48 * 1024 * 1024
MOE_BLOCK_PROMPT = 512
SB_BLOCK = 256
DEC_PAGES_PER_STEP = 16
CONV_PAD = 8
NEG_BIG = -1e30
LOG2E = 1.4426950408889634


def _params(sem):
    return pltpu.CompilerParams(dimension_semantics=sem, vmem_limit_bytes=VMEM_LIMIT)


def _row_block(n):
    return n if n < 512 else 512


def _layer_norm_f32(x, g, b):
    mu = jnp.mean(x, axis=-1, keepdims=True)
    xc = x - mu
    var = jnp.mean(xc * xc, axis=-1, keepdims=True)
    return xc * lax.rsqrt(var + LN_EPS) * g + b


def _softplus(z):
    return jnp.maximum(z, 0.0) + jnp.log1p(jnp.exp(-jnp.abs(z)))


def _softplus2(z2):
    return jnp.maximum(z2, 0.0) + jnp.log2(1.0 + jnp.exp2(-jnp.abs(z2)))


def _silu(x):
    return x * jax.nn.sigmoid(x)


def _dot(a, b):
    return jnp.dot(a, b, preferred_element_type=F32)


def _dot_nt(a, b):
    return lax.dot_general(a, b, (((1,), (1,)), ((), ())), preferred_element_type=F32)


def _dot_tn(a, b):
    return lax.dot_general(a, b, (((0,), (0,)), ((), ())), preferred_element_type=F32)


def _split2(x):
    hi = x.astype(BF16)
    return hi, (x - hi.astype(F32)).astype(BF16)


def _split3(x):
    hi = x.astype(BF16)
    r = x - hi.astype(F32)
    mid = r.astype(BF16)
    return hi, mid, (r - mid.astype(F32)).astype(BF16)


def _split_dot(x, w_bf16):
    hi, lo = _split2(x)
    return _dot(hi, w_bf16) + _dot(lo, w_bf16)


def _dot3_left(x, w_bf16):
    hi, mid, lo = _split3(x)
    return _dot(hi, w_bf16) + (_dot(mid, w_bf16) + _dot(lo, w_bf16))


def _dot3_right(w_bf16, x):
    hi, mid, lo = _split3(x)
    return _dot(w_bf16, hi) + (_dot(w_bf16, mid) + _dot(w_bf16, lo))


def _mm3(a, b):
    ah, al = _split2(a)
    bh, bl = _split2(b)
    return _dot(ah, bh) + (_dot(ah, bl) + _dot(al, bh))


def _cumsum_matrix(n):
    j = np.arange(n)[:, None]
    s = np.arange(n)[None, :]
    return jnp.asarray(np.concatenate([(j > s), np.ones((n, LANE), bool)], axis=1), BF16)


def _tril_incl(n):
    return jnp.asarray(np.tril(np.ones((n, n), bool)), BF16)


def _expand_matrix(first_lane, heads, width):
    m = np.zeros((LANE, heads * width), bool)
    for h in range(heads):
        m[first_lane + h, h * width:(h + 1) * width] = True
    return jnp.asarray(m, BF16)


def _block_diag_ones(n, blk):
    i = np.arange(n)
    return jnp.asarray((i[:, None] // blk) == (i[None, :] // blk), BF16)


def _fold_matrix(heads, width):
    i = np.arange(heads * width)
    return jnp.asarray((i[:, None] % width) == np.arange(width)[None, :], BF16)


def _ln_body(x_ref, g_ref, b_ref, o_ref):
    o_ref[...] = _layer_norm_f32(x_ref[...], g_ref[...], b_ref[...])


def _ln_call(x, g, b):
    n, d = x.shape
    bm = _row_block(n)
    return pl.pallas_call(
        _ln_body, grid=(n // bm,),
        in_specs=[pl.BlockSpec((bm, d), lambda i: (i, 0)),
                  pl.BlockSpec((1, d), lambda i: (0, 0)), pl.BlockSpec((1, d), lambda i: (0, 0))],
        out_specs=pl.BlockSpec((bm, d), lambda i: (i, 0)),
        out_shape=jax.ShapeDtypeStruct((n, d), F32),
        compiler_params=_params(("parallel",)), name="ln_in",
    )(x, g.reshape(1, d), b.reshape(1, d))


_PROJ_CHUNK = 384


def _proj_in_body(x_ref, w_ref, o_ref):
    xb = x_ref[...].astype(BF16)
    for j in range(0, U_WIDTH, _PROJ_CHUNK):
        o_ref[:, j:j + _PROJ_CHUNK] = _dot(xb, w_ref[:, j:j + _PROJ_CHUNK])


def _proj_in_call(x, w_bf16):
    n, d = x.shape
    bm = _row_block(n)
    return pl.pallas_call(
        _proj_in_body, grid=(n // bm,),
        in_specs=[pl.BlockSpec((bm, d), lambda i: (i, 0)),
                  pl.BlockSpec((d, U_WIDTH), lambda i: (0, 0))],
        out_specs=pl.BlockSpec((bm, U_WIDTH), lambda i: (i, 0)),
        out_shape=jax.ShapeDtypeStruct((n, U_WIDTH), F32),
        compiler_params=_params(("parallel",)), name="proj_in",
    )(x, w_bf16)


def _sb_prefill_body(bias_ref, q_ref, k_ref, v_ref, u_ref, o_ref, qh_ref, acc_ref, carry_ref):
    blk = q_ref.shape[0]
    i = pl.program_id(1)
    q = q_ref[...] * (HEAD_DIM ** -0.5 * LOG2E)
    lane_head = lax.broadcasted_iota(jnp.int32, (1, SB_WIDTH), 1) // HEAD_DIM
    for h in range(SB_HEADS):
        qh_ref[h] = jnp.where(lane_head == h, q, 0.0).astype(BF16)
    acc_ref[...] = jnp.zeros_like(acc_ref)
    carry_ref[...] = jnp.zeros_like(carry_ref)
    rows = lax.broadcasted_iota(jnp.int32, (blk, blk), 0)
    cols = lax.broadcasted_iota(jnp.int32, (blk, blk), 1)

    def key_block(j, diagonal):
        start = pl.multiple_of(j * blk, blk)
        kj = k_ref[pl.ds(start, blk), :].astype(BF16)
        vj = v_ref[pl.ds(start, blk), :].astype(BF16)
        for h in range(SB_HEADS):
            z = _dot_nt(qh_ref[h], kj) + bias_ref[h]
            log_keep = -_softplus2(z)
            logit = z + log_keep
            if diagonal:
                visible = cols < rows
                log_keep = jnp.where(visible, log_keep, 0.0)
            lt = _split_dot(log_keep, u_ref[...])
            c = carry_ref[h]
            later = lt[:, :blk] + jnp.concatenate([c] * (blk // LANE), axis=1)
            w = jnp.exp2(logit + later)
            if diagonal:
                w = jnp.where(visible, w, 0.0)
            acc_ref[h] += _dot(w.astype(BF16), vj)
            carry_ref[h] = c + lt[:, blk:]

    key_block(i, True)

    def earlier(t, carry):
        key_block(i - 1 - t, False)
        return carry

    lax.fori_loop(0, i, earlier, 0)
    out = jnp.zeros((blk, SB_WIDTH), F32)
    for h in range(SB_HEADS):
        out = out + jnp.where(lane_head == h, acc_ref[h], 0.0)
    o_ref[...] = out


def _sb_prefill_call(u, bias, bsz, t):
    blk = SB_BLOCK
    nq = t // blk
    grid_spec = pltpu.PrefetchScalarGridSpec(
        num_scalar_prefetch=0, grid=(bsz, nq),
        in_specs=[pl.BlockSpec(memory_space=pltpu.SMEM),
                  pl.BlockSpec((blk, SB_WIDTH), lambda b, i: (b * nq + i, OFF_Q // SB_WIDTH)),
                  pl.BlockSpec((t, SB_WIDTH), lambda b, i: (b, OFF_K // SB_WIDTH)),
                  pl.BlockSpec((t, SB_WIDTH), lambda b, i: (b, OFF_V // SB_WIDTH)),
                  pl.BlockSpec((blk, blk + LANE), lambda b, i: (0, 0))],
        out_specs=pl.BlockSpec((blk, SB_WIDTH), lambda b, i: (b * nq + i, 0)),
        scratch_shapes=[pltpu.VMEM((SB_HEADS, blk, SB_WIDTH), BF16),
                        pltpu.VMEM((SB_HEADS, blk, SB_WIDTH), F32),
                        pltpu.VMEM((SB_HEADS, blk, LANE), F32)])
    return pl.pallas_call(
        _sb_prefill_body, grid_spec=grid_spec,
        out_shape=jax.ShapeDtypeStruct((bsz * t, SB_WIDTH), F32),
        compiler_params=_params(("parallel", "arbitrary")), name="sb_prefill",
    )(bias * LOG2E, u, u, u, _cumsum_matrix(blk))


def _sb_decode_body(pt_ref, q_ref, bias_ref, u_ref, *refs):
    g = DEC_PAGES_PER_STEP
    k_refs, v_refs = refs[:g], refs[g:2 * g]
    o_ref, acc_ref, carry_ref = refs[2 * g], refs[2 * g + 1], refs[2 * g + 2]
    s = pl.program_id(1)

    @pl.when(s == 0)
    def _():
        acc_ref[...] = jnp.zeros_like(acc_ref)
        carry_ref[...] = jnp.zeros_like(carry_ref)

    row = lax.broadcasted_iota(jnp.int32, (8, SB_WIDTH), 0)
    lane_head = lax.broadcasted_iota(jnp.int32, (8, SB_WIDTH), 1) // HEAD_DIM
    own = row == lane_head
    qm = jnp.where(own, q_ref[...] * (HEAD_DIM ** -0.5 * LOG2E), 0.0).astype(BF16)
    acc = acc_ref[...]
    carry = carry_ref[...]
    zs = [_dot(qm, k_refs[c][...].astype(BF16)) + bias_ref[...] for c in range(g)]
    log_keeps = [-_softplus2(z) for z in zs]
    lt_all = _split_dot(jnp.concatenate(log_keeps, axis=0), u_ref[...])
    ws = []
    for c in range(g):
        lt = lt_all[8 * c:8 * c + 8]
        ws.append(jnp.exp2(zs[c] + log_keeps[c] + lt[:, :LANE] + carry).astype(BF16))
        carry = carry + lt[:, LANE:]
    for c in range(g):
        acc = acc + _dot_nt(ws[c], v_refs[c][...].astype(BF16))
    acc_ref[...] = acc
    carry_ref[...] = carry

    @pl.when(s == pl.num_programs(1) - 1)
    def _():
        o_ref[...] = jnp.sum(jnp.where(own, acc, 0.0), axis=0, keepdims=True)


def _sb_decode_call(q, bias, cache_k, cache_v, page_table, layer):
    bsz = q.shape[0]
    n_pages = page_table.shape[1]
    g = DEC_PAGES_PER_STEP
    n_steps = n_pages // g
    bias8 = jnp.zeros((8, LANE), F32).at[:SB_HEADS].set(jnp.broadcast_to(bias[:, None] * LOG2E, (SB_HEADS, LANE)))

    def page_spec(c):
        return pl.BlockSpec((None, None, SB_WIDTH, PAGE_SIZE),
                            lambda b, s, pt: (layer, pt[b, n_pages - 1 - (s * g + c)], 0, 0))

    grid_spec = pltpu.PrefetchScalarGridSpec(
        num_scalar_prefetch=1, grid=(bsz, n_steps),
        in_specs=[pl.BlockSpec((None, 1, SB_WIDTH), lambda b, s, pt: (b, 0, 0)),
                  pl.BlockSpec((8, LANE), lambda b, s, pt: (0, 0)),
                  pl.BlockSpec((PAGE_SIZE, 2 * LANE), lambda b, s, pt: (0, 0))]
                 + [page_spec(c) for c in range(g)] + [page_spec(c) for c in range(g)],
        out_specs=pl.BlockSpec((None, 1, SB_WIDTH), lambda b, s, pt: (b, 0, 0)),
        scratch_shapes=[pltpu.VMEM((8, SB_WIDTH), F32), pltpu.VMEM((8, LANE), F32)])
    out = pl.pallas_call(
        _sb_decode_body, grid_spec=grid_spec,
        out_shape=jax.ShapeDtypeStruct((bsz, 1, SB_WIDTH), F32),
        compiler_params=_params(("parallel", "arbitrary")), name="sb_decode",
    )(page_table, q.reshape(bsz, 1, SB_WIDTH), bias8, _cumsum_matrix(PAGE_SIZE),
      *([cache_k] * g), *([cache_v] * g))
    return out.reshape(bsz, SB_WIDTH)


def _chunk_conv(x_ref, xfull_ref, wconv_ref, rows):
    xfull_ref[CONV_PAD:CONV_PAD + rows, :] = x_ref[...]
    acc = wconv_ref[CONV_WIDTH - 1:CONV_WIDTH, :] * xfull_ref[CONV_PAD:CONV_PAD + rows, :]
    for j in range(1, CONV_WIDTH):
        acc = acc + wconv_ref[CONV_WIDTH - 1 - j:CONV_WIDTH - j, :] * xfull_ref[CONV_PAD - j:CONV_PAD - j + rows, :]
    return acc


GDN_SEQS_PER_STEP = 2


def _mm_split(a_parts, b_parts):
    (ah, al), (bh, bl) = a_parts, b_parts
    return _dot(ah, bh) + (_dot(ah, bl) + _dot(al, bh))


def _same_head_mask():
    r2 = lax.broadcasted_iota(jnp.int32, (GDN_WIDTH, GDN_WIDTH), 0)
    c2 = lax.broadcasted_iota(jnp.int32, (GDN_WIDTH, GDN_WIDTH), 1)
    return r2, c2, (r2 // GDN_CHUNK) == (c2 // GDN_CHUNK)


def _gdn_chunk(c, qkv_ref, z_ref, small_ref, wconv_ref, alog_ref, dtb_ref, gnorm_ref,
               tril_ref, expb_ref, expg_ref, bd_ref, o_ref, xfull_ref, s_ref, *, n_chunks, tv_last):
    L = GDN_CHUNK
    W = GDN_WIDTH
    r2, c2, bd = _same_head_mask()
    xs = _silu(_chunk_conv(qkv_ref, xfull_ref, wconv_ref, L))
    q, k, v = xs[:, 0:W], xs[:, W:2 * W], xs[:, 2 * W:3 * W]
    small = small_ref[...]
    lane = lax.broadcasted_iota(jnp.int32, (L, LANE), 1)
    beta_s = jnp.where(lane < SMALL_DECAY, jax.nn.sigmoid(small), 0.0)
    g_s = jnp.where((lane >= SMALL_DECAY) & (lane < SMALL_DT),
                    -jnp.exp(alog_ref[...]) * _softplus(small + dtb_ref[...]), 0.0)
    if tv_last < L:
        last_rows = jnp.where(c == n_chunks - 1, tv_last, L)
        keep_w = lax.broadcasted_iota(jnp.int32, (L, W), 0) < last_rows
        keep_s = lax.broadcasted_iota(jnp.int32, (L, LANE), 0) < last_rows
        q, k, v = jnp.where(keep_w, q, 0.0), jnp.where(keep_w, k, 0.0), jnp.where(keep_w, v, 0.0)
        beta_s, g_s = jnp.where(keep_s, beta_s, 0.0), jnp.where(keep_s, g_s, 0.0)
    bdm = bd_ref[...]
    qn = q * lax.rsqrt(_dot3_left(q * q, bdm) + RMS_EPS) * (GDN_DK ** -0.5)
    kn = k * lax.rsqrt(_dot3_left(k * k, bdm) + RMS_EPS)
    gcs_s = _dot3_right(tril_ref[...], g_s)
    beta_e = _dot3_left(beta_s, expb_ref[...])
    gcs_e = _dot3_left(gcs_s, expg_ref[...])
    eg = jnp.exp(gcs_e)
    kb = kn * beta_e
    tile4 = lambda a: jnp.concatenate([a] * GDN_HEADS, axis=0)
    stack = lambda a: jnp.where(bd, tile4(a), 0.0)
    rl = lax.broadcasted_iota(jnp.int32, (L, W), 0)
    cl = lax.broadcasted_iota(jnp.int32, (L, W), 1)
    grow = jnp.sum(jnp.where((cl % L) == rl, gcs_e, 0.0), axis=0, keepdims=True)
    incl = bd & ((r2 % L) >= (c2 % L))
    decay = jnp.exp(jnp.where(incl, tile4(gcs_e) - grow, NEG_BIG))
    kns = stack(kn).astype(BF16)
    a_mat = jnp.where((r2 % L) != (c2 % L), _dot_nt(stack(kb).astype(BF16), kns) * decay, 0.0)
    qk = _dot_nt(stack(qn).astype(BF16), kns) * decay
    p = jnp.where(r2 == c2, 1.0, 0.0) - a_mat
    a_parts = _split2(a_mat)
    x_parts = _split2(_mm_split(a_parts, a_parts))
    p = p + _mm_split(_split2(p), x_parts)
    for _ in range(4):
        x_parts = _split2(_mm_split(x_parts, x_parts))
        p = p + _mm_split(_split2(p), x_parts)
    p_parts = _split2(p)
    collapse = lambda m: (m[0:L] + m[L:2 * L]) + (m[2 * L:3 * L] + m[3 * L:4 * L])
    u_c = collapse(_mm_split(p_parts, _split2(stack(v * beta_e))))
    w_c = collapse(_mm_split(p_parts, _split2(stack(kb * eg))))
    s = s_ref[...]
    sb = s.astype(BF16)
    v_new = u_c - _dot(w_c.astype(BF16), sb)
    o = _dot((qn * eg).astype(BF16), sb) + collapse(_dot(qk.astype(BF16), stack(v_new).astype(BF16)))
    glast = gcs_e[L - 1:L, :]
    k_dec = kn * jnp.exp(glast - gcs_e)
    s_new = s * jnp.exp(glast) + jnp.where(bd, _dot_tn(k_dec.astype(BF16), v_new.astype(BF16)), 0.0)
    s_ref[...] = s_new
    on = o * lax.rsqrt(_dot3_left(o * o, bdm) * (1.0 / HEAD_DIM) + RMS_EPS) * gnorm_ref[...]
    o_ref[...] = on * _silu(z_ref[...])


def _gdn_body(qkv_ref, z_ref, small_ref, wconv_ref, alog_ref, dtb_ref, gnorm_ref, conv0_ref, s0_ref,
              tril_ref, expb_ref, expg_ref, bd_ref, fold_ref, foldt_ref,
              o_ref, convn_ref, sn_ref, xfull_ref, s_ref, *, n_chunks, tv_last):
    L = GDN_CHUNK
    c = pl.program_id(1)
    seqs = range(GDN_SEQS_PER_STEP)

    @pl.when(c == 0)
    def _():
        bd = _same_head_mask()[2]
        for j in seqs:
            xfull_ref[j, 0:CONV_PAD, :] = conv0_ref[j]
            s_ref[j] = jnp.where(bd, _dot3_left(s0_ref[j], foldt_ref[...]), 0.0)

    for j in seqs:
        _gdn_chunk(c, qkv_ref.at[j], z_ref.at[j], small_ref.at[j], wconv_ref, alog_ref, dtb_ref, gnorm_ref,
                   tril_ref, expb_ref, expg_ref, bd_ref, o_ref.at[j], xfull_ref.at[j], s_ref.at[j],
                   n_chunks=n_chunks, tv_last=tv_last)

    @pl.when(c == n_chunks - 1)
    def _():
        for j in seqs:
            convn_ref[j] = xfull_ref[j, CONV_PAD + tv_last - (CONV_WIDTH - 1):CONV_PAD + tv_last, :]
            sn_ref[j] = _dot3_left(s_ref[j], fold_ref[...])

    for j in seqs:
        xfull_ref[j, 0:CONV_PAD, :] = xfull_ref[j, L:L + CONV_PAD, :]


def _gdn_call(u, n_seq, rows_per_seq, t_valid, conv0, s0, w_conv, a_log, dt_bias, g_norm):
    L = GDN_CHUNK
    G = GDN_SEQS_PER_STEP
    n_chunks = -(-t_valid // L)
    tv_last = t_valid - (n_chunks - 1) * L
    lane_row = lambda vals, off: jnp.zeros((1, LANE), F32).at[0, off:off + vals.shape[0]].set(vals)
    conv0_p = jnp.pad(conv0, ((0, 0), (CONV_PAD - (CONV_WIDTH - 1), 0), (0, 0)))
    u3 = u.reshape(n_seq, rows_per_seq, U_WIDTH)
    row = lambda w, col: pl.BlockSpec((G, L, w), lambda b, c: (b, c, col))
    full = lambda r, cc: pl.BlockSpec((r, cc), lambda b, c: (0, 0))
    per_seq = lambda r, cc: pl.BlockSpec((G, r, cc), lambda b, c: (b, 0, 0))
    out = pl.pallas_call(
        functools.partial(_gdn_body, n_chunks=n_chunks, tv_last=tv_last),
        grid=(n_seq // G, n_chunks),
        in_specs=[row(GDN_CONV_DIM, OFF_QKVB // GDN_CONV_DIM), row(GDN_WIDTH, OFF_ZB // GDN_WIDTH),
                  row(LANE, OFF_SMALL // LANE),
                  full(CONV_WIDTH, GDN_CONV_DIM), full(1, LANE), full(1, LANE), full(1, GDN_WIDTH),
                  per_seq(CONV_PAD, GDN_CONV_DIM), per_seq(GDN_WIDTH, HEAD_DIM),
                  full(L, L), full(LANE, GDN_WIDTH), full(LANE, GDN_WIDTH), full(GDN_WIDTH, GDN_WIDTH),
                  full(GDN_WIDTH, HEAD_DIM), full(HEAD_DIM, GDN_WIDTH)],
        out_specs=[pl.BlockSpec((G, L, GDN_WIDTH), lambda b, c: (b, c, 0)),
                   per_seq(CONV_WIDTH - 1, GDN_CONV_DIM), per_seq(GDN_WIDTH, HEAD_DIM)],
        out_shape=[jax.ShapeDtypeStruct((n_seq, n_chunks * L, GDN_WIDTH), F32),
                   jax.ShapeDtypeStruct((n_seq, CONV_WIDTH - 1, GDN_CONV_DIM), F32),
                   jax.ShapeDtypeStruct((n_seq, GDN_WIDTH, HEAD_DIM), F32)],
        scratch_shapes=[pltpu.VMEM((G, L + CONV_PAD, GDN_CONV_DIM), F32),
                        pltpu.VMEM((G, GDN_WIDTH, GDN_WIDTH), F32)],
        compiler_params=_params(("parallel", "arbitrary")), name="gdn",
    )(u3, u3, u3, w_conv, lane_row(a_log, SMALL_DECAY), lane_row(dt_bias, SMALL_DECAY),
      jnp.tile(g_norm, GDN_HEADS).reshape(1, GDN_WIDTH), conv0_p, s0.reshape(n_seq, GDN_WIDTH, HEAD_DIM),
      _tril_incl(L), _expand_matrix(SMALL_BETA, GDN_HEADS, HEAD_DIM), _expand_matrix(SMALL_DECAY, GDN_HEADS, HEAD_DIM),
      _block_diag_ones(GDN_WIDTH, HEAD_DIM), _fold_matrix(GDN_HEADS, HEAD_DIM), _fold_matrix(GDN_HEADS, HEAD_DIM).T)
    o, conv_new, s_new = out
    return o.reshape(n_seq * n_chunks * L, GDN_WIDTH), conv_new, s_new.reshape(n_seq, GDN_HEADS, GDN_DK, HEAD_DIM)


def _ssd_body(xbc_ref, z_ref, small_ref, wconv_ref, bconv_ref, alog_ref, dtb_ref, dskip_ref, gnorm_ref,
              conv0_ref, h0_ref, tril_ref, expc_ref,
              o_ref, convn_ref, hn_ref, xfull_ref, ht_ref, *, n_chunks, tv_last):
    L = SSD_CHUNK
    W = SSD_WIDTH
    GW = W // SSD_GROUPS
    c = pl.program_id(1)

    @pl.when(c == 0)
    def _():
        xfull_ref[0:CONV_PAD, :] = conv0_ref[...]
        for j in range(W // LANE):
            ht_ref[:, j * LANE:(j + 1) * LANE] = h0_ref[j * LANE:(j + 1) * LANE, :].T

    xs = _silu(_chunk_conv(xbc_ref, xfull_ref, wconv_ref, L) + bconv_ref[...])
    x_c, bm, cm = xs[:, 0:W], xs[:, W:W + GW], xs[:, W + GW:W + 2 * GW]
    small = small_ref[...]
    lane = lax.broadcasted_iota(jnp.int32, (L, LANE), 1)
    dt_s = jnp.where((lane >= SMALL_DT) & (lane < SMALL_DT + SSD_HEADS), _softplus(small + dtb_ref[...]), 0.0)
    if tv_last < L:
        last_rows = jnp.where(c == n_chunks - 1, tv_last, L)
        keep_s = lax.broadcasted_iota(jnp.int32, (L, LANE), 0) < last_rows
        keep_g = lax.broadcasted_iota(jnp.int32, (L, GW), 0) < last_rows
        dt_s = jnp.where(keep_s, dt_s, 0.0)
        bm, cm = jnp.where(keep_g, bm, 0.0), jnp.where(keep_g, cm, 0.0)
    ac_s = dt_s * (-jnp.exp(alog_ref[...]))
    acs_s = _dot3_right(tril_ref[...], ac_s)
    acs_e = _dot3_left(acs_s, expc_ref[...])
    dt_e = _dot3_left(dt_s, expc_ref[...])
    xdt = x_c * dt_e
    e = jnp.exp(acs_e)
    last = acs_e[L - 1:L, :]
    xw = xdt * jnp.exp(last - acs_e)
    dchunk = jnp.exp(last)
    acs_t = acs_s.T
    incl = lax.broadcasted_iota(jnp.int32, (L, L), 0) >= lax.broadcasted_iota(jnp.int32, (L, L), 1)
    first_half = lax.broadcasted_iota(jnp.int32, (L, LANE), 1) < HEAD_DIM
    ht = ht_ref[...]
    ys = []
    for g in range(SSD_GROUPS):
        gl = slice(g * GW, (g + 1) * GW)
        bg = bm[:, g * SSD_STATE:(g + 1) * SSD_STATE].astype(BF16)
        cg = cm[:, g * SSD_STATE:(g + 1) * SSD_STATE].astype(BF16)
        scores = _dot_nt(cg, bg)
        y_off = _dot(cg, ht[:, gl].astype(BF16)) * e[:, gl]
        ht_ref[:, gl] = ht[:, gl] * dchunk[:, gl] + _dot_tn(bg, xw[:, gl].astype(BF16))
        pairs = []
        for pp in range(GW // LANE):
            xp = xdt[:, g * GW + pp * LANE:g * GW + (pp + 1) * LANE]
            yp = jnp.zeros((L, LANE), F32)
            for hh in range(2):
                h = g * (SSD_HEADS // SSD_GROUPS) + pp * 2 + hh
                col = acs_s[:, SMALL_DT + h:SMALL_DT + h + 1]
                rowv = acs_t[SMALL_DT + h:SMALL_DT + h + 1, :]
                decay = jnp.exp(jnp.where(incl, col - rowv, NEG_BIG))
                xm = jnp.where(first_half if hh == 0 else jnp.logical_not(first_half), xp, 0.0)
                yp = yp + _dot((scores * decay).astype(BF16), xm.astype(BF16))
            pairs.append(yp)
        ys.append(jnp.concatenate(pairs, axis=1) + y_off)
    y = (jnp.concatenate(ys, axis=1) + dskip_ref[...] * x_c) * _silu(z_ref[...])
    outs = []
    for g in range(SSD_GROUPS):
        yg = y[:, g * GW:(g + 1) * GW]
        outs.append(yg * lax.rsqrt(jnp.mean(yg * yg, axis=-1, keepdims=True) + RMS_EPS))
    o_ref[...] = jnp.concatenate(outs, axis=1) * gnorm_ref[...]

    @pl.when(c == n_chunks - 1)
    def _():
        convn_ref[...] = xfull_ref[CONV_PAD + tv_last - (CONV_WIDTH - 1):CONV_PAD + tv_last, :]
        for j in range(W // LANE):
            hn_ref[j * LANE:(j + 1) * LANE, :] = ht_ref[:, j * LANE:(j + 1) * LANE].T

    xfull_ref[0:CONV_PAD, :] = xfull_ref[L:L + CONV_PAD, :]


def _ssd_call(u, n_seq, rows_per_seq, t_valid, conv0, h0, w_conv, b_conv, a_log, dt_bias, d_skip, g_norm):
    L = SSD_CHUNK
    n_chunks = -(-t_valid // L)
    tv_last = t_valid - (n_chunks - 1) * L
    rb = rows_per_seq // L
    lane_row = lambda vals, off: jnp.zeros((1, LANE), F32).at[0, off:off + vals.shape[0]].set(vals)
    conv0_p = jnp.pad(conv0, ((0, 0), (CONV_PAD - (CONV_WIDTH - 1), 0), (0, 0)))
    row = lambda w, col: pl.BlockSpec((L, w), lambda b, c: (b * rb + c, col))
    full = lambda r, cc: pl.BlockSpec((r, cc), lambda b, c: (0, 0))
    per_seq = lambda r, cc: pl.BlockSpec((None, r, cc), lambda b, c: (b, 0, 0))
    out = pl.pallas_call(
        functools.partial(_ssd_body, n_chunks=n_chunks, tv_last=tv_last),
        grid=(n_seq, n_chunks),
        in_specs=[row(SSD_XBC, OFF_XBC // SSD_XBC), row(SSD_WIDTH, OFF_ZC // SSD_WIDTH), row(LANE, OFF_SMALL // LANE),
                  full(CONV_WIDTH, SSD_XBC), full(1, SSD_XBC), full(1, LANE), full(1, LANE),
                  full(1, SSD_WIDTH), full(1, SSD_WIDTH),
                  per_seq(CONV_PAD, SSD_XBC), per_seq(SSD_WIDTH, SSD_STATE),
                  full(L, L), full(LANE, SSD_WIDTH)],
        out_specs=[pl.BlockSpec((L, SSD_WIDTH), lambda b, c: (b * n_chunks + c, 0)),
                   per_seq(CONV_WIDTH - 1, SSD_XBC), per_seq(SSD_WIDTH, SSD_STATE)],
        out_shape=[jax.ShapeDtypeStruct((n_seq * n_chunks * L, SSD_WIDTH), F32),
                   jax.ShapeDtypeStruct((n_seq, CONV_WIDTH - 1, SSD_XBC), F32),
                   jax.ShapeDtypeStruct((n_seq, SSD_WIDTH, SSD_STATE), F32)],
        scratch_shapes=[pltpu.VMEM((L + CONV_PAD, SSD_XBC), F32), pltpu.VMEM((SSD_STATE, SSD_WIDTH), F32)],
        compiler_params=_params(("parallel", "arbitrary")), name="ssd",
    )(u, u, u, w_conv, b_conv.reshape(1, SSD_XBC), lane_row(a_log, SMALL_DT), lane_row(dt_bias, SMALL_DT),
      jnp.repeat(d_skip, HEAD_DIM).reshape(1, SSD_WIDTH), g_norm.reshape(1, SSD_WIDTH),
      conv0_p, h0.reshape(n_seq, SSD_WIDTH, SSD_STATE), _tril_incl(L), _expand_matrix(SMALL_DT, SSD_HEADS, HEAD_DIM))
    o, conv_new, h_new = out
    return o, conv_new, h_new.reshape(n_seq, SSD_HEADS, HEAD_DIM, SSD_STATE)


ROUTE_GATE, ROUTE_IDX, ROUTE_RANK = 0, TOP_K, 2 * TOP_K


def _strict_lower(n):
    return jnp.asarray(np.tril(np.ones((n, n), bool), -1), BF16)


def _proj_out_body(x_ref, oa_ref, ob_ref, oc_ref, w_ref, g_ref, b_ref, rw_ref, rb_ref, lower_ref,
                   x1_ref, route_ref, count_ref):
    mix = _dot(oa_ref[...].astype(BF16), w_ref[0:256, :])
    mix += _dot(ob_ref[...].astype(BF16), w_ref[256:512, :])
    mix += _dot(oc_ref[...].astype(BF16), w_ref[512:1024, :])
    x1 = _layer_norm_f32(DEEPNORM_ALPHA * x_ref[...] + mix, g_ref[...], b_ref[...])
    x1_ref[...] = x1
    logits = _mm3(x1, rw_ref[...]) + rb_ref[...]

    @pl.when(pl.program_id(0) == 0)
    def _():
        count_ref[...] = jnp.zeros_like(count_ref)

    lane = lax.broadcasted_iota(jnp.int32, logits.shape, 1)
    lg = jnp.where(lane < N_EXPERTS, logits, -jnp.inf)
    vals, idxs = [], []
    for _ in range(TOP_K):
        m = jnp.max(lg, axis=1, keepdims=True)
        idx = jnp.min(jnp.where(lg == m, lane, LANE), axis=1, keepdims=True)
        vals.append(m)
        idxs.append(idx)
        lg = jnp.where(lane == idx, -jnp.inf, lg)
    exps = [jnp.exp(v - vals[0]) for v in vals]
    denom = exps[0] + exps[1] + exps[2] + exps[3]
    hot = [lane == idx for idx in idxs]
    multi = jnp.where(hot[0] | hot[1] | hot[2] | hot[3], 1.0, 0.0)
    before = _dot(lower_ref[...], multi.astype(BF16)) + count_ref[...]
    route = jnp.zeros(logits.shape, F32)
    for k in range(TOP_K):
        rank = jnp.sum(jnp.where(hot[k], before, 0.0), axis=1, keepdims=True)
        route = jnp.where(lane == ROUTE_GATE + k, exps[k] / denom, route)
        route = jnp.where(lane == ROUTE_IDX + k, idxs[k].astype(F32), route)
        route = jnp.where(lane == ROUTE_RANK + k, rank, route)
    route_ref[...] = route
    count_ref[...] += jnp.sum(multi, axis=0, keepdims=True)


def _proj_out_call(x, o_a, o_b, o_c, w_bf16, g, b, router_w, router_b):
    n, d = x.shape
    bm = _row_block(n)
    rw = jnp.pad(router_w, ((0, 0), (0, LANE - N_EXPERTS)))
    rb = jnp.pad(router_b, (0, LANE - N_EXPERTS)).reshape(1, LANE)
    row = lambda w: pl.BlockSpec((bm, w), lambda i: (i, 0))
    full = lambda r, c: pl.BlockSpec((r, c), lambda i: (0, 0))
    return pl.pallas_call(
        _proj_out_body, grid=(n // bm,),
        in_specs=[row(d), row(SB_WIDTH), row(GDN_WIDTH), row(SSD_WIDTH), full(d, d), full(1, d), full(1, d),
                  full(d, LANE), full(1, LANE), full(bm, bm)],
        out_specs=[row(d), row(LANE), full(1, LANE)],
        out_shape=[jax.ShapeDtypeStruct((n, d), F32), jax.ShapeDtypeStruct((n, LANE), F32),
                   jax.ShapeDtypeStruct((1, LANE), F32)],
        compiler_params=_params(("arbitrary",)), name="proj_out",
    )(x, o_a, o_b, o_c, w_bf16, g.reshape(1, d), b.reshape(1, d), rw, rb, _strict_lower(bm))


def _moe_body(be_ref, nb_ref, rows_ref, wg_ref, bg_ref, wu_ref, bu_ref, wd_ref, bd_ref, o_ref, wg_s, wu_s, wd_s):
    i = pl.program_id(0)
    prev = be_ref[jnp.maximum(i - 1, 0)]

    @pl.when(jnp.logical_or(i == 0, be_ref[i] != prev))
    def _():
        wg_s[...] = wg_ref[...].astype(BF16)
        wu_s[...] = wu_ref[...].astype(BF16)
        wd_s[...] = wd_ref[...].astype(BF16)

    @pl.when(i < nb_ref[0])
    def _():
        xb = rows_ref[...].astype(BF16)
        glu = jnp.minimum(_dot(xb, wg_s[...]) + bg_ref[...], SWIGLU_LIMIT)
        lin = jnp.clip(_dot(xb, wu_s[...]) + bu_ref[...], -SWIGLU_LIMIT, SWIGLU_LIMIT)
        act = glu * jax.nn.sigmoid(SWIGLU_ALPHA * glu) * (lin + 1.0)
        o_ref[...] = _dot(act.astype(BF16), wd_s[...]) + bd_ref[...]

    @pl.when(i >= nb_ref[0])
    def _():
        o_ref[...] = jnp.zeros_like(o_ref)


def _moe_call(rows_in, block_expert, n_used, w_gate, b_gate, w_up, b_up, w_down, b_down, layer, blk):
    n_rows, d = rows_in.shape
    n_blocks = n_rows // blk
    wspec = lambda r, c: pl.BlockSpec((None, None, r, c), lambda i, be, nb: (layer, be[i], 0, 0))
    grid_spec = pltpu.PrefetchScalarGridSpec(
        num_scalar_prefetch=2, grid=(n_blocks,),
        in_specs=[pl.BlockSpec((blk, d), lambda i, be, nb: (i, 0)),
                  wspec(d, D_EXPERT), wspec(1, D_EXPERT), wspec(d, D_EXPERT), wspec(1, D_EXPERT),
                  wspec(D_EXPERT, d), wspec(1, d)],
        out_specs=pl.BlockSpec((blk, d), lambda i, be, nb: (i, 0)),
        scratch_shapes=[pltpu.VMEM((d, D_EXPERT), BF16), pltpu.VMEM((d, D_EXPERT), BF16),
                        pltpu.VMEM((D_EXPERT, d), BF16)])
    depth = w_gate.shape[0]
    return pl.pallas_call(
        _moe_body, grid_spec=grid_spec,
        out_shape=jax.ShapeDtypeStruct((n_rows, d), F32),
        compiler_params=_params(("arbitrary",)), name="moe_experts",
    )(block_expert, n_used, rows_in, w_gate, b_gate.reshape(depth, N_EXPERTS, 1, D_EXPERT),
      w_up, b_up.reshape(depth, N_EXPERTS, 1, D_EXPERT), w_down, b_down.reshape(depth, N_EXPERTS, 1, d))


def _route(route, counts, blk):
    n_tok = route.shape[0]
    gate = route[:, ROUTE_GATE:ROUTE_GATE + TOP_K]
    expert = route[:, ROUTE_IDX:ROUTE_IDX + TOP_K].astype(jnp.int32)
    rank = route[:, ROUTE_RANK:ROUTE_RANK + TOP_K].astype(jnp.int32)
    counts = counts[0, :N_EXPERTS].astype(jnp.int32)
    n_assign = n_tok * TOP_K
    n_blocks = -(-(n_assign + N_EXPERTS * (blk - 1)) // blk)
    padded = (counts + blk - 1) // blk * blk
    padded_end = jnp.cumsum(padded)
    padded_start = padded_end - padded
    start = jnp.cumsum(counts) - counts
    dest = padded_start[expert] + rank
    token = jnp.broadcast_to(jnp.arange(n_tok, dtype=jnp.int32)[:, None], (n_tok, TOP_K))
    _, token_sorted = lax.sort((dest.reshape(-1), token.reshape(-1)), num_keys=1)
    block_first = jnp.arange(n_blocks, dtype=jnp.int32) * blk
    block_expert = jnp.minimum(jnp.sum(padded_end[None, :] <= block_first[:, None], axis=1, dtype=jnp.int32),
                               N_EXPERTS - 1)
    r = jnp.arange(n_blocks * blk, dtype=jnp.int32)
    e_r = jnp.repeat(block_expert, blk)
    src = token_sorted[jnp.clip(start[e_r] + r - padded_start[e_r], 0, n_assign - 1)]
    dest_t = dest.T.reshape(-1)
    n_used = (padded_end[-1] // blk).astype(jnp.int32).reshape(1)
    return gate, src, dest_t, block_expert, n_used


def _post_body(x1_ref, r0_ref, r1_ref, r2_ref, r3_ref, gate_ref, p_ref, g_ref, b_ref, wg_ref, bg_ref, wp_ref, o_ref):
    gate = gate_ref[...]
    ffn = gate[:, 0:1] * r0_ref[...]
    for k, r_ref in ((1, r1_ref), (2, r2_ref), (3, r3_ref)):
        ffn += gate[:, k:k + 1] * r_ref[...]
    x2 = _layer_norm_f32(DEEPNORM_ALPHA * x1_ref[...] + ffn, g_ref[...], b_ref[...])
    gate_logit = _dot(x2.astype(BF16), wg_ref[...]) + bg_ref[...]
    emb = _dot(p_ref[...].astype(BF16), wp_ref[...])
    o_ref[...] = x2 + jax.nn.sigmoid(gate_logit) * emb


def _post_call(x1, rows_k, gate, p, g, b, wg_bf16, bg, wp_bf16):
    n, d = x1.shape
    bm = _row_block(n)
    nb = n // bm
    row = lambda w: pl.BlockSpec((bm, w), lambda i: (i, 0))
    plane = lambda k: pl.BlockSpec((bm, d), lambda i: (k * nb + i, 0))
    full = lambda r, c: pl.BlockSpec((r, c), lambda i: (0, 0))
    return pl.pallas_call(
        _post_body, grid=(nb,),
        in_specs=[row(d)] + [plane(k) for k in range(TOP_K)] + [row(TOP_K), row(PLE_DIM), full(1, d), full(1, d),
                                                                 full(d, d), full(1, d), full(PLE_DIM, d)],
        out_specs=row(d),
        out_shape=jax.ShapeDtypeStruct((n, d), F32),
        compiler_params=_params(("parallel",)), name="post_ffn",
    )(x1, rows_k, rows_k, rows_k, rows_k, gate, p, g.reshape(1, d), b.reshape(1, d), wg_bf16, bg.reshape(1, d),
      wp_bf16)


def kernel(x_prompt, x_sample, cache_k_a, cache_v_a, state_conv_b, state_delta_b, state_conv_c, state_ssm_c, page_table, p_prompt, p_sample, ln_in_g, ln_in_b, w_in, sb_bias, w_conv_b, a_log_b, dt_bias_b, g_norm_b, w_conv_c, b_conv_c, a_log_c, dt_bias_c, d_skip_c, g_norm_c, w_out, ln1_g, ln1_b, router_w, router_b, w_gate, b_gate, w_up, b_up, w_down, b_down, ln2_g, ln2_b, ple_gate_w, ple_gate_b, ple_proj_w):
    depth = w_in.shape[0]
    w_in_r = jnp.pad(w_in[:, :, _IN_PERM], ((0, 0), (0, 0), (0, U_WIDTH - _IN_PERM.size))).astype(BF16)
    w_out_b = w_out.astype(BF16)
    ple_gate_w_b = ple_gate_w.astype(BF16)
    ple_proj_w_b = ple_proj_w.astype(BF16)
    n_pool = cache_k_a.shape[1]
    cache_k = cache_k_a.transpose(0, 1, 3, 4, 2).reshape(depth, n_pool, SB_WIDTH, PAGE_SIZE)
    cache_v = cache_v_a.transpose(0, 1, 3, 4, 2).reshape(depth, n_pool, SB_WIDTH, PAGE_SIZE)

    def run_group(x, p, sample):
        bsz, t, d = x.shape
        n = bsz * t
        x = _ln_call(x.reshape(n, d), ln_in_g, ln_in_b)
        new = []
        for i in range(depth):
            u = _proj_in_call(x, w_in_r[i])
            k_a = u[:, OFF_K:OFF_K + SB_WIDTH].reshape(bsz, t, SB_HEADS, HEAD_DIM)
            v_a = u[:, OFF_V:OFF_V + SB_WIDTH].reshape(bsz, t, SB_HEADS, HEAD_DIM)
            if sample:
                o_a = _sb_decode_call(u[:, OFF_Q:OFF_Q + SB_WIDTH], sb_bias[i], cache_k, cache_v, page_table, i)
                conv_b, delta_b, conv_c, ssm_c = state_conv_b[i], state_delta_b[i], state_conv_c[i], state_ssm_c[i]
                rows_per_seq = SSD_CHUNK
                u_mix = jnp.pad(u.reshape(bsz, t, U_WIDTH), ((0, 0), (0, rows_per_seq - t), (0, 0)))
                u_mix = u_mix.reshape(bsz * rows_per_seq, U_WIDTH)
            else:
                o_a = _sb_prefill_call(u, sb_bias[i], bsz, t)
                conv_b = jnp.zeros((bsz, CONV_WIDTH - 1, GDN_CONV_DIM), F32)
                delta_b = jnp.zeros((bsz, GDN_HEADS, GDN_DK, HEAD_DIM), F32)
                conv_c = jnp.zeros((bsz, CONV_WIDTH - 1, SSD_XBC), F32)
                ssm_c = jnp.zeros((bsz, SSD_HEADS, HEAD_DIM, SSD_STATE), F32)
                rows_per_seq = t
                u_mix = u
            o_b, conv_b_new, delta_new = _gdn_call(u_mix, bsz, rows_per_seq, t, conv_b, delta_b, w_conv_b[i],
                                                   a_log_b[i], dt_bias_b[i], g_norm_b[i])
            o_c, conv_c_new, ssm_new = _ssd_call(u_mix, bsz, rows_per_seq, t, conv_c, ssm_c, w_conv_c[i], b_conv_c[i],
                                                 a_log_c[i], dt_bias_c[i], d_skip_c[i], g_norm_c[i])
            if sample:
                o_b = o_b.reshape(bsz, -1, GDN_WIDTH)[:, :t].reshape(n, GDN_WIDTH)
                o_c = o_c.reshape(bsz, -1, SSD_WIDTH)[:, :t].reshape(n, SSD_WIDTH)
            x1, route, counts = _proj_out_call(x, o_a, o_b, o_c, w_out_b[i], ln1_g[i], ln1_b[i], router_w[i],
                                               router_b[i])
            blk = MOE_BLOCK_PROMPT if n * TOP_K >= MOE_BLOCK_PROMPT * N_EXPERTS else 8
            gate, src, dest_t, block_expert, n_used = _route(route, counts, blk)
            rows_out = _moe_call(x1[src], block_expert, n_used, w_gate, b_gate, w_up, b_up, w_down, b_down, i, blk)
            x = _post_call(x1, rows_out[dest_t], gate, p[i].reshape(n, PLE_DIM), ln2_g[i], ln2_b[i], ple_gate_w_b[i],
                           ple_gate_b[i], ple_proj_w_b[i])
            new.append((k_a, v_a, conv_b_new, delta_new, conv_c_new, ssm_new))
        return x.reshape(bsz, t, d), [jnp.stack([s[j] for s in new]) for j in range(6)]

    y_prompt, new_prompt = run_group(x_prompt, p_prompt, False)
    y_sample, new_sample = run_group(x_sample, p_sample, True)
    return (y_prompt, y_sample, *new_prompt, *new_sample)
```

```python
import functools

import jax
import jax.numpy as jnp
import numpy as np
from jax import lax
from jax.experimental import pallas as pl
from jax.experimental.pallas import tpu as pltpu

F32 = jnp.float32
BF16 = jnp.bfloat16

D_MODEL = 1024
DEPTH = 4
PAGE_SIZE = 128
HEAD_DIM = 64
CONV_WIDTH = 4
SB_WIDTH = 256
SB_HEADS = 4
GDN_WIDTH = 256
GDN_HEADS = 4
GDN_DK = 64
GDN_CONV_DIM = 768
GDN_CHUNK = 64
SSD_WIDTH = 512
SSD_HEADS = 8
SSD_GROUPS = 2
SSD_STATE = 128
SSD_XBC = 1024
SSD_CHUNK = 128
N_EXPERTS = 32
TOP_K = 4
D_EXPERT = 512
SWIGLU_ALPHA = 1.702
SWIGLU_LIMIT = 7.0
PLE_DIM = 256
DEEPNORM_ALPHA = (2 * DEPTH) ** 0.25
LN_EPS = 1e-5
RMS_EPS = 1e-6

_REF_WIDTHS = (256, 256, 256, 768, 256, 4, 4, 512, 1024, 8)
_REF_OFF = tuple(int(v) for v in np.cumsum((0,) + _REF_WIDTHS))
OFF_XBC, OFF_ZC, OFF_QKVB, OFF_Q, OFF_K, OFF_V, OFF_ZB, OFF_SMALL = 0, 1024, 1536, 2304, 2560, 2816, 3072, 3328
U_WIDTH = 3456
LANE = 128
SMALL_BETA, SMALL_DECAY, SMALL_DT = 0, GDN_HEADS, 2 * GDN_HEADS
_IN_PERM = np.concatenate([
    np.arange(_REF_OFF[8], _REF_OFF[9]),
    np.arange(_REF_OFF[7], _REF_OFF[8]),
    np.arange(_REF_OFF[3], _REF_OFF[4]),
    np.arange(_REF_OFF[0], _REF_OFF[3]),
    np.arange(_REF_OFF[4], _REF_OFF[5]),
    np.arange(_REF_OFF[5], _REF_OFF[7]),
    np.arange(_REF_OFF[9], _REF_OFF[10]),
])

VMEM_LIMIT = 48 * 1024 * 1024
MOE_BLOCK_PROMPT = 512
MOE_BLOCK_SMALL = 16
SB_BLOCK = 256
DEC_PAGES_PER_STEP = 16
CONV_PAD = 8
NEG_BIG = -1e30
LOG2E = 1.4426950408889634


def _params(sem):
    return pltpu.CompilerParams(dimension_semantics=sem, vmem_limit_bytes=VMEM_LIMIT)


def _row_block(n):
    return n if n < 512 else 512


def _layer_norm_f32(x, g, b):
    mu = jnp.mean(x, axis=-1, keepdims=True)
    xc = x - mu
    var = jnp.mean(xc * xc, axis=-1, keepdims=True)
    return xc * lax.rsqrt(var + LN_EPS) * g + b


def _softplus(z):
    return jnp.maximum(z, 0.0) + jnp.log1p(jnp.exp(-jnp.abs(z)))


def _softplus2(z2):
    return jnp.maximum(z2, 0.0) + jnp.log2(1.0 + jnp.exp2(-jnp.abs(z2)))


def _silu(x):
    return x * jax.nn.sigmoid(x)


def _dot(a, b):
    return jnp.dot(a, b, preferred_element_type=F32)


def _dot_nt(a, b):
    return lax.dot_general(a, b, (((1,), (1,)), ((), ())), preferred_element_type=F32)


def _dot_tn(a, b):
    return lax.dot_general(a, b, (((0,), (0,)), ((), ())), preferred_element_type=F32)


def _split2(x):
    hi = x.astype(BF16)
    return hi, (x - hi.astype(F32)).astype(BF16)


def _split3(x):
    hi = x.astype(BF16)
    r = x - hi.astype(F32)
    mid = r.astype(BF16)
    return hi, mid, (r - mid.astype(F32)).astype(BF16)


def _split_dot(x, w_bf16):
    hi, lo = _split2(x)
    return _dot(hi, w_bf16) + _dot(lo, w_bf16)


def _dot3_left(x, w_bf16):
    hi, mid, lo = _split3(x)
    return _dot(hi, w_bf16) + (_dot(mid, w_bf16) + _dot(lo, w_bf16))


def _dot3_right(w_bf16, x):
    hi, mid, lo = _split3(x)
    return _dot(w_bf16, hi) + (_dot(w_bf16, mid) + _dot(w_bf16, lo))


def _mm3(a, b):
    ah, al = _split2(a)
    bh, bl = _split2(b)
    return _dot(ah, bh) + (_dot(ah, bl) + _dot(al, bh))


def _cumsum_matrix(n):
    j = np.arange(n)[:, None]
    s = np.arange(n)[None, :]
    return jnp.asarray(np.concatenate([(j > s), np.ones((n, LANE), bool)], axis=1), BF16)


def _tril_incl(n):
    return jnp.asarray(np.tril(np.ones((n, n), bool)), BF16)


def _expand_matrix(first_lane, heads, width):
    m = np.zeros((LANE, heads * width), bool)
    for h in range(heads):
        m[first_lane + h, h * width:(h + 1) * width] = True
    return jnp.asarray(m, BF16)


def _block_diag_ones(n, blk):
    i = np.arange(n)
    return jnp.asarray((i[:, None] // blk) == (i[None, :] // blk), BF16)


def _fold_matrix(heads, width):
    i = np.arange(heads * width)
    return jnp.asarray((i[:, None] % width) == np.arange(width)[None, :], BF16)


def _ln_body(x_ref, g_ref, b_ref, o_ref):
    o_ref[...] = _layer_norm_f32(x_ref[...], g_ref[...], b_ref[...])


def _ln_call(x, g, b):
    n, d = x.shape
    bm = _row_block(n)
    return pl.pallas_call(
        _ln_body, grid=(n // bm,),
        in_specs=[pl.BlockSpec((bm, d), lambda i: (i, 0)),
                  pl.BlockSpec((1, d), lambda i: (0, 0)), pl.BlockSpec((1, d), lambda i: (0, 0))],
        out_specs=pl.BlockSpec((bm, d), lambda i: (i, 0)),
        out_shape=jax.ShapeDtypeStruct((n, d), F32),
        compiler_params=_params(("parallel",)), name="ln_in",
    )(x, g.reshape(1, d), b.reshape(1, d))


_PROJ_CHUNK = 384


def _proj_in_body(x_ref, w_ref, o_ref):
    xb = x_ref[...].astype(BF16)
    for j in range(0, U_WIDTH, _PROJ_CHUNK):
        o_ref[:, j:j + _PROJ_CHUNK] = _dot(xb, w_ref[:, j:j + _PROJ_CHUNK])


def _proj_in_call(x, w_bf16):
    n, d = x.shape
    bm = _row_block(n)
    return pl.pallas_call(
        _proj_in_body, grid=(n // bm,),
        in_specs=[pl.BlockSpec((bm, d), lambda i: (i, 0)),
                  pl.BlockSpec((d, U_WIDTH), lambda i: (0, 0))],
        out_specs=pl.BlockSpec((bm, U_WIDTH), lambda i: (i, 0)),
        out_shape=jax.ShapeDtypeStruct((n, U_WIDTH), F32),
        compiler_params=_params(("parallel",)), name="proj_in",
    )(x, w_bf16)


def _sb_prefill_body(bias_ref, q_ref, k_ref, v_ref, u_ref, o_ref, qh_ref, acc_ref, carry_ref):
    blk = q_ref.shape[0]
    i = pl.program_id(1)
    q = q_ref[...] * (HEAD_DIM ** -0.5 * LOG2E)
    lane_head = lax.broadcasted_iota(jnp.int32, (1, SB_WIDTH), 1) // HEAD_DIM
    for h in range(SB_HEADS):
        qh_ref[h] = jnp.where(lane_head == h, q, 0.0).astype(BF16)
    acc_ref[...] = jnp.zeros_like(acc_ref)
    carry_ref[...] = jnp.zeros_like(carry_ref)
    rows = lax.broadcasted_iota(jnp.int32, (blk, blk), 0)
    cols = lax.broadcasted_iota(jnp.int32, (blk, blk), 1)

    def key_block(j, diagonal):
        start = pl.multiple_of(j * blk, blk)
        kj = k_ref[pl.ds(start, blk), :].astype(BF16)
        vj = v_ref[pl.ds(start, blk), :].astype(BF16)
        for h in range(SB_HEADS):
            z = _dot_nt(qh_ref[h], kj) + bias_ref[h]
            log_keep = -_softplus2(z)
            logit = z + log_keep
            if diagonal:
                visible = cols < rows
                log_keep = jnp.where(visible, log_keep, 0.0)
            lt = _split_dot(log_keep, u_ref[...])
            c = carry_ref[h]
            later = lt + jnp.concatenate([c] * (blk // LANE), axis=1)
            w = jnp.exp2(logit + later)
            if diagonal:
                w = jnp.where(visible, w, 0.0)
            acc_ref[h] += _dot(w.astype(BF16), vj)
            carry_ref[h] = c + (lt[:, 0:1] + log_keep[:, 0:1])

    key_block(i, True)

    def earlier(t, carry):
        key_block(i - 1 - t, False)
        return carry

    lax.fori_loop(0, i, earlier, 0)
    out = jnp.zeros((blk, SB_WIDTH), F32)
    for h in range(SB_HEADS):
        out = out + jnp.where(lane_head == h, acc_ref[h], 0.0)
    o_ref[...] = out


def _sb_prefill_call(u, bias, bsz, t):
    blk = SB_BLOCK
    nq = t // blk
    grid_spec = pltpu.PrefetchScalarGridSpec(
        num_scalar_prefetch=0, grid=(bsz, nq),
        in_specs=[pl.BlockSpec(memory_space=pltpu.SMEM),
                  pl.BlockSpec((blk, SB_WIDTH), lambda b, i: (b * nq + i, OFF_Q // SB_WIDTH)),
                  pl.BlockSpec((t, SB_WIDTH), lambda b, i: (b, OFF_K // SB_WIDTH)),
                  pl.BlockSpec((t, SB_WIDTH), lambda b, i: (b, OFF_V // SB_WIDTH)),
                  pl.BlockSpec((blk, blk), lambda b, i: (0, 0))],
        out_specs=pl.BlockSpec((blk, SB_WIDTH), lambda b, i: (b * nq + i, 0)),
        scratch_shapes=[pltpu.VMEM((SB_HEADS, blk, SB_WIDTH), BF16),
                        pltpu.VMEM((SB_HEADS, blk, SB_WIDTH), F32),
                        pltpu.VMEM((SB_HEADS, blk, LANE), F32)])
    return pl.pallas_call(
        _sb_prefill_body, grid_spec=grid_spec,
        out_shape=jax.ShapeDtypeStruct((bsz * t, SB_WIDTH), F32),
        compiler_params=_params(("parallel", "arbitrary")), name="sb_prefill",
    )(bias * LOG2E, u, u, u, _cumsum_matrix(blk)[:, :blk])


def _sb_decode_body(pt_ref, q_ref, bias_ref, u_ref, *refs):
    g = DEC_PAGES_PER_STEP
    k_refs, v_refs = refs[:g], refs[g:2 * g]
    o_ref, acc_ref, carry_ref = refs[2 * g], refs[2 * g + 1], refs[2 * g + 2]
    s = pl.program_id(1)

    @pl.when(s == 0)
    def _():
        acc_ref[...] = jnp.zeros_like(acc_ref)
        carry_ref[...] = jnp.zeros_like(carry_ref)

    row = lax.broadcasted_iota(jnp.int32, (8, SB_WIDTH), 0)
    lane_head = lax.broadcasted_iota(jnp.int32, (8, SB_WIDTH), 1) // HEAD_DIM
    own = row == lane_head
    qm = jnp.where(own, q_ref[...] * (HEAD_DIM ** -0.5 * LOG2E), 0.0).astype(BF16)
    acc = acc_ref[...]
    carry = carry_ref[...]
    zs = [_dot(qm, k_refs[c][...].astype(BF16)) + bias_ref[...] for c in range(g)]
    log_keeps = [-_softplus2(z) for z in zs]
    lt_all = _split_dot(jnp.concatenate(log_keeps, axis=0), u_ref[...])
    ws = []
    for c in range(g):
        lt = lt_all[8 * c:8 * c + 8]
        ws.append(jnp.exp2(zs[c] + log_keeps[c] + lt[:, :LANE] + carry).astype(BF16))
        carry = carry + lt[:, LANE:]
    for c in range(g):
        acc = acc + _dot_nt(ws[c], v_refs[c][...].astype(BF16))
    acc_ref[...] = acc
    carry_ref[...] = carry

    @pl.when(s == pl.num_programs(1) - 1)
    def _():
        o_ref[...] = jnp.sum(jnp.where(own, acc, 0.0), axis=0, keepdims=True)


def _sb_decode_call(q, bias, cache_k, cache_v, page_table, layer):
    bsz = q.shape[0]
    n_pages = page_table.shape[1]
    g = DEC_PAGES_PER_STEP
    n_steps = n_pages // g
    bias8 = jnp.zeros((8, LANE), F32).at[:SB_HEADS].set(jnp.broadcast_to(bias[:, None] * LOG2E, (SB_HEADS, LANE)))

    def page_spec(c):
        return pl.BlockSpec((None, None, SB_WIDTH, PAGE_SIZE),
                            lambda b, s, pt: (layer, pt[b, n_pages - 1 - (s * g + c)], 0, 0))

    grid_spec = pltpu.PrefetchScalarGridSpec(
        num_scalar_prefetch=1, grid=(bsz, n_steps),
        in_specs=[pl.BlockSpec((None, 1, SB_WIDTH), lambda b, s, pt: (b, 0, 0)),
                  pl.BlockSpec((8, LANE), lambda b, s, pt: (0, 0)),
                  pl.BlockSpec((PAGE_SIZE, 2 * LANE), lambda b, s, pt: (0, 0))]
                 + [page_spec(c) for c in range(g)] + [page_spec(c) for c in range(g)],
        out_specs=pl.BlockSpec((None, 1, SB_WIDTH), lambda b, s, pt: (b, 0, 0)),
        scratch_shapes=[pltpu.VMEM((8, SB_WIDTH), F32), pltpu.VMEM((8, LANE), F32)])
    out = pl.pallas_call(
        _sb_decode_body, grid_spec=grid_spec,
        out_shape=jax.ShapeDtypeStruct((bsz, 1, SB_WIDTH), F32),
        compiler_params=_params(("parallel", "arbitrary")), name="sb_decode",
    )(page_table, q.reshape(bsz, 1, SB_WIDTH), bias8, _cumsum_matrix(PAGE_SIZE),
      *([cache_k] * g), *([cache_v] * g))
    return out.reshape(bsz, SB_WIDTH)


def _chunk_conv(x_ref, xfull_ref, wconv_ref, rows):
    xfull_ref[CONV_PAD:CONV_PAD + rows, :] = x_ref[...]
    acc = wconv_ref[CONV_WIDTH - 1:CONV_WIDTH, :] * xfull_ref[CONV_PAD:CONV_PAD + rows, :]
    for j in range(1, CONV_WIDTH):
        acc = acc + wconv_ref[CONV_WIDTH - 1 - j:CONV_WIDTH - j, :] * xfull_ref[CONV_PAD - j:CONV_PAD - j + rows, :]
    return acc


GDN_SEQS_PER_STEP = 4


def _mm_split(a_parts, b_parts):
    (ah, al), (bh, bl) = a_parts, b_parts
    return _dot(ah, bh) + (_dot(ah, bl) + _dot(al, bh))


def _same_head_mask():
    r2 = lax.broadcasted_iota(jnp.int32, (GDN_WIDTH, GDN_WIDTH), 0)
    c2 = lax.broadcasted_iota(jnp.int32, (GDN_WIDTH, GDN_WIDTH), 1)
    return r2, c2, (r2 // GDN_CHUNK) == (c2 // GDN_CHUNK)


def _gdn_prepare(c, qkv_ref, small_ref, wconv_ref, alog_ref, dtb_ref,
                 tril_ref, expb_ref, expg_ref, bd_ref, xfull_ref, *, n_chunks, tv_last):
    L = GDN_CHUNK
    W = GDN_WIDTH
    r2, c2, bd = _same_head_mask()
    xs = _silu(_chunk_conv(qkv_ref, xfull_ref, wconv_ref, L))
    q, k, v = xs[:, 0:W], xs[:, W:2 * W], xs[:, 2 * W:3 * W]
    small = small_ref[...]
    lane = lax.broadcasted_iota(jnp.int32, (L, LANE), 1)
    beta_s = jnp.where(lane < SMALL_DECAY, jax.nn.sigmoid(small), 0.0)
    g_s = jnp.where((lane >= SMALL_DECAY) & (lane < SMALL_DT),
                    -jnp.exp(alog_ref[...]) * _softplus(small + dtb_ref[...]), 0.0)
    if tv_last < L:
        last_rows = jnp.where(c == n_chunks - 1, tv_last, L)
        keep_w = lax.broadcasted_iota(jnp.int32, (L, W), 0) < last_rows
        keep_s = lax.broadcasted_iota(jnp.int32, (L, LANE), 0) < last_rows
        q, k, v = jnp.where(keep_w, q, 0.0), jnp.where(keep_w, k, 0.0), jnp.where(keep_w, v, 0.0)
        beta_s, g_s = jnp.where(keep_s, beta_s, 0.0), jnp.where(keep_s, g_s, 0.0)
    bdm = bd_ref[...]
    qn = q * lax.rsqrt(_dot3_left(q * q, bdm) + RMS_EPS) * (GDN_DK ** -0.5)
    kn = k * lax.rsqrt(_dot3_left(k * k, bdm) + RMS_EPS)
    gcs_s = _dot3_right(tril_ref[...], g_s)
    beta_e = _dot3_left(beta_s, expb_ref[...])
    gcs_e = _dot3_left(gcs_s, expg_ref[...])
    eg = jnp.exp(gcs_e)
    kb = kn * beta_e
    tile4 = lambda a: jnp.concatenate([a] * GDN_HEADS, axis=0)
    stack = lambda a: jnp.where(bd, tile4(a), 0.0)
    rl = lax.broadcasted_iota(jnp.int32, (L, W), 0)
    cl = lax.broadcasted_iota(jnp.int32, (L, W), 1)
    grow = jnp.sum(jnp.where((cl % L) == rl, gcs_e, 0.0), axis=0, keepdims=True)
    incl = bd & ((r2 % L) >= (c2 % L))
    decay = jnp.exp(jnp.where(incl, tile4(gcs_e) - grow, NEG_BIG))
    kns = stack(kn).astype(BF16)
    a_mat = jnp.where((r2 % L) != (c2 % L), _dot_nt(stack(kb).astype(BF16), kns) * decay, 0.0)
    qk = _dot_nt(stack(qn).astype(BF16), kns) * decay
    return dict(a=a_mat, qk=qk.astype(BF16), q_dec=(qn * eg).astype(BF16), kn=kn, gcs_e=gcs_e,
                rhs_v=_split2(stack(v * beta_e)), rhs_k=_split2(stack(kb * eg)))


def _gdn_inverses(a_mats):
    r2, c2, _ = _same_head_mask()
    eye = jnp.where(r2 == c2, 1.0, 0.0)
    ps = [eye - a for a in a_mats]
    a_parts = [_split2(a) for a in a_mats]
    x_parts = [_split2(_mm_split(ap, ap)) for ap in a_parts]
    ps = [p + _mm_split(_split2(p), xp) for p, xp in zip(ps, x_parts)]
    for _ in range(4):
        x_parts = [_split2(_mm_split(xp, xp)) for xp in x_parts]
        ps = [p + _mm_split(_split2(p), xp) for p, xp in zip(ps, x_parts)]
    return [_split2(p) for p in ps]


def _gdn_finish(prep, p_parts, z_ref, gnorm_ref, bd_ref, o_ref, s_ref):
    L = GDN_CHUNK
    bd = _same_head_mask()[2]
    stack = lambda a: jnp.where(bd, jnp.concatenate([a] * GDN_HEADS, axis=0), 0.0)
    collapse = lambda m: (m[0:L] + m[L:2 * L]) + (m[2 * L:3 * L] + m[3 * L:4 * L])
    u_c = collapse(_mm_split(p_parts, prep["rhs_v"]))
    w_c = collapse(_mm_split(p_parts, prep["rhs_k"]))
    s = s_ref[...]
    sb = s.astype(BF16)
    v_new = u_c - _dot(w_c.astype(BF16), sb)
    o = _dot(prep["q_dec"], sb) + collapse(_dot(prep["qk"], stack(v_new).astype(BF16)))
    gcs_e = prep["gcs_e"]
    glast = gcs_e[L - 1:L, :]
    k_dec = prep["kn"] * jnp.exp(glast - gcs_e)
    s_ref[...] = s * jnp.exp(glast) + jnp.where(bd, _dot_tn(k_dec.astype(BF16), v_new.astype(BF16)), 0.0)
    on = o * lax.rsqrt(_dot3_left(o * o, bd_ref[...]) * (1.0 / HEAD_DIM) + RMS_EPS) * gnorm_ref[...]
    o_ref[...] = on * _silu(z_ref[...])


def _gdn_body(qkv_ref, z_ref, small_ref, wconv_ref, alog_ref, dtb_ref, gnorm_ref, conv0_ref, s0_ref,
              tril_ref, expb_ref, expg_ref, bd_ref, fold_ref, foldt_ref,
              o_ref, convn_ref, sn_ref, xfull_ref, s_ref, *, n_chunks, tv_last):
    L = GDN_CHUNK
    c = pl.program_id(1)
    seqs = range(GDN_SEQS_PER_STEP)

    @pl.when(c == 0)
    def _():
        bd = _same_head_mask()[2]
        for j in seqs:
            xfull_ref[j, 0:CONV_PAD, :] = conv0_ref[j]
            s_ref[j] = jnp.where(bd, _dot3_left(s0_ref[j], foldt_ref[...]), 0.0)

    preps = [_gdn_prepare(c, qkv_ref.at[j], small_ref.at[j], wconv_ref, alog_ref, dtb_ref,
                          tril_ref, expb_ref, expg_ref, bd_ref, xfull_ref.at[j], n_chunks=n_chunks, tv_last=tv_last)
             for j in seqs]
    inverses = _gdn_inverses([prep["a"] for prep in preps])
    for j in seqs:
        _gdn_finish(preps[j], inverses[j], z_ref.at[j], gnorm_ref, bd_ref, o_ref.at[j], s_ref.at[j])

    @pl.when(c == n_chunks - 1)
    def _():
        for j in seqs:
            convn_ref[j] = xfull_ref[j, CONV_PAD + tv_last - (CONV_WIDTH - 1):CONV_PAD + tv_last, :]
            sn_ref[j] = _dot3_left(s_ref[j], fold_ref[...])

    for j in seqs:
        xfull_ref[j, 0:CONV_PAD, :] = xfull_ref[j, L:L + CONV_PAD, :]


def _gdn_call(u, n_seq, rows_per_seq, t_valid, conv0, s0, w_conv, a_log, dt_bias, g_norm):
    L = GDN_CHUNK
    G = GDN_SEQS_PER_STEP
    n_chunks = -(-t_valid // L)
    tv_last = t_valid - (n_chunks - 1) * L
    lane_row = lambda vals, off: jnp.zeros((1, LANE), F32).at[0, off:off + vals.shape[0]].set(vals)
    conv0_p = jnp.pad(conv0, ((0, 0), (CONV_PAD - (CONV_WIDTH - 1), 0), (0, 0)))
    u3 = u.reshape(n_seq, rows_per_seq, U_WIDTH)
    row = lambda w, col: pl.BlockSpec((G, L, w), lambda b, c: (b, c, col))
    full = lambda r, cc: pl.BlockSpec((r, cc), lambda b, c: (0, 0))
    per_seq = lambda r, cc: pl.BlockSpec((G, r, cc), lambda b, c: (b, 0, 0))
    out = pl.pallas_call(
        functools.partial(_gdn_body, n_chunks=n_chunks, tv_last=tv_last),
        grid=(n_seq // G, n_chunks),
        in_specs=[row(GDN_CONV_DIM, OFF_QKVB // GDN_CONV_DIM), row(GDN_WIDTH, OFF_ZB // GDN_WIDTH),
                  row(LANE, OFF_SMALL // LANE),
                  full(CONV_WIDTH, GDN_CONV_DIM), full(1, LANE), full(1, LANE), full(1, GDN_WIDTH),
                  per_seq(CONV_PAD, GDN_CONV_DIM), per_seq(GDN_WIDTH, HEAD_DIM),
                  full(L, L), full(LANE, GDN_WIDTH), full(LANE, GDN_WIDTH), full(GDN_WIDTH, GDN_WIDTH),
                  full(GDN_WIDTH, HEAD_DIM), full(HEAD_DIM, GDN_WIDTH)],
        out_specs=[pl.BlockSpec((G, L, GDN_WIDTH), lambda b, c: (b, c, 0)),
                   per_seq(CONV_WIDTH - 1, GDN_CONV_DIM), per_seq(GDN_WIDTH, HEAD_DIM)],
        out_shape=[jax.ShapeDtypeStruct((n_seq, n_chunks * L, GDN_WIDTH), F32),
                   jax.ShapeDtypeStruct((n_seq, CONV_WIDTH - 1, GDN_CONV_DIM), F32),
                   jax.ShapeDtypeStruct((n_seq, GDN_WIDTH, HEAD_DIM), F32)],
        scratch_shapes=[pltpu.VMEM((G, L + CONV_PAD, GDN_CONV_DIM), F32),
                        pltpu.VMEM((G, GDN_WIDTH, GDN_WIDTH), F32)],
        compiler_params=_params(("parallel", "arbitrary")), name="gdn",
    )(u3, u3, u3, w_conv, lane_row(a_log, SMALL_DECAY), lane_row(dt_bias, SMALL_DECAY),
      jnp.tile(g_norm, GDN_HEADS).reshape(1, GDN_WIDTH), conv0_p, s0.reshape(n_seq, GDN_WIDTH, HEAD_DIM),
      _tril_incl(L), _expand_matrix(SMALL_BETA, GDN_HEADS, HEAD_DIM), _expand_matrix(SMALL_DECAY, GDN_HEADS, HEAD_DIM),
      _block_diag_ones(GDN_WIDTH, HEAD_DIM), _fold_matrix(GDN_HEADS, HEAD_DIM), _fold_matrix(GDN_HEADS, HEAD_DIM).T)
    o, conv_new, s_new = out
    return o.reshape(n_seq * n_chunks * L, GDN_WIDTH), conv_new, s_new.reshape(n_seq, GDN_HEADS, GDN_DK, HEAD_DIM)


def _ssd_body(xbc_ref, z_ref, small_ref, wconv_ref, bconv_ref, alog_ref, dtb_ref, dskip_ref, gnorm_ref,
              conv0_ref, h0_ref, tril_ref, expc_ref,
              o_ref, convn_ref, hn_ref, xfull_ref, ht_ref, *, n_chunks, tv_last):
    L = SSD_CHUNK
    W = SSD_WIDTH
    GW = W // SSD_GROUPS
    c = pl.program_id(1)

    @pl.when(c == 0)
    def _():
        xfull_ref[0:CONV_PAD, :] = conv0_ref[...]
        for j in range(W // LANE):
            ht_ref[:, j * LANE:(j + 1) * LANE] = h0_ref[j * LANE:(j + 1) * LANE, :].T

    xs = _silu(_chunk_conv(xbc_ref, xfull_ref, wconv_ref, L) + bconv_ref[...])
    x_c, bm, cm = xs[:, 0:W], xs[:, W:W + GW], xs[:, W + GW:W + 2 * GW]
    small = small_ref[...]
    lane = lax.broadcasted_iota(jnp.int32, (L, LANE), 1)
    dt_s = jnp.where((lane >= SMALL_DT) & (lane < SMALL_DT + SSD_HEADS), _softplus(small + dtb_ref[...]), 0.0)
    if tv_last < L:
        last_rows = jnp.where(c == n_chunks - 1, tv_last, L)
        keep_s = lax.broadcasted_iota(jnp.int32, (L, LANE), 0) < last_rows
        keep_g = lax.broadcasted_iota(jnp.int32, (L, GW), 0) < last_rows
        dt_s = jnp.where(keep_s, dt_s, 0.0)
        bm, cm = jnp.where(keep_g, bm, 0.0), jnp.where(keep_g, cm, 0.0)
    ac_s = dt_s * (-jnp.exp(alog_ref[...]))
    acs_s = _dot3_right(tril_ref[...], ac_s)
    acs_e = _dot3_left(acs_s, expc_ref[...])
    dt_e = _dot3_left(dt_s, expc_ref[...])
    xdt = x_c * dt_e
    e = jnp.exp(acs_e)
    last = acs_e[L - 1:L, :]
    xw = xdt * jnp.exp(last - acs_e)
    dchunk = jnp.exp(last)
    acs_t = acs_s.T
    incl = lax.broadcasted_iota(jnp.int32, (L, L), 0) >= lax.broadcasted_iota(jnp.int32, (L, L), 1)
    first_half = lax.broadcasted_iota(jnp.int32, (L, LANE), 1) < HEAD_DIM
    ht = ht_ref[...]
    ys = []
    for g in range(SSD_GROUPS):
        gl = slice(g * GW, (g + 1) * GW)
        bg = bm[:, g * SSD_STATE:(g + 1) * SSD_STATE].astype(BF16)
        cg = cm[:, g * SSD_STATE:(g + 1) * SSD_STATE].astype(BF16)
        scores = _dot_nt(cg, bg)
        y_off = _dot(cg, ht[:, gl].astype(BF16)) * e[:, gl]
        ht_ref[:, gl] = ht[:, gl] * dchunk[:, gl] + _dot_tn(bg, xw[:, gl].astype(BF16))
        pairs = []
        for pp in range(GW // LANE):
            xp = xdt[:, g * GW + pp * LANE:g * GW + (pp + 1) * LANE]
            yp = jnp.zeros((L, LANE), F32)
            for hh in range(2):
                h = g * (SSD_HEADS // SSD_GROUPS) + pp * 2 + hh
                col = acs_s[:, SMALL_DT + h:SMALL_DT + h + 1]
                rowv = acs_t[SMALL_DT + h:SMALL_DT + h + 1, :]
                decay = jnp.exp(jnp.where(incl, col - rowv, NEG_BIG))
                xm = jnp.where(first_half if hh == 0 else jnp.logical_not(first_half), xp, 0.0)
                yp = yp + _dot((scores * decay).astype(BF16), xm.astype(BF16))
            pairs.append(yp)
        ys.append(jnp.concatenate(pairs, axis=1) + y_off)
    y = (jnp.concatenate(ys, axis=1) + dskip_ref[...] * x_c) * _silu(z_ref[...])
    outs = []
    for g in range(SSD_GROUPS):
        yg = y[:, g * GW:(g + 1) * GW]
        outs.append(yg * lax.rsqrt(jnp.mean(yg * yg, axis=-1, keepdims=True) + RMS_EPS))
    o_ref[...] = jnp.concatenate(outs, axis=1) * gnorm_ref[...]

    @pl.when(c == n_chunks - 1)
    def _():
        convn_ref[...] = xfull_ref[CONV_PAD + tv_last - (CONV_WIDTH - 1):CONV_PAD + tv_last, :]
        for j in range(W // LANE):
            hn_ref[j * LANE:(j + 1) * LANE, :] = ht_ref[:, j * LANE:(j + 1) * LANE].T

    xfull_ref[0:CONV_PAD, :] = xfull_ref[L:L + CONV_PAD, :]


def _ssd_call(u, n_seq, rows_per_seq, t_valid, conv0, h0, w_conv, b_conv, a_log, dt_bias, d_skip, g_norm):
    L = SSD_CHUNK
    n_chunks = -(-t_valid // L)
    tv_last = t_valid - (n_chunks - 1) * L
    rb = rows_per_seq // L
    lane_row = lambda vals, off: jnp.zeros((1, LANE), F32).at[0, off:off + vals.shape[0]].set(vals)
    conv0_p = jnp.pad(conv0, ((0, 0), (CONV_PAD - (CONV_WIDTH - 1), 0), (0, 0)))
    row = lambda w, col: pl.BlockSpec((L, w), lambda b, c: (b * rb + c, col))
    full = lambda r, cc: pl.BlockSpec((r, cc), lambda b, c: (0, 0))
    per_seq = lambda r, cc: pl.BlockSpec((None, r, cc), lambda b, c: (b, 0, 0))
    out = pl.pallas_call(
        functools.partial(_ssd_body, n_chunks=n_chunks, tv_last=tv_last),
        grid=(n_seq, n_chunks),
        in_specs=[row(SSD_XBC, OFF_XBC // SSD_XBC), row(SSD_WIDTH, OFF_ZC // SSD_WIDTH), row(LANE, OFF_SMALL // LANE),
                  full(CONV_WIDTH, SSD_XBC), full(1, SSD_XBC), full(1, LANE), full(1, LANE),
                  full(1, SSD_WIDTH), full(1, SSD_WIDTH),
                  per_seq(CONV_PAD, SSD_XBC), per_seq(SSD_WIDTH, SSD_STATE),
                  full(L, L), full(LANE, SSD_WIDTH)],
        out_specs=[pl.BlockSpec((L, SSD_WIDTH), lambda b, c: (b * n_chunks + c, 0)),
                   per_seq(CONV_WIDTH - 1, SSD_XBC), per_seq(SSD_WIDTH, SSD_STATE)],
        out_shape=[jax.ShapeDtypeStruct((n_seq * n_chunks * L, SSD_WIDTH), F32),
                   jax.ShapeDtypeStruct((n_seq, CONV_WIDTH - 1, SSD_XBC), F32),
                   jax.ShapeDtypeStruct((n_seq, SSD_WIDTH, SSD_STATE), F32)],
        scratch_shapes=[pltpu.VMEM((L + CONV_PAD, SSD_XBC), F32), pltpu.VMEM((SSD_STATE, SSD_WIDTH), F32)],
        compiler_params=_params(("parallel", "arbitrary")), name="ssd",
    )(u, u, u, w_conv, b_conv.reshape(1, SSD_XBC), lane_row(a_log, SMALL_DT), lane_row(dt_bias, SMALL_DT),
      jnp.repeat(d_skip, HEAD_DIM).reshape(1, SSD_WIDTH), g_norm.reshape(1, SSD_WIDTH),
      conv0_p, h0.reshape(n_seq, SSD_WIDTH, SSD_STATE), _tril_incl(L), _expand_matrix(SMALL_DT, SSD_HEADS, HEAD_DIM))
    o, conv_new, h_new = out
    return o, conv_new, h_new.reshape(n_seq, SSD_HEADS, HEAD_DIM, SSD_STATE)


ROUTE_GATE, ROUTE_IDX, ROUTE_RANK = 0, TOP_K, 2 * TOP_K


def _strict_lower(n):
    return jnp.asarray(np.tril(np.ones((n, n), bool), -1), BF16)


def _proj_out_body(x_ref, oa_ref, ob_ref, oc_ref, w_ref, g_ref, b_ref, rw_ref, rb_ref, lower_ref,
                   x1_ref, x1b_ref, route_ref, count_ref):
    mix = _dot(oa_ref[...].astype(BF16), w_ref[0:256, :])
    mix += _dot(ob_ref[...].astype(BF16), w_ref[256:512, :])
    mix += _dot(oc_ref[...].astype(BF16), w_ref[512:1024, :])
    x1 = _layer_norm_f32(DEEPNORM_ALPHA * x_ref[...] + mix, g_ref[...], b_ref[...])
    x1_ref[...] = x1
    x1b_ref[...] = x1.astype(BF16)
    logits = _mm3(x1, rw_ref[...]) + rb_ref[...]

    @pl.when(pl.program_id(0) == 0)
    def _():
        count_ref[...] = jnp.zeros_like(count_ref)

    lane = lax.broadcasted_iota(jnp.int32, logits.shape, 1)
    lg = jnp.where(lane < N_EXPERTS, logits, -jnp.inf)
    vals, idxs = [], []
    for _ in range(TOP_K):
        m = jnp.max(lg, axis=1, keepdims=True)
        idx = jnp.min(jnp.where(lg == m, lane, LANE), axis=1, keepdims=True)
        vals.append(m)
        idxs.append(idx)
        lg = jnp.where(lane == idx, -jnp.inf, lg)
    exps = [jnp.exp(v - vals[0]) for v in vals]
    denom = exps[0] + exps[1] + exps[2] + exps[3]
    hot = [lane == idx for idx in idxs]
    multi = jnp.where(hot[0] | hot[1] | hot[2] | hot[3], 1.0, 0.0)
    before = _dot(lower_ref[...], multi.astype(BF16)) + count_ref[...]
    route = jnp.zeros(logits.shape, F32)
    for k in range(TOP_K):
        rank = jnp.sum(jnp.where(hot[k], before, 0.0), axis=1, keepdims=True)
        route = jnp.where(lane == ROUTE_GATE + k, exps[k] / denom, route)
        route = jnp.where(lane == ROUTE_IDX + k, idxs[k].astype(F32), route)
        route = jnp.where(lane == ROUTE_RANK + k, rank, route)
    route_ref[...] = route
    count_ref[...] += jnp.sum(multi, axis=0, keepdims=True)


def _proj_out_call(x, o_a, o_b, o_c, w_bf16, g, b, router_w, router_b):
    n, d = x.shape
    bm = _row_block(n)
    rw = jnp.pad(router_w, ((0, 0), (0, LANE - N_EXPERTS)))
    rb = jnp.pad(router_b, (0, LANE - N_EXPERTS)).reshape(1, LANE)
    row = lambda w: pl.BlockSpec((bm, w), lambda i: (i, 0))
    full = lambda r, c: pl.BlockSpec((r, c), lambda i: (0, 0))
    return pl.pallas_call(
        _proj_out_body, grid=(n // bm,),
        in_specs=[row(d), row(SB_WIDTH), row(GDN_WIDTH), row(SSD_WIDTH), full(d, d), full(1, d), full(1, d),
                  full(d, LANE), full(1, LANE), full(bm, bm)],
        out_specs=[row(d), row(d), row(LANE), full(1, LANE)],
        out_shape=[jax.ShapeDtypeStruct((n, d), F32), jax.ShapeDtypeStruct((n, d), BF16),
                   jax.ShapeDtypeStruct((n, LANE), F32), jax.ShapeDtypeStruct((1, LANE), F32)],
        compiler_params=_params(("arbitrary",)), name="proj_out",
    )(x, o_a, o_b, o_c, w_bf16, g.reshape(1, d), b.reshape(1, d), rw, rb, _strict_lower(bm))


def _moe_body(be_ref, nb_ref, rows_ref, wg_ref, bg_ref, wu_ref, bu_ref, wd_ref, bd_ref, o_ref, wg_s, wu_s, wd_s):
    i = pl.program_id(0)
    prev = be_ref[jnp.maximum(i - 1, 0)]

    @pl.when(jnp.logical_or(i == 0, be_ref[i] != prev))
    def _():
        wg_s[...] = wg_ref[...].astype(BF16)
        wu_s[...] = wu_ref[...].astype(BF16)
        wd_s[...] = wd_ref[...].astype(BF16)

    @pl.when(i < nb_ref[0])
    def _():
        xb = rows_ref[...]
        glu = jnp.minimum(_dot(xb, wg_s[...]) + bg_ref[...], SWIGLU_LIMIT)
        lin = jnp.clip(_dot(xb, wu_s[...]) + bu_ref[...], -SWIGLU_LIMIT, SWIGLU_LIMIT)
        act = glu * jax.nn.sigmoid(SWIGLU_ALPHA * glu) * (lin + 1.0)
        o_ref[...] = _dot(act.astype(BF16), wd_s[...]) + bd_ref[...]

    @pl.when(i >= nb_ref[0])
    def _():
        o_ref[...] = jnp.zeros_like(o_ref)


def _moe_call(rows_in, block_expert, n_used, w_gate, b_gate, w_up, b_up, w_down, b_down, layer, blk):
    n_rows, d = rows_in.shape
    n_blocks = n_rows // blk
    wspec = lambda r, c: pl.BlockSpec((None, None, r, c), lambda i, be, nb: (layer, be[i], 0, 0))
    grid_spec = pltpu.PrefetchScalarGridSpec(
        num_scalar_prefetch=2, grid=(n_blocks,),
        in_specs=[pl.BlockSpec((blk, d), lambda i, be, nb: (i, 0)),
                  wspec(d, D_EXPERT), wspec(1, D_EXPERT), wspec(d, D_EXPERT), wspec(1, D_EXPERT),
                  wspec(D_EXPERT, d), wspec(1, d)],
        out_specs=pl.BlockSpec((blk, d), lambda i, be, nb: (i, 0)),
        scratch_shapes=[pltpu.VMEM((d, D_EXPERT), BF16), pltpu.VMEM((d, D_EXPERT), BF16),
                        pltpu.VMEM((D_EXPERT, d), BF16)])
    depth = w_gate.shape[0]
    return pl.pallas_call(
        _moe_body, grid_spec=grid_spec,
        out_shape=jax.ShapeDtypeStruct((n_rows, d), F32),
        compiler_params=_params(("arbitrary",)), name="moe_experts",
    )(block_expert, n_used, rows_in, w_gate, b_gate.reshape(depth, N_EXPERTS, 1, D_EXPERT),
      w_up, b_up.reshape(depth, N_EXPERTS, 1, D_EXPERT), w_down, b_down.reshape(depth, N_EXPERTS, 1, d))


def _route(route, counts, blk):
    n_tok = route.shape[0]
    gate = route[:, ROUTE_GATE:ROUTE_GATE + TOP_K]
    expert = route[:, ROUTE_IDX:ROUTE_IDX + TOP_K].astype(jnp.int32)
    rank = route[:, ROUTE_RANK:ROUTE_RANK + TOP_K].astype(jnp.int32)
    counts = counts[0, :N_EXPERTS].astype(jnp.int32)
    n_assign = n_tok * TOP_K
    n_blocks = -(-(n_assign + N_EXPERTS * (blk - 1)) // blk)
    padded = (counts + blk - 1) // blk * blk
    padded_end = jnp.cumsum(padded)
    padded_start = padded_end - padded
    start = jnp.cumsum(counts) - counts
    dest = padded_start[expert] + rank
    token = jnp.broadcast_to(jnp.arange(n_tok, dtype=jnp.int32)[:, None], (n_tok, TOP_K))
    _, token_sorted = lax.sort((dest.reshape(-1), token.reshape(-1)), num_keys=1)
    block_first = jnp.arange(n_blocks, dtype=jnp.int32) * blk
    block_expert = jnp.minimum(jnp.sum(padded_end[None, :] <= block_first[:, None], axis=1, dtype=jnp.int32),
                               N_EXPERTS - 1)
    r = jnp.arange(n_blocks * blk, dtype=jnp.int32)
    e_r = jnp.repeat(block_expert, blk)
    src = token_sorted[jnp.clip(start[e_r] + r - padded_start[e_r], 0, n_assign - 1)]
    dest_t = dest.T.reshape(-1)
    n_used = (padded_end[-1] // blk).astype(jnp.int32).reshape(1)
    return gate, src, dest_t, block_expert, n_used


def _post_body(x1_ref, r0_ref, r1_ref, r2_ref, r3_ref, gate_ref, p_ref, g_ref, b_ref, wg_ref, bg_ref, wp_ref, o_ref):
    gate = gate_ref[...]
    ffn = gate[:, 0:1] * r0_ref[...]
    for k, r_ref in ((1, r1_ref), (2, r2_ref), (3, r3_ref)):
        ffn += gate[:, k:k + 1] * r_ref[...]
    x2 = _layer_norm_f32(DEEPNORM_ALPHA * x1_ref[...] + ffn, g_ref[...], b_ref[...])
    gate_logit = _dot(x2.astype(BF16), wg_ref[...]) + bg_ref[...]
    emb = _dot(p_ref[...].astype(BF16), wp_ref[...])
    o_ref[...] = x2 + jax.nn.sigmoid(gate_logit) * emb


def _post_call(x1, rows_k, gate, p, g, b, wg_bf16, bg, wp_bf16):
    n, d = x1.shape
    bm = _row_block(n)
    nb = n // bm
    row = lambda w: pl.BlockSpec((bm, w), lambda i: (i, 0))
    plane = lambda k: pl.BlockSpec((bm, d), lambda i: (k * nb + i, 0))
    full = lambda r, c: pl.BlockSpec((r, c), lambda i: (0, 0))
    return pl.pallas_call(
        _post_body, grid=(nb,),
        in_specs=[row(d)] + [plane(k) for k in range(TOP_K)] + [row(TOP_K), row(PLE_DIM), full(1, d), full(1, d),
                                                                 full(d, d), full(1, d), full(PLE_DIM, d)],
        out_specs=row(d),
        out_shape=jax.ShapeDtypeStruct((n, d), F32),
        compiler_params=_params(("parallel",)), name="post_ffn",
    )(x1, rows_k, rows_k, rows_k, rows_k, gate, p, g.reshape(1, d), b.reshape(1, d), wg_bf16, bg.reshape(1, d),
      wp_bf16)


def kernel(x_prompt, x_sample, cache_k_a, cache_v_a, state_conv_b, state_delta_b, state_conv_c, state_ssm_c, page_table, p_prompt, p_sample, ln_in_g, ln_in_b, w_in, sb_bias, w_conv_b, a_log_b, dt_bias_b, g_norm_b, w_conv_c, b_conv_c, a_log_c, dt_bias_c, d_skip_c, g_norm_c, w_out, ln1_g, ln1_b, router_w, router_b, w_gate, b_gate, w_up, b_up, w_down, b_down, ln2_g, ln2_b, ple_gate_w, ple_gate_b, ple_proj_w):
    depth = w_in.shape[0]
    w_in_r = jnp.pad(w_in[:, :, _IN_PERM], ((0, 0), (0, 0), (0, U_WIDTH - _IN_PERM.size))).astype(BF16)
    w_out_b = w_out.astype(BF16)
    ple_gate_w_b = ple_gate_w.astype(BF16)
    ple_proj_w_b = ple_proj_w.astype(BF16)
    n_pool = cache_k_a.shape[1]
    cache_k = cache_k_a.transpose(0, 1, 3, 4, 2).reshape(depth, n_pool, SB_WIDTH, PAGE_SIZE)
    cache_v = cache_v_a.transpose(0, 1, 3, 4, 2).reshape(depth, n_pool, SB_WIDTH, PAGE_SIZE)

    def run_group(x, p, sample):
        bsz, t, d = x.shape
        n = bsz * t
        x = _ln_call(x.reshape(n, d), ln_in_g, ln_in_b)
        new = []
        for i in range(depth):
            u = _proj_in_call(x, w_in_r[i])
            k_a = u[:, OFF_K:OFF_K + SB_WIDTH].reshape(bsz, t, SB_HEADS, HEAD_DIM)
            v_a = u[:, OFF_V:OFF_V + SB_WIDTH].reshape(bsz, t, SB_HEADS, HEAD_DIM)
            if sample:
                o_a = _sb_decode_call(u[:, OFF_Q:OFF_Q + SB_WIDTH], sb_bias[i], cache_k, cache_v, page_table, i)
                conv_b, delta_b, conv_c, ssm_c = state_conv_b[i], state_delta_b[i], state_conv_c[i], state_ssm_c[i]
                rows_per_seq = SSD_CHUNK
                u_mix = jnp.pad(u.reshape(bsz, t, U_WIDTH), ((0, 0), (0, rows_per_seq - t), (0, 0)))
                u_mix = u_mix.reshape(bsz * rows_per_seq, U_WIDTH)
            else:
                o_a = _sb_prefill_call(u, sb_bias[i], bsz, t)
                conv_b = jnp.zeros((bsz, CONV_WIDTH - 1, GDN_CONV_DIM), F32)
                delta_b = jnp.zeros((bsz, GDN_HEADS, GDN_DK, HEAD_DIM), F32)
                conv_c = jnp.zeros((bsz, CONV_WIDTH - 1, SSD_XBC), F32)
                ssm_c = jnp.zeros((bsz, SSD_HEADS, HEAD_DIM, SSD_STATE), F32)
                rows_per_seq = t
                u_mix = u
            o_b, conv_b_new, delta_new = _gdn_call(u_mix, bsz, rows_per_seq, t, conv_b, delta_b, w_conv_b[i],
                                                   a_log_b[i], dt_bias_b[i], g_norm_b[i])
            o_c, conv_c_new, ssm_new = _ssd_call(u_mix, bsz, rows_per_seq, t, conv_c, ssm_c, w_conv_c[i], b_conv_c[i],
                                                 a_log_c[i], dt_bias_c[i], d_skip_c[i], g_norm_c[i])
            if sample:
                o_b = o_b.reshape(bsz, -1, GDN_WIDTH)[:, :t].reshape(n, GDN_WIDTH)
                o_c = o_c.reshape(bsz, -1, SSD_WIDTH)[:, :t].reshape(n, SSD_WIDTH)
            x1, x1b, route, counts = _proj_out_call(x, o_a, o_b, o_c, w_out_b[i], ln1_g[i], ln1_b[i], router_w[i],
                                                    router_b[i])
            blk = MOE_BLOCK_PROMPT if n * TOP_K >= MOE_BLOCK_PROMPT * N_EXPERTS else MOE_BLOCK_SMALL
            gate, src, dest_t, block_expert, n_used = _route(route, counts, blk)
            rows_out = _moe_call(x1b[src], block_expert, n_used, w_gate, b_gate, w_up, b_up, w_down, b_down, i, blk)
            x = _post_call(x1, rows_out[dest_t], gate, p[i].reshape(n, PLE_DIM), ln2_g[i], ln2_b[i], ple_gate_w_b[i],
                           ple_gate_b[i], ple_proj_w_b[i])
            new.append((k_a, v_a, conv_b_new, delta_new, conv_c_new, ssm_new))
        return x.reshape(bsz, t, d), [jnp.stack([s[j] for s in new]) for j in range(6)]

    y_prompt, new_prompt = run_group(x_prompt, p_prompt, False)
    y_sample, new_sample = run_group(x_sample, p_sample, True)
    return (y_prompt, y_sample, *new_prompt, *new_sample)
```

```python
import functools

import jax
import jax.numpy as jnp
import numpy as np
from jax import lax
from jax.experimental import pallas as pl
from jax.experimental.pallas import tpu as pltpu

F32 = jnp.float32
BF16 = jnp.bfloat16

D_MODEL = 1024
DEPTH = 4
PAGE_SIZE = 128
HEAD_DIM = 64
CONV_WIDTH = 4
SB_WIDTH = 256
SB_HEADS = 4
GDN_WIDTH = 256
GDN_HEADS = 4
GDN_DK = 64
GDN_CONV_DIM = 768
GDN_CHUNK = 64
SSD_WIDTH = 512
SSD_HEADS = 8
SSD_GROUPS = 2
SSD_STATE = 128
SSD_XBC = 1024
SSD_CHUNK = 128
N_EXPERTS = 32
TOP_K = 4
D_EXPERT = 512
SWIGLU_ALPHA = 1.702
SWIGLU_LIMIT = 7.0
PLE_DIM = 256
DEEPNORM_ALPHA = (2 * DEPTH) ** 0.25
LN_EPS = 1e-5
RMS_EPS = 1e-6

_REF_WIDTHS = (256, 256, 256, 768, 256, 4, 4, 512, 1024, 8)
_REF_OFF = tuple(int(v) for v in np.cumsum((0,) + _REF_WIDTHS))
OFF_XBC, OFF_ZC, OFF_QKVB, OFF_Q, OFF_K, OFF_V, OFF_ZB, OFF_SMALL = 0, 1024, 1536, 2304, 2560, 2816, 3072, 3328
U_WIDTH = 3456
LANE = 128
SMALL_BETA, SMALL_DECAY, SMALL_DT = 0, GDN_HEADS, 2 * GDN_HEADS
_IN_PERM = np.concatenate([
    np.arange(_REF_OFF[8], _REF_OFF[9]),
    np.arange(_REF_OFF[7], _REF_OFF[8]),
    np.arange(_REF_OFF[3], _REF_OFF[4]),
    np.arange(_REF_OFF[0], _REF_OFF[3]),
    np.arange(_REF_OFF[4], _REF_OFF[5]),
    np.arange(_REF_OFF[5], _REF_OFF[7]),
    np.arange(_REF_OFF[9], _REF_OFF[10]),
])

VMEM_LIMIT = 48 * 1024 * 1024
MOE_BLOCK_PROMPT = 512
MOE_BLOCK_SMALL = 8
SB_BLOCK = 256
DEC_PAGES_PER_STEP = 16
CONV_PAD = 8
NEG_BIG = -1e30
LOG2E = 1.4426950408889634


def _params(sem):
    return pltpu.CompilerParams(dimension_semantics=sem, vmem_limit_bytes=VMEM_LIMIT)


def _row_block(n):
    return n if n < 512 else 512


def _layer_norm_f32(x, g, b):
    mu = jnp.mean(x, axis=-1, keepdims=True)
    xc = x - mu
    var = jnp.mean(xc * xc, axis=-1, keepdims=True)
    return xc * lax.rsqrt(var + LN_EPS) * g + b


def _softplus(z):
    return jnp.maximum(z, 0.0) + jnp.log1p(jnp.exp(-jnp.abs(z)))


def _softplus2(z2):
    return jnp.maximum(z2, 0.0) + jnp.log2(1.0 + jnp.exp2(-jnp.abs(z2)))


def _silu(x):
    return x * jax.nn.sigmoid(x)


def _dot(a, b):
    return jnp.dot(a, b, preferred_element_type=F32)


def _dot_nt(a, b):
    return lax.dot_general(a, b, (((1,), (1,)), ((), ())), preferred_element_type=F32)


def _dot_tn(a, b):
    return lax.dot_general(a, b, (((0,), (0,)), ((), ())), preferred_element_type=F32)


def _split2(x):
    hi = x.astype(BF16)
    return hi, (x - hi.astype(F32)).astype(BF16)


def _split3(x):
    hi = x.astype(BF16)
    r = x - hi.astype(F32)
    mid = r.astype(BF16)
    return hi, mid, (r - mid.astype(F32)).astype(BF16)


def _split_dot(x, w_bf16):
    hi, lo = _split2(x)
    return _dot(hi, w_bf16) + _dot(lo, w_bf16)


def _dot3_left(x, w_bf16):
    hi, mid, lo = _split3(x)
    return _dot(hi, w_bf16) + (_dot(mid, w_bf16) + _dot(lo, w_bf16))


def _dot3_right(w_bf16, x):
    hi, mid, lo = _split3(x)
    return _dot(w_bf16, hi) + (_dot(w_bf16, mid) + _dot(w_bf16, lo))


def _mm3(a, b):
    ah, al = _split2(a)
    bh, bl = _split2(b)
    return _dot(ah, bh) + (_dot(ah, bl) + _dot(al, bh))


def _cumsum_matrix(n):
    j = np.arange(n)[:, None]
    s = np.arange(n)[None, :]
    return jnp.asarray(np.concatenate([(j > s), np.ones((n, LANE), bool)], axis=1), BF16)


def _tril_incl(n):
    return jnp.asarray(np.tril(np.ones((n, n), bool)), BF16)


def _expand_matrix(first_lane, heads, width):
    m = np.zeros((LANE, heads * width), bool)
    for h in range(heads):
        m[first_lane + h, h * width:(h + 1) * width] = True
    return jnp.asarray(m, BF16)


def _block_diag_ones(n, blk):
    i = np.arange(n)
    return jnp.asarray((i[:, None] // blk) == (i[None, :] // blk), BF16)


def _fold_matrix(heads, width):
    i = np.arange(heads * width)
    return jnp.asarray((i[:, None] % width) == np.arange(width)[None, :], BF16)


def _ln_body(x_ref, g_ref, b_ref, o_ref):
    o_ref[...] = _layer_norm_f32(x_ref[...], g_ref[...], b_ref[...])


def _ln_call(x, g, b):
    n, d = x.shape
    bm = _row_block(n)
    return pl.pallas_call(
        _ln_body, grid=(n // bm,),
        in_specs=[pl.BlockSpec((bm, d), lambda i: (i, 0)),
                  pl.BlockSpec((1, d), lambda i: (0, 0)), pl.BlockSpec((1, d), lambda i: (0, 0))],
        out_specs=pl.BlockSpec((bm, d), lambda i: (i, 0)),
        out_shape=jax.ShapeDtypeStruct((n, d), F32),
        compiler_params=_params(("parallel",)), name="ln_in",
    )(x, g.reshape(1, d), b.reshape(1, d))


_PROJ_CHUNK = 384


def _proj_in_body(x_ref, w_ref, o_ref):
    xb = x_ref[...].astype(BF16)
    for j in range(0, U_WIDTH, _PROJ_CHUNK):
        o_ref[:, j:j + _PROJ_CHUNK] = _dot(xb, w_ref[:, j:j + _PROJ_CHUNK])


def _proj_in_call(x, w_bf16):
    n, d = x.shape
    bm = _row_block(n)
    return pl.pallas_call(
        _proj_in_body, grid=(n // bm,),
        in_specs=[pl.BlockSpec((bm, d), lambda i: (i, 0)),
                  pl.BlockSpec((d, U_WIDTH), lambda i: (0, 0))],
        out_specs=pl.BlockSpec((bm, U_WIDTH), lambda i: (i, 0)),
        out_shape=jax.ShapeDtypeStruct((n, U_WIDTH), F32),
        compiler_params=_params(("parallel",)), name="proj_in",
    )(x, w_bf16)


def _sb_prefill_body(bias_ref, q_ref, k_ref, v_ref, u_ref, o_ref, qh_ref, acc_ref, carry_ref):
    blk = q_ref.shape[0]
    i = pl.program_id(1)
    q = q_ref[...] * (HEAD_DIM ** -0.5 * LOG2E)
    lane_head = lax.broadcasted_iota(jnp.int32, (1, SB_WIDTH), 1) // HEAD_DIM
    for h in range(SB_HEADS):
        qh_ref[h] = jnp.where(lane_head == h, q, 0.0).astype(BF16)
    acc_ref[...] = jnp.zeros_like(acc_ref)
    carry_ref[...] = jnp.zeros_like(carry_ref)
    rows = lax.broadcasted_iota(jnp.int32, (blk, blk), 0)
    cols = lax.broadcasted_iota(jnp.int32, (blk, blk), 1)

    def key_block(j, diagonal):
        start = pl.multiple_of(j * blk, blk)
        kj = k_ref[pl.ds(start, blk), :].astype(BF16)
        vj = v_ref[pl.ds(start, blk), :].astype(BF16)
        heads = range(SB_HEADS)
        visible = cols < rows
        zs = [_dot_nt(qh_ref[h], kj) + bias_ref[h] for h in heads]
        log_keeps = [-_softplus2(z) for z in zs]
        logits = [z + lk for z, lk in zip(zs, log_keeps)]
        if diagonal:
            log_keeps = [jnp.where(visible, lk, 0.0) for lk in log_keeps]
        lts = [_split_dot(lk, u_ref[...]) for lk in log_keeps]
        carries = [carry_ref[h] for h in heads]
        ws = [jnp.exp2(logits[h] + (lts[h][:, :blk] + jnp.concatenate([carries[h]] * (blk // LANE), axis=1)))
              for h in heads]
        if diagonal:
            ws = [jnp.where(visible, w, 0.0) for w in ws]
        for h in heads:
            acc_ref[h] += _dot(ws[h].astype(BF16), vj)
            carry_ref[h] = carries[h] + lts[h][:, blk:]

    key_block(i, True)

    def earlier(t, carry):
        key_block(i - 1 - t, False)
        return carry

    lax.fori_loop(0, i, earlier, 0)
    out = jnp.zeros((blk, SB_WIDTH), F32)
    for h in range(SB_HEADS):
        out = out + jnp.where(lane_head == h, acc_ref[h], 0.0)
    o_ref[...] = out


def _sb_prefill_call(u, bias, bsz, t):
    blk = SB_BLOCK
    nq = t // blk
    grid_spec = pltpu.PrefetchScalarGridSpec(
        num_scalar_prefetch=0, grid=(bsz, nq),
        in_specs=[pl.BlockSpec(memory_space=pltpu.SMEM),
                  pl.BlockSpec((blk, SB_WIDTH), lambda b, i: (b * nq + i, OFF_Q // SB_WIDTH)),
                  pl.BlockSpec((t, SB_WIDTH), lambda b, i: (b, OFF_K // SB_WIDTH)),
                  pl.BlockSpec((t, SB_WIDTH), lambda b, i: (b, OFF_V // SB_WIDTH)),
                  pl.BlockSpec((blk, blk + LANE), lambda b, i: (0, 0))],
        out_specs=pl.BlockSpec((blk, SB_WIDTH), lambda b, i: (b * nq + i, 0)),
        scratch_shapes=[pltpu.VMEM((SB_HEADS, blk, SB_WIDTH), BF16),
                        pltpu.VMEM((SB_HEADS, blk, SB_WIDTH), F32),
                        pltpu.VMEM((SB_HEADS, blk, LANE), F32)])
    return pl.pallas_call(
        _sb_prefill_body, grid_spec=grid_spec,
        out_shape=jax.ShapeDtypeStruct((bsz * t, SB_WIDTH), F32),
        compiler_params=_params(("parallel", "arbitrary")), name="sb_prefill",
    )(bias * LOG2E, u, u, u, _cumsum_matrix(blk))


def _sb_decode_body(pt_ref, q_ref, bias_ref, u_ref, *refs):
    g = DEC_PAGES_PER_STEP
    k_refs, v_refs = refs[:g], refs[g:2 * g]
    o_ref, acc_ref, carry_ref = refs[2 * g], refs[2 * g + 1], refs[2 * g + 2]
    s = pl.program_id(1)

    @pl.when(s == 0)
    def _():
        acc_ref[...] = jnp.zeros_like(acc_ref)
        carry_ref[...] = jnp.zeros_like(carry_ref)

    row = lax.broadcasted_iota(jnp.int32, (8, SB_WIDTH), 0)
    lane_head = lax.broadcasted_iota(jnp.int32, (8, SB_WIDTH), 1) // HEAD_DIM
    own = row == lane_head
    qm = jnp.where(own, q_ref[...] * (HEAD_DIM ** -0.5 * LOG2E), 0.0).astype(BF16)
    acc = acc_ref[...]
    carry = carry_ref[...]
    zs = [_dot(qm, k_refs[c][...].astype(BF16)) + bias_ref[...] for c in range(g)]
    log_keeps = [-_softplus2(z) for z in zs]
    lt_all = _split_dot(jnp.concatenate(log_keeps, axis=0), u_ref[...])
    ws = []
    for c in range(g):
        lt = lt_all[8 * c:8 * c + 8]
        ws.append(jnp.exp2(zs[c] + log_keeps[c] + lt[:, :LANE] + carry).astype(BF16))
        carry = carry + lt[:, LANE:]
    for c in range(g):
        acc = acc + _dot_nt(ws[c], v_refs[c][...].astype(BF16))
    acc_ref[...] = acc
    carry_ref[...] = carry

    @pl.when(s == pl.num_programs(1) - 1)
    def _():
        o_ref[...] = jnp.sum(jnp.where(own, acc, 0.0), axis=0, keepdims=True)


def _sb_decode_call(q, bias, cache_k, cache_v, page_table, layer):
    bsz = q.shape[0]
    n_pages = page_table.shape[1]
    g = DEC_PAGES_PER_STEP
    n_steps = n_pages // g
    bias8 = jnp.zeros((8, LANE), F32).at[:SB_HEADS].set(jnp.broadcast_to(bias[:, None] * LOG2E, (SB_HEADS, LANE)))

    def page_spec(c):
        return pl.BlockSpec((None, None, SB_WIDTH, PAGE_SIZE),
                            lambda b, s, pt: (layer, pt[b, n_pages - 1 - (s * g + c)], 0, 0))

    grid_spec = pltpu.PrefetchScalarGridSpec(
        num_scalar_prefetch=1, grid=(bsz, n_steps),
        in_specs=[pl.BlockSpec((None, 1, SB_WIDTH), lambda b, s, pt: (b, 0, 0)),
                  pl.BlockSpec((8, LANE), lambda b, s, pt: (0, 0)),
                  pl.BlockSpec((PAGE_SIZE, 2 * LANE), lambda b, s, pt: (0, 0))]
                 + [page_spec(c) for c in range(g)] + [page_spec(c) for c in range(g)],
        out_specs=pl.BlockSpec((None, 1, SB_WIDTH), lambda b, s, pt: (b, 0, 0)),
        scratch_shapes=[pltpu.VMEM((8, SB_WIDTH), F32), pltpu.VMEM((8, LANE), F32)])
    out = pl.pallas_call(
        _sb_decode_body, grid_spec=grid_spec,
        out_shape=jax.ShapeDtypeStruct((bsz, 1, SB_WIDTH), F32),
        compiler_params=_params(("parallel", "arbitrary")), name="sb_decode",
    )(page_table, q.reshape(bsz, 1, SB_WIDTH), bias8, _cumsum_matrix(PAGE_SIZE),
      *([cache_k] * g), *([cache_v] * g))
    return out.reshape(bsz, SB_WIDTH)


def _chunk_conv(x_ref, xfull_ref, wconv_ref, rows):
    xfull_ref[CONV_PAD:CONV_PAD + rows, :] = x_ref[...]
    acc = wconv_ref[CONV_WIDTH - 1:CONV_WIDTH, :] * xfull_ref[CONV_PAD:CONV_PAD + rows, :]
    for j in range(1, CONV_WIDTH):
        acc = acc + wconv_ref[CONV_WIDTH - 1 - j:CONV_WIDTH - j, :] * xfull_ref[CONV_PAD - j:CONV_PAD - j + rows, :]
    return acc


GDN_SEQS_PER_STEP = 4


def _mm_split(a_parts, b_parts):
    (ah, al), (bh, bl) = a_parts, b_parts
    return _dot(ah, bh) + (_dot(ah, bl) + _dot(al, bh))


def _same_head_mask():
    r2 = lax.broadcasted_iota(jnp.int32, (GDN_WIDTH, GDN_WIDTH), 0)
    c2 = lax.broadcasted_iota(jnp.int32, (GDN_WIDTH, GDN_WIDTH), 1)
    return r2, c2, (r2 // GDN_CHUNK) == (c2 // GDN_CHUNK)


def _gdn_prepare(c, qkv_ref, small_ref, wconv_ref, alog_ref, dtb_ref,
                 tril_ref, expb_ref, expg_ref, bd_ref, xfull_ref, *, n_chunks, tv_last):
    L = GDN_CHUNK
    W = GDN_WIDTH
    r2, c2, bd = _same_head_mask()
    xs = _silu(_chunk_conv(qkv_ref, xfull_ref, wconv_ref, L))
    q, k, v = xs[:, 0:W], xs[:, W:2 * W], xs[:, 2 * W:3 * W]
    small = small_ref[...]
    lane = lax.broadcasted_iota(jnp.int32, (L, LANE), 1)
    beta_s = jnp.where(lane < SMALL_DECAY, jax.nn.sigmoid(small), 0.0)
    g_s = jnp.where((lane >= SMALL_DECAY) & (lane < SMALL_DT),
                    -jnp.exp(alog_ref[...]) * _softplus(small + dtb_ref[...]), 0.0)
    if tv_last < L:
        last_rows = jnp.where(c == n_chunks - 1, tv_last, L)
        keep_w = lax.broadcasted_iota(jnp.int32, (L, W), 0) < last_rows
        keep_s = lax.broadcasted_iota(jnp.int32, (L, LANE), 0) < last_rows
        q, k, v = jnp.where(keep_w, q, 0.0), jnp.where(keep_w, k, 0.0), jnp.where(keep_w, v, 0.0)
        beta_s, g_s = jnp.where(keep_s, beta_s, 0.0), jnp.where(keep_s, g_s, 0.0)
    bdm = bd_ref[...]
    qn = q * lax.rsqrt(_dot3_left(q * q, bdm) + RMS_EPS) * (GDN_DK ** -0.5)
    kn = k * lax.rsqrt(_dot3_left(k * k, bdm) + RMS_EPS)
    gcs_s = _dot3_right(tril_ref[...], g_s)
    beta_e = _dot3_left(beta_s, expb_ref[...])
    gcs_e = _dot3_left(gcs_s, expg_ref[...])
    eg = jnp.exp(gcs_e)
    kb = kn * beta_e
    tile4 = lambda a: jnp.concatenate([a] * GDN_HEADS, axis=0)
    stack = lambda a: jnp.where(bd, tile4(a), 0.0)
    rl = lax.broadcasted_iota(jnp.int32, (L, W), 0)
    cl = lax.broadcasted_iota(jnp.int32, (L, W), 1)
    grow = jnp.sum(jnp.where((cl % L) == rl, gcs_e, 0.0), axis=0, keepdims=True)
    incl = bd & ((r2 % L) >= (c2 % L))
    decay = jnp.exp(jnp.where(incl, tile4(gcs_e) - grow, NEG_BIG))
    kns = stack(kn).astype(BF16)
    a_mat = jnp.where((r2 % L) != (c2 % L), _dot_nt(stack(kb).astype(BF16), kns) * decay, 0.0)
    qk = _dot_nt(stack(qn).astype(BF16), kns) * decay
    return dict(a=a_mat, qk=qk.astype(BF16), q_dec=(qn * eg).astype(BF16), kn=kn, gcs_e=gcs_e,
                rhs_v=_split2(stack(v * beta_e)), rhs_k=_split2(stack(kb * eg)))


def _gdn_inverses(a_mats):
    r2, c2, _ = _same_head_mask()
    eye = jnp.where(r2 == c2, 1.0, 0.0)
    ps = [eye - a for a in a_mats]
    a_parts = [_split2(a) for a in a_mats]
    x_parts = [_split2(_mm_split(ap, ap)) for ap in a_parts]
    ps = [p + _mm_split(_split2(p), xp) for p, xp in zip(ps, x_parts)]
    for _ in range(4):
        x_parts = [_split2(_mm_split(xp, xp)) for xp in x_parts]
        ps = [p + _mm_split(_split2(p), xp) for p, xp in zip(ps, x_parts)]
    return [_split2(p) for p in ps]


def _gdn_finish(prep, p_parts, z_ref, gnorm_ref, bd_ref, o_ref, s_ref):
    L = GDN_CHUNK
    bd = _same_head_mask()[2]
    stack = lambda a: jnp.where(bd, jnp.concatenate([a] * GDN_HEADS, axis=0), 0.0)
    collapse = lambda m: (m[0:L] + m[L:2 * L]) + (m[2 * L:3 * L] + m[3 * L:4 * L])
    u_c = collapse(_mm_split(p_parts, prep["rhs_v"]))
    w_c = collapse(_mm_split(p_parts, prep["rhs_k"]))
    s = s_ref[...]
    sb = s.astype(BF16)
    v_new = u_c - _dot(w_c.astype(BF16), sb)
    o = _dot(prep["q_dec"], sb) + collapse(_dot(prep["qk"], stack(v_new).astype(BF16)))
    gcs_e = prep["gcs_e"]
    glast = gcs_e[L - 1:L, :]
    k_dec = prep["kn"] * jnp.exp(glast - gcs_e)
    s_ref[...] = s * jnp.exp(glast) + jnp.where(bd, _dot_tn(k_dec.astype(BF16), v_new.astype(BF16)), 0.0)
    on = o * lax.rsqrt(_dot3_left(o * o, bd_ref[...]) * (1.0 / HEAD_DIM) + RMS_EPS) * gnorm_ref[...]
    o_ref[...] = on * _silu(z_ref[...])


def _gdn_body(qkv_ref, z_ref, small_ref, wconv_ref, alog_ref, dtb_ref, gnorm_ref, conv0_ref, s0_ref,
              tril_ref, expb_ref, expg_ref, bd_ref, fold_ref, foldt_ref,
              o_ref, convn_ref, sn_ref, xfull_ref, s_ref, *, n_chunks, tv_last):
    L = GDN_CHUNK
    c = pl.program_id(1)
    seqs = range(GDN_SEQS_PER_STEP)

    @pl.when(c == 0)
    def _():
        bd = _same_head_mask()[2]
        for j in seqs:
            xfull_ref[j, 0:CONV_PAD, :] = conv0_ref[j]
            s_ref[j] = jnp.where(bd, _dot3_left(s0_ref[j], foldt_ref[...]), 0.0)

    preps = [_gdn_prepare(c, qkv_ref.at[j], small_ref.at[j], wconv_ref, alog_ref, dtb_ref,
                          tril_ref, expb_ref, expg_ref, bd_ref, xfull_ref.at[j], n_chunks=n_chunks, tv_last=tv_last)
             for j in seqs]
    inverses = _gdn_inverses([prep["a"] for prep in preps])
    for j in seqs:
        _gdn_finish(preps[j], inverses[j], z_ref.at[j], gnorm_ref, bd_ref, o_ref.at[j], s_ref.at[j])

    @pl.when(c == n_chunks - 1)
    def _():
        for j in seqs:
            convn_ref[j] = xfull_ref[j, CONV_PAD + tv_last - (CONV_WIDTH - 1):CONV_PAD + tv_last, :]
            sn_ref[j] = _dot3_left(s_ref[j], fold_ref[...])

    for j in seqs:
        xfull_ref[j, 0:CONV_PAD, :] = xfull_ref[j, L:L + CONV_PAD, :]


def _gdn_call(u, n_seq, rows_per_seq, t_valid, conv0, s0, w_conv, a_log, dt_bias, g_norm):
    L = GDN_CHUNK
    G = GDN_SEQS_PER_STEP
    n_chunks = -(-t_valid // L)
    tv_last = t_valid - (n_chunks - 1) * L
    lane_row = lambda vals, off: jnp.zeros((1, LANE), F32).at[0, off:off + vals.shape[0]].set(vals)
    conv0_p = jnp.pad(conv0, ((0, 0), (CONV_PAD - (CONV_WIDTH - 1), 0), (0, 0)))
    u3 = u.reshape(n_seq, rows_per_seq, U_WIDTH)
    row = lambda w, col: pl.BlockSpec((G, L, w), lambda b, c: (b, c, col))
    full = lambda r, cc: pl.BlockSpec((r, cc), lambda b, c: (0, 0))
    per_seq = lambda r, cc: pl.BlockSpec((G, r, cc), lambda b, c: (b, 0, 0))
    out = pl.pallas_call(
        functools.partial(_gdn_body, n_chunks=n_chunks, tv_last=tv_last),
        grid=(n_seq // G, n_chunks),
        in_specs=[row(GDN_CONV_DIM, OFF_QKVB // GDN_CONV_DIM), row(GDN_WIDTH, OFF_ZB // GDN_WIDTH),
                  row(LANE, OFF_SMALL // LANE),
                  full(CONV_WIDTH, GDN_CONV_DIM), full(1, LANE), full(1, LANE), full(1, GDN_WIDTH),
                  per_seq(CONV_PAD, GDN_CONV_DIM), per_seq(GDN_WIDTH, HEAD_DIM),
                  full(L, L), full(LANE, GDN_WIDTH), full(LANE, GDN_WIDTH), full(GDN_WIDTH, GDN_WIDTH),
                  full(GDN_WIDTH, HEAD_DIM), full(HEAD_DIM, GDN_WIDTH)],
        out_specs=[pl.BlockSpec((G, L, GDN_WIDTH), lambda b, c: (b, c, 0)),
                   per_seq(CONV_WIDTH - 1, GDN_CONV_DIM), per_seq(GDN_WIDTH, HEAD_DIM)],
        out_shape=[jax.ShapeDtypeStruct((n_seq, n_chunks * L, GDN_WIDTH), F32),
                   jax.ShapeDtypeStruct((n_seq, CONV_WIDTH - 1, GDN_CONV_DIM), F32),
                   jax.ShapeDtypeStruct((n_seq, GDN_WIDTH, HEAD_DIM), F32)],
        scratch_shapes=[pltpu.VMEM((G, L + CONV_PAD, GDN_CONV_DIM), F32),
                        pltpu.VMEM((G, GDN_WIDTH, GDN_WIDTH), F32)],
        compiler_params=_params(("parallel", "arbitrary")), name="gdn",
    )(u3, u3, u3, w_conv, lane_row(a_log, SMALL_DECAY), lane_row(dt_bias, SMALL_DECAY),
      jnp.tile(g_norm, GDN_HEADS).reshape(1, GDN_WIDTH), conv0_p, s0.reshape(n_seq, GDN_WIDTH, HEAD_DIM),
      _tril_incl(L), _expand_matrix(SMALL_BETA, GDN_HEADS, HEAD_DIM), _expand_matrix(SMALL_DECAY, GDN_HEADS, HEAD_DIM),
      _block_diag_ones(GDN_WIDTH, HEAD_DIM), _fold_matrix(GDN_HEADS, HEAD_DIM), _fold_matrix(GDN_HEADS, HEAD_DIM).T)
    o, conv_new, s_new = out
    return o.reshape(n_seq * n_chunks * L, GDN_WIDTH), conv_new, s_new.reshape(n_seq, GDN_HEADS, GDN_DK, HEAD_DIM)


def _ssd_body(xbc_ref, z_ref, small_ref, wconv_ref, bconv_ref, alog_ref, dtb_ref, dskip_ref, gnorm_ref,
              conv0_ref, h0_ref, tril_ref, expc_ref,
              o_ref, convn_ref, hn_ref, xfull_ref, ht_ref, *, n_chunks, tv_last):
    L = SSD_CHUNK
    W = SSD_WIDTH
    GW = W // SSD_GROUPS
    c = pl.program_id(1)

    @pl.when(c == 0)
    def _():
        xfull_ref[0:CONV_PAD, :] = conv0_ref[...]
        for j in range(W // LANE):
            ht_ref[:, j * LANE:(j + 1) * LANE] = h0_ref[j * LANE:(j + 1) * LANE, :].T

    xs = _silu(_chunk_conv(xbc_ref, xfull_ref, wconv_ref, L) + bconv_ref[...])
    x_c, bm, cm = xs[:, 0:W], xs[:, W:W + GW], xs[:, W + GW:W + 2 * GW]
    small = small_ref[...]
    lane = lax.broadcasted_iota(jnp.int32, (L, LANE), 1)
    dt_s = jnp.where((lane >= SMALL_DT) & (lane < SMALL_DT + SSD_HEADS), _softplus(small + dtb_ref[...]), 0.0)
    if tv_last < L:
        last_rows = jnp.where(c == n_chunks - 1, tv_last, L)
        keep_s = lax.broadcasted_iota(jnp.int32, (L, LANE), 0) < last_rows
        keep_g = lax.broadcasted_iota(jnp.int32, (L, GW), 0) < last_rows
        dt_s = jnp.where(keep_s, dt_s, 0.0)
        bm, cm = jnp.where(keep_g, bm, 0.0), jnp.where(keep_g, cm, 0.0)
    ac_s = dt_s * (-jnp.exp(alog_ref[...]))
    acs_s = _dot3_right(tril_ref[...], ac_s)
    acs_e = _dot3_left(acs_s, expc_ref[...])
    dt_e = _dot3_left(dt_s, expc_ref[...])
    xdt = x_c * dt_e
    e = jnp.exp(acs_e)
    last = acs_e[L - 1:L, :]
    xw = xdt * jnp.exp(last - acs_e)
    dchunk = jnp.exp(last)
    acs_t = acs_s.T
    incl = lax.broadcasted_iota(jnp.int32, (L, L), 0) >= lax.broadcasted_iota(jnp.int32, (L, L), 1)
    first_half = lax.broadcasted_iota(jnp.int32, (L, LANE), 1) < HEAD_DIM
    ht = ht_ref[...]
    ys = []
    for g in range(SSD_GROUPS):
        gl = slice(g * GW, (g + 1) * GW)
        bg = bm[:, g * SSD_STATE:(g + 1) * SSD_STATE].astype(BF16)
        cg = cm[:, g * SSD_STATE:(g + 1) * SSD_STATE].astype(BF16)
        scores = _dot_nt(cg, bg)
        y_off = _dot(cg, ht[:, gl].astype(BF16)) * e[:, gl]
        ht_ref[:, gl] = ht[:, gl] * dchunk[:, gl] + _dot_tn(bg, xw[:, gl].astype(BF16))
        pairs = []
        for pp in range(GW // LANE):
            xp = xdt[:, g * GW + pp * LANE:g * GW + (pp + 1) * LANE]
            yp = jnp.zeros((L, LANE), F32)
            for hh in range(2):
                h = g * (SSD_HEADS // SSD_GROUPS) + pp * 2 + hh
                col = acs_s[:, SMALL_DT + h:SMALL_DT + h + 1]
                rowv = acs_t[SMALL_DT + h:SMALL_DT + h + 1, :]
                decay = jnp.exp(jnp.where(incl, col - rowv, NEG_BIG))
                xm = jnp.where(first_half if hh == 0 else jnp.logical_not(first_half), xp, 0.0)
                yp = yp + _dot((scores * decay).astype(BF16), xm.astype(BF16))
            pairs.append(yp)
        ys.append(jnp.concatenate(pairs, axis=1) + y_off)
    y = (jnp.concatenate(ys, axis=1) + dskip_ref[...] * x_c) * _silu(z_ref[...])
    outs = []
    for g in range(SSD_GROUPS):
        yg = y[:, g * GW:(g + 1) * GW]
        outs.append(yg * lax.rsqrt(jnp.mean(yg * yg, axis=-1, keepdims=True) + RMS_EPS))
    o_ref[...] = jnp.concatenate(outs, axis=1) * gnorm_ref[...]

    @pl.when(c == n_chunks - 1)
    def _():
        convn_ref[...] = xfull_ref[CONV_PAD + tv_last - (CONV_WIDTH - 1):CONV_PAD + tv_last, :]
        for j in range(W // LANE):
            hn_ref[j * LANE:(j + 1) * LANE, :] = ht_ref[:, j * LANE:(j + 1) * LANE].T

    xfull_ref[0:CONV_PAD, :] = xfull_ref[L:L + CONV_PAD, :]


def _ssd_call(u, n_seq, rows_per_seq, t_valid, conv0, h0, w_conv, b_conv, a_log, dt_bias, d_skip, g_norm):
    L = SSD_CHUNK
    n_chunks = -(-t_valid // L)
    tv_last = t_valid - (n_chunks - 1) * L
    rb = rows_per_seq // L
    lane_row = lambda vals, off: jnp.zeros((1, LANE), F32).at[0, off:off + vals.shape[0]].set(vals)
    conv0_p = jnp.pad(conv0, ((0, 0), (CONV_PAD - (CONV_WIDTH - 1), 0), (0, 0)))
    row = lambda w, col: pl.BlockSpec((L, w), lambda b, c: (b * rb + c, col))
    full = lambda r, cc: pl.BlockSpec((r, cc), lambda b, c: (0, 0))
    per_seq = lambda r, cc: pl.BlockSpec((None, r, cc), lambda b, c: (b, 0, 0))
    out = pl.pallas_call(
        functools.partial(_ssd_body, n_chunks=n_chunks, tv_last=tv_last),
        grid=(n_seq, n_chunks),
        in_specs=[row(SSD_XBC, OFF_XBC // SSD_XBC), row(SSD_WIDTH, OFF_ZC // SSD_WIDTH), row(LANE, OFF_SMALL // LANE),
                  full(CONV_WIDTH, SSD_XBC), full(1, SSD_XBC), full(1, LANE), full(1, LANE),
                  full(1, SSD_WIDTH), full(1, SSD_WIDTH),
                  per_seq(CONV_PAD, SSD_XBC), per_seq(SSD_WIDTH, SSD_STATE),
                  full(L, L), full(LANE, SSD_WIDTH)],
        out_specs=[pl.BlockSpec((L, SSD_WIDTH), lambda b, c: (b * n_chunks + c, 0)),
                   per_seq(CONV_WIDTH - 1, SSD_XBC), per_seq(SSD_WIDTH, SSD_STATE)],
        out_shape=[jax.ShapeDtypeStruct((n_seq * n_chunks * L, SSD_WIDTH), F32),
                   jax.ShapeDtypeStruct((n_seq, CONV_WIDTH - 1, SSD_XBC), F32),
                   jax.ShapeDtypeStruct((n_seq, SSD_WIDTH, SSD_STATE), F32)],
        scratch_shapes=[pltpu.VMEM((L + CONV_PAD, SSD_XBC), F32), pltpu.VMEM((SSD_STATE, SSD_WIDTH), F32)],
        compiler_params=_params(("parallel", "arbitrary")), name="ssd",
    )(u, u, u, w_conv, b_conv.reshape(1, SSD_XBC), lane_row(a_log, SMALL_DT), lane_row(dt_bias, SMALL_DT),
      jnp.repeat(d_skip, HEAD_DIM).reshape(1, SSD_WIDTH), g_norm.reshape(1, SSD_WIDTH),
      conv0_p, h0.reshape(n_seq, SSD_WIDTH, SSD_STATE), _tril_incl(L), _expand_matrix(SMALL_DT, SSD_HEADS, HEAD_DIM))
    o, conv_new, h_new = out
    return o, conv_new, h_new.reshape(n_seq, SSD_HEADS, HEAD_DIM, SSD_STATE)


ROUTE_GATE, ROUTE_IDX, ROUTE_RANK = 0, TOP_K, 2 * TOP_K


def _strict_lower(n):
    return jnp.asarray(np.tril(np.ones((n, n), bool), -1), BF16)


def _proj_out_body(x_ref, oa_ref, ob_ref, oc_ref, w_ref, g_ref, b_ref, rw_ref, rb_ref, lower_ref,
                   x1_ref, route_ref, count_ref):
    mix = _dot(oa_ref[...].astype(BF16), w_ref[0:256, :])
    mix += _dot(ob_ref[...].astype(BF16), w_ref[256:512, :])
    mix += _dot(oc_ref[...].astype(BF16), w_ref[512:1024, :])
    x1 = _layer_norm_f32(DEEPNORM_ALPHA * x_ref[...] + mix, g_ref[...], b_ref[...])
    x1_ref[...] = x1
    logits = _mm3(x1, rw_ref[...]) + rb_ref[...]

    @pl.when(pl.program_id(0) == 0)
    def _():
        count_ref[...] = jnp.zeros_like(count_ref)

    lane = lax.broadcasted_iota(jnp.int32, logits.shape, 1)
    lg = jnp.where(lane < N_EXPERTS, logits, -jnp.inf)
    vals, idxs = [], []
    for _ in range(TOP_K):
        m = jnp.max(lg, axis=1, keepdims=True)
        idx = jnp.min(jnp.where(lg == m, lane, LANE), axis=1, keepdims=True)
        vals.append(m)
        idxs.append(idx)
        lg = jnp.where(lane == idx, -jnp.inf, lg)
    exps = [jnp.exp(v - vals[0]) for v in vals]
    denom = exps[0] + exps[1] + exps[2] + exps[3]
    hot = [lane == idx for idx in idxs]
    multi = jnp.where(hot[0] | hot[1] | hot[2] | hot[3], 1.0, 0.0)
    before = _dot(lower_ref[...], multi.astype(BF16)) + count_ref[...]
    route = jnp.zeros(logits.shape, F32)
    for k in range(TOP_K):
        rank = jnp.sum(jnp.where(hot[k], before, 0.0), axis=1, keepdims=True)
        route = jnp.where(lane == ROUTE_GATE + k, exps[k] / denom, route)
        route = jnp.where(lane == ROUTE_IDX + k, idxs[k].astype(F32), route)
        route = jnp.where(lane == ROUTE_RANK + k, rank, route)
    route_ref[...] = route
    count_ref[...] += jnp.sum(multi, axis=0, keepdims=True)


def _proj_out_call(x, o_a, o_b, o_c, w_bf16, g, b, router_w, router_b):
    n, d = x.shape
    bm = _row_block(n)
    rw = jnp.pad(router_w, ((0, 0), (0, LANE - N_EXPERTS)))
    rb = jnp.pad(router_b, (0, LANE - N_EXPERTS)).reshape(1, LANE)
    row = lambda w: pl.BlockSpec((bm, w), lambda i: (i, 0))
    full = lambda r, c: pl.BlockSpec((r, c), lambda i: (0, 0))
    return pl.pallas_call(
        _proj_out_body, grid=(n // bm,),
        in_specs=[row(d), row(SB_WIDTH), row(GDN_WIDTH), row(SSD_WIDTH), full(d, d), full(1, d), full(1, d),
                  full(d, LANE), full(1, LANE), full(bm, bm)],
        out_specs=[row(d), row(LANE), full(1, LANE)],
        out_shape=[jax.ShapeDtypeStruct((n, d), F32), jax.ShapeDtypeStruct((n, LANE), F32),
                   jax.ShapeDtypeStruct((1, LANE), F32)],
        compiler_params=_params(("arbitrary",)), name="proj_out",
    )(x, o_a, o_b, o_c, w_bf16, g.reshape(1, d), b.reshape(1, d), rw, rb, _strict_lower(bm))


def _moe_body(be_ref, nb_ref, rows_ref, wg_ref, bg_ref, wu_ref, bu_ref, wd_ref, bd_ref, o_ref, wg_s, wu_s, wd_s):
    i = pl.program_id(0)
    prev = be_ref[jnp.maximum(i - 1, 0)]

    @pl.when(jnp.logical_or(i == 0, be_ref[i] != prev))
    def _():
        wg_s[...] = wg_ref[...].astype(BF16)
        wu_s[...] = wu_ref[...].astype(BF16)
        wd_s[...] = wd_ref[...].astype(BF16)

    @pl.when(i < nb_ref[0])
    def _():
        xb = rows_ref[...].astype(BF16)
        glu = jnp.minimum(_dot(xb, wg_s[...]) + bg_ref[...], SWIGLU_LIMIT)
        lin = jnp.clip(_dot(xb, wu_s[...]) + bu_ref[...], -SWIGLU_LIMIT, SWIGLU_LIMIT)
        act = glu * jax.nn.sigmoid(SWIGLU_ALPHA * glu) * (lin + 1.0)
        o_ref[...] = _dot(act.astype(BF16), wd_s[...]) + bd_ref[...]

    @pl.when(i >= nb_ref[0])
    def _():
        o_ref[...] = jnp.zeros_like(o_ref)


def _moe_call(rows_in, block_expert, n_used, w_gate, b_gate, w_up, b_up, w_down, b_down, layer, blk):
    n_rows, d = rows_in.shape
    n_blocks = n_rows // blk
    wspec = lambda r, c: pl.BlockSpec((None, None, r, c), lambda i, be, nb: (layer, be[i], 0, 0))
    grid_spec = pltpu.PrefetchScalarGridSpec(
        num_scalar_prefetch=2, grid=(n_blocks,),
        in_specs=[pl.BlockSpec((blk, d), lambda i, be, nb: (i, 0)),
                  wspec(d, D_EXPERT), wspec(1, D_EXPERT), wspec(d, D_EXPERT), wspec(1, D_EXPERT),
                  wspec(D_EXPERT, d), wspec(1, d)],
        out_specs=pl.BlockSpec((blk, d), lambda i, be, nb: (i, 0)),
        scratch_shapes=[pltpu.VMEM((d, D_EXPERT), BF16), pltpu.VMEM((d, D_EXPERT), BF16),
                        pltpu.VMEM((D_EXPERT, d), BF16)])
    depth = w_gate.shape[0]
    return pl.pallas_call(
        _moe_body, grid_spec=grid_spec,
        out_shape=jax.ShapeDtypeStruct((n_rows, d), F32),
        compiler_params=_params(("arbitrary",)), name="moe_experts",
    )(block_expert, n_used, rows_in, w_gate, b_gate.reshape(depth, N_EXPERTS, 1, D_EXPERT),
      w_up, b_up.reshape(depth, N_EXPERTS, 1, D_EXPERT), w_down, b_down.reshape(depth, N_EXPERTS, 1, d))


def _route(route, counts, blk):
    n_tok = route.shape[0]
    gate = route[:, ROUTE_GATE:ROUTE_GATE + TOP_K]
    expert = route[:, ROUTE_IDX:ROUTE_IDX + TOP_K].astype(jnp.int32)
    rank = route[:, ROUTE_RANK:ROUTE_RANK + TOP_K].astype(jnp.int32)
    counts = counts[0, :N_EXPERTS].astype(jnp.int32)
    n_assign = n_tok * TOP_K
    n_blocks = -(-(n_assign + N_EXPERTS * (blk - 1)) // blk)
    padded = (counts + blk - 1) // blk * blk
    padded_end = jnp.cumsum(padded)
    padded_start = padded_end - padded
    start = jnp.cumsum(counts) - counts
    dest = padded_start[expert] + rank
    token = jnp.broadcast_to(jnp.arange(n_tok, dtype=jnp.int32)[:, None], (n_tok, TOP_K))
    _, token_sorted = lax.sort((dest.reshape(-1), token.reshape(-1)), num_keys=1)
    block_first = jnp.arange(n_blocks, dtype=jnp.int32) * blk
    block_expert = jnp.minimum(jnp.sum(padded_end[None, :] <= block_first[:, None], axis=1, dtype=jnp.int32),
                               N_EXPERTS - 1)
    r = jnp.arange(n_blocks * blk, dtype=jnp.int32)
    e_r = jnp.repeat(block_expert, blk)
    src = token_sorted[jnp.clip(start[e_r] + r - padded_start[e_r], 0, n_assign - 1)]
    dest_t = dest.T.reshape(-1)
    n_used = (padded_end[-1] // blk).astype(jnp.int32).reshape(1)
    return gate, src, dest_t, block_expert, n_used


def _post_body(x1_ref, r0_ref, r1_ref, r2_ref, r3_ref, gate_ref, p_ref, g_ref, b_ref, wg_ref, bg_ref, wp_ref, o_ref):
    gate = gate_ref[...]
    ffn = gate[:, 0:1] * r0_ref[...]
    for k, r_ref in ((1, r1_ref), (2, r2_ref), (3, r3_ref)):
        ffn += gate[:, k:k + 1] * r_ref[...]
    x2 = _layer_norm_f32(DEEPNORM_ALPHA * x1_ref[...] + ffn, g_ref[...], b_ref[...])
    gate_logit = _dot(x2.astype(BF16), wg_ref[...]) + bg_ref[...]
    emb = _dot(p_ref[...].astype(BF16), wp_ref[...])
    o_ref[...] = x2 + jax.nn.sigmoid(gate_logit) * emb


def _post_call(x1, rows_k, gate, p, g, b, wg_bf16, bg, wp_bf16):
    n, d = x1.shape
    bm = _row_block(n)
    nb = n // bm
    row = lambda w: pl.BlockSpec((bm, w), lambda i: (i, 0))
    plane = lambda k: pl.BlockSpec((bm, d), lambda i: (k * nb + i, 0))
    full = lambda r, c: pl.BlockSpec((r, c), lambda i: (0, 0))
    return pl.pallas_call(
        _post_body, grid=(nb,),
        in_specs=[row(d)] + [plane(k) for k in range(TOP_K)] + [row(TOP_K), row(PLE_DIM), full(1, d), full(1, d),
                                                                 full(d, d), full(1, d), full(PLE_DIM, d)],
        out_specs=row(d),
        out_shape=jax.ShapeDtypeStruct((n, d), F32),
        compiler_params=_params(("parallel",)), name="post_ffn",
    )(x1, rows_k, rows_k, rows_k, rows_k, gate, p, g.reshape(1, d), b.reshape(1, d), wg_bf16, bg.reshape(1, d),
      wp_bf16)


def kernel(x_prompt, x_sample, cache_k_a, cache_v_a, state_conv_b, state_delta_b, state_conv_c, state_ssm_c, page_table, p_prompt, p_sample, ln_in_g, ln_in_b, w_in, sb_bias, w_conv_b, a_log_b, dt_bias_b, g_norm_b, w_conv_c, b_conv_c, a_log_c, dt_bias_c, d_skip_c, g_norm_c, w_out, ln1_g, ln1_b, router_w, router_b, w_gate, b_gate, w_up, b_up, w_down, b_down, ln2_g, ln2_b, ple_gate_w, ple_gate_b, ple_proj_w):
    depth = w_in.shape[0]
    w_in_r = jnp.pad(w_in[:, :, _IN_PERM], ((0, 0), (0, 0), (0, U_WIDTH - _IN_PERM.size))).astype(BF16)
    w_out_b = w_out.astype(BF16)
    ple_gate_w_b = ple_gate_w.astype(BF16)
    ple_proj_w_b = ple_proj_w.astype(BF16)
    n_pool = cache_k_a.shape[1]
    cache_k = cache_k_a.transpose(0, 1, 3, 4, 2).reshape(depth, n_pool, SB_WIDTH, PAGE_SIZE)
    cache_v = cache_v_a.transpose(0, 1, 3, 4, 2).reshape(depth, n_pool, SB_WIDTH, PAGE_SIZE)

    def mixers_and_router(i, x, bsz, t, sample):
        n = bsz * t
        u = _proj_in_call(x, w_in_r[i])
        k_a = u[:, OFF_K:OFF_K + SB_WIDTH].reshape(bsz, t, SB_HEADS, HEAD_DIM)
        v_a = u[:, OFF_V:OFF_V + SB_WIDTH].reshape(bsz, t, SB_HEADS, HEAD_DIM)
        if sample:
            o_a = _sb_decode_call(u[:, OFF_Q:OFF_Q + SB_WIDTH], sb_bias[i], cache_k, cache_v, page_table, i)
            conv_b, delta_b, conv_c, ssm_c = state_conv_b[i], state_delta_b[i], state_conv_c[i], state_ssm_c[i]
            rows_per_seq = SSD_CHUNK
            u_mix = jnp.pad(u.reshape(bsz, t, U_WIDTH), ((0, 0), (0, rows_per_seq - t), (0, 0)))
            u_mix = u_mix.reshape(bsz * rows_per_seq, U_WIDTH)
        else:
            o_a = _sb_prefill_call(u, sb_bias[i], bsz, t)
            conv_b = jnp.zeros((bsz, CONV_WIDTH - 1, GDN_CONV_DIM), F32)
            delta_b = jnp.zeros((bsz, GDN_HEADS, GDN_DK, HEAD_DIM), F32)
            conv_c = jnp.zeros((bsz, CONV_WIDTH - 1, SSD_XBC), F32)
            ssm_c = jnp.zeros((bsz, SSD_HEADS, HEAD_DIM, SSD_STATE), F32)
            rows_per_seq = t
            u_mix = u
        o_b, conv_b_new, delta_new = _gdn_call(u_mix, bsz, rows_per_seq, t, conv_b, delta_b, w_conv_b[i],
                                               a_log_b[i], dt_bias_b[i], g_norm_b[i])
        o_c, conv_c_new, ssm_new = _ssd_call(u_mix, bsz, rows_per_seq, t, conv_c, ssm_c, w_conv_c[i], b_conv_c[i],
                                             a_log_c[i], dt_bias_c[i], d_skip_c[i], g_norm_c[i])
        if sample:
            o_b = o_b.reshape(bsz, -1, GDN_WIDTH)[:, :t].reshape(n, GDN_WIDTH)
            o_c = o_c.reshape(bsz, -1, SSD_WIDTH)[:, :t].reshape(n, SSD_WIDTH)
        x1, route, counts = _proj_out_call(x, o_a, o_b, o_c, w_out_b[i], ln1_g[i], ln1_b[i], router_w[i], router_b[i])
        blk = MOE_BLOCK_PROMPT if n * TOP_K >= MOE_BLOCK_PROMPT * N_EXPERTS else MOE_BLOCK_SMALL
        gate, src, dest_t, block_expert, n_used = _route(route, counts, blk)
        return (x1, gate, dest_t, block_expert, n_used, blk, x1[src]), (k_a, v_a, conv_b_new, delta_new, conv_c_new,
                                                                        ssm_new)

    def experts(i, routed):
        x1, gate, dest_t, block_expert, n_used, blk, rows_in = routed
        rows_out = _moe_call(rows_in, block_expert, n_used, w_gate, b_gate, w_up, b_up, w_down, b_down, i, blk)
        return rows_out[dest_t]

    def combine(i, routed, rows_k, p):
        x1, gate = routed[0], routed[1]
        return _post_call(x1, rows_k, gate, p[i].reshape(x1.shape[0], PLE_DIM), ln2_g[i], ln2_b[i], ple_gate_w_b[i],
                          ple_gate_b[i], ple_proj_w_b[i])

    groups = [(x_prompt, p_prompt, False), (x_sample, p_sample, True)]
    xs = [_ln_call(x.reshape(-1, x.shape[-1]), ln_in_g, ln_in_b) for x, _, _ in groups]
    states = [[], []]
    for i in range(depth):
        routed = []
        for gi, (x, _, sample) in enumerate(groups):
            r, st = mixers_and_router(i, xs[gi], x.shape[0], x.shape[1], sample)
            routed.append(r)
            states[gi].append(st)
        rows_k = [experts(i, r) for r in routed]
        xs = [combine(i, routed[gi], rows_k[gi], groups[gi][1]) for gi in range(len(groups))]
    y_prompt, y_sample = (xs[gi].reshape(groups[gi][0].shape) for gi in range(len(groups)))
    new_prompt, new_sample = ([jnp.stack([s[j] for s in st]) for j in range(6)] for st in states)
    return (y_prompt, y_sample, *new_prompt, *new_sample)
```

```python
import functools

import jax
import jax.numpy as jnp
import numpy as np
from jax import lax
from jax.experimental import pallas as pl
from jax.experimental.pallas import tpu as pltpu

F32 = jnp.float32
BF16 = jnp.bfloat16

D_MODEL = 1024
DEPTH = 4
PAGE_SIZE = 128
HEAD_DIM = 64
CONV_WIDTH = 4
SB_WIDTH = 256
SB_HEADS = 4
GDN_WIDTH = 256
GDN_HEADS = 4
GDN_DK = 64
GDN_CONV_DIM = 768
GDN_CHUNK = 64
SSD_WIDTH = 512
SSD_HEADS = 8
SSD_GROUPS = 2
SSD_STATE = 128
SSD_XBC = 1024
SSD_CHUNK = 128
N_EXPERTS = 32
TOP_K = 4
D_EXPERT = 512
SWIGLU_ALPHA = 1.702
SWIGLU_LIMIT = 7.0
PLE_DIM = 256
DEEPNORM_ALPHA = (2 * DEPTH) ** 0.25
LN_EPS = 1e-5
RMS_EPS = 1e-6

_REF_WIDTHS = (256, 256, 256, 768, 256, 4, 4, 512, 1024, 8)
_REF_OFF = tuple(int(v) for v in np.cumsum((0,) + _REF_WIDTHS))
OFF_XBC, OFF_ZC, OFF_QKVB, OFF_Q, OFF_K, OFF_V, OFF_ZB, OFF_SMALL = 0, 1024, 1536, 2304, 2560, 2816, 3072, 3328
U_WIDTH = 3456
LANE = 128
SMALL_BETA, SMALL_DECAY, SMALL_DT = 0, GDN_HEADS, 2 * GDN_HEADS
_IN_SEGMENTS = (
    (_REF_OFF[8], _REF_OFF[9]),
    (_REF_OFF[7], _REF_OFF[8]),
    (_REF_OFF[3], _REF_OFF[4]),
    (_REF_OFF[0], _REF_OFF[3]),
    (_REF_OFF[4], _REF_OFF[5]),
    (_REF_OFF[5], _REF_OFF[7]),
    (_REF_OFF[9], _REF_OFF[10]),
)

VMEM_LIMIT = 48 * 1024 * 1024
MOE_BLOCK_PROMPT = 512
MOE_BLOCK_SMALL = 8
SMALL_ROUTE_ASSIGNMENTS = 1024
SB_BLOCK = 256
DEC_PAGES_PER_STEP = 16
CONV_PAD = 8
NEG_BIG = -1e30
LOG2E = 1.4426950408889634


def _params(sem):
    return pltpu.CompilerParams(dimension_semantics=sem, vmem_limit_bytes=VMEM_LIMIT)


def _row_block(n):
    return n if n < 512 else 512


def _layer_norm_f32(x, g, b):
    mu = jnp.mean(x, axis=-1, keepdims=True)
    xc = x - mu
    var = jnp.mean(xc * xc, axis=-1, keepdims=True)
    return xc * lax.rsqrt(var + LN_EPS) * g + b


def _softplus(z):
    return jnp.maximum(z, 0.0) + jnp.log1p(jnp.exp(-jnp.abs(z)))


def _softplus2(z2):
    return jnp.maximum(z2, 0.0) + jnp.log2(1.0 + jnp.exp2(-jnp.abs(z2)))


def _silu(x):
    return x * jax.nn.sigmoid(x)


def _dot(a, b):
    return jnp.dot(a, b, preferred_element_type=F32)


def _dot_nt(a, b):
    return lax.dot_general(a, b, (((1,), (1,)), ((), ())), preferred_element_type=F32)


def _dot_tn(a, b):
    return lax.dot_general(a, b, (((0,), (0,)), ((), ())), preferred_element_type=F32)


def _split2(x):
    hi = x.astype(BF16)
    return hi, (x - hi.astype(F32)).astype(BF16)


def _split3(x):
    hi = x.astype(BF16)
    r = x - hi.astype(F32)
    mid = r.astype(BF16)
    return hi, mid, (r - mid.astype(F32)).astype(BF16)


def _split_dot(x, w_bf16):
    hi, lo = _split2(x)
    return _dot(hi, w_bf16) + _dot(lo, w_bf16)


def _dot3_left(x, w_bf16):
    hi, mid, lo = _split3(x)
    return _dot(hi, w_bf16) + (_dot(mid, w_bf16) + _dot(lo, w_bf16))


def _dot3_right(w_bf16, x):
    hi, mid, lo = _split3(x)
    return _dot(w_bf16, hi) + (_dot(w_bf16, mid) + _dot(w_bf16, lo))


def _mm3(a, b):
    ah, al = _split2(a)
    bh, bl = _split2(b)
    return _dot(ah, bh) + (_dot(ah, bl) + _dot(al, bh))


def _cumsum_matrix(n):
    j = np.arange(n)[:, None]
    s = np.arange(n)[None, :]
    return jnp.asarray(np.concatenate([(j > s), np.ones((n, LANE), bool)], axis=1), BF16)


def _tril_incl(n):
    return jnp.asarray(np.tril(np.ones((n, n), bool)), BF16)


def _expand_matrix(first_lane, heads, width):
    m = np.zeros((LANE, heads * width), bool)
    for h in range(heads):
        m[first_lane + h, h * width:(h + 1) * width] = True
    return jnp.asarray(m, BF16)


def _block_diag_ones(n, blk):
    i = np.arange(n)
    return jnp.asarray((i[:, None] // blk) == (i[None, :] // blk), BF16)


def _fold_matrix(heads, width):
    i = np.arange(heads * width)
    return jnp.asarray((i[:, None] % width) == np.arange(width)[None, :], BF16)


def _ln_body(x_ref, g_ref, b_ref, o_ref):
    o_ref[...] = _layer_norm_f32(x_ref[...], g_ref[...], b_ref[...])


def _ln_call(x, g, b):
    n, d = x.shape
    bm = _row_block(n)
    return pl.pallas_call(
        _ln_body, grid=(n // bm,),
        in_specs=[pl.BlockSpec((bm, d), lambda i: (i, 0)),
                  pl.BlockSpec((1, d), lambda i: (0, 0)), pl.BlockSpec((1, d), lambda i: (0, 0))],
        out_specs=pl.BlockSpec((bm, d), lambda i: (i, 0)),
        out_shape=jax.ShapeDtypeStruct((n, d), F32),
        compiler_params=_params(("parallel",)), name="ln_in",
    )(x, g.reshape(1, d), b.reshape(1, d))


_PROJ_CHUNK = 384


def _proj_in_body(x_ref, w_ref, o_ref):
    xb = x_ref[...].astype(BF16)
    for j in range(0, U_WIDTH, _PROJ_CHUNK):
        o_ref[:, j:j + _PROJ_CHUNK] = _dot(xb, w_ref[:, j:j + _PROJ_CHUNK])


def _proj_in_call(x, w_bf16):
    n, d = x.shape
    bm = _row_block(n)
    return pl.pallas_call(
        _proj_in_body, grid=(n // bm,),
        in_specs=[pl.BlockSpec((bm, d), lambda i: (i, 0)),
                  pl.BlockSpec((d, U_WIDTH), lambda i: (0, 0))],
        out_specs=pl.BlockSpec((bm, U_WIDTH), lambda i: (i, 0)),
        out_shape=jax.ShapeDtypeStruct((n, U_WIDTH), F32),
        compiler_params=_params(("parallel",)), name="proj_in",
    )(x, w_bf16)


def _sb_prefill_body(bias_ref, q_ref, k_ref, v_ref, u_ref, o_ref, qh_ref, acc_ref, carry_ref):
    blk = q_ref.shape[0]
    i = pl.program_id(1)
    q = q_ref[...] * (HEAD_DIM ** -0.5 * LOG2E)
    lane_head = lax.broadcasted_iota(jnp.int32, (1, SB_WIDTH), 1) // HEAD_DIM
    for h in range(SB_HEADS):
        qh_ref[h] = jnp.where(lane_head == h, q, 0.0).astype(BF16)
    acc_ref[...] = jnp.zeros_like(acc_ref)
    carry_ref[...] = jnp.zeros_like(carry_ref)
    rows = lax.broadcasted_iota(jnp.int32, (blk, blk), 0)
    cols = lax.broadcasted_iota(jnp.int32, (blk, blk), 1)

    def key_block(j, diagonal):
        start = pl.multiple_of(j * blk, blk)
        kj = k_ref[pl.ds(start, blk), :].astype(BF16)
        vj = v_ref[pl.ds(start, blk), :].astype(BF16)
        heads = range(SB_HEADS)
        visible = cols < rows
        zs = [_dot_nt(qh_ref[h], kj) + bias_ref[h] for h in heads]
        log_keeps = [-_softplus2(z) for z in zs]
        logits = [z + lk for z, lk in zip(zs, log_keeps)]
        if diagonal:
            log_keeps = [jnp.where(visible, lk, 0.0) for lk in log_keeps]
        lts = [_split_dot(lk, u_ref[...]) for lk in log_keeps]
        carries = [carry_ref[h] for h in heads]
        ws = [jnp.exp2(logits[h] + (lts[h][:, :blk] + jnp.concatenate([carries[h]] * (blk // LANE), axis=1)))
              for h in heads]
        if diagonal:
            ws = [jnp.where(visible, w, 0.0) for w in ws]
        for h in heads:
            acc_ref[h] += _dot(ws[h].astype(BF16), vj)
            carry_ref[h] = carries[h] + lts[h][:, blk:]

    key_block(i, True)

    def earlier(t, carry):
        key_block(i - 1 - t, False)
        return carry

    lax.fori_loop(0, i, earlier, 0)
    out = jnp.zeros((blk, SB_WIDTH), F32)
    for h in range(SB_HEADS):
        out = out + jnp.where(lane_head == h, acc_ref[h], 0.0)
    o_ref[...] = out


def _sb_prefill_call(u, bias, bsz, t):
    blk = SB_BLOCK
    nq = t // blk
    grid_spec = pltpu.PrefetchScalarGridSpec(
        num_scalar_prefetch=0, grid=(bsz, nq),
        in_specs=[pl.BlockSpec(memory_space=pltpu.SMEM),
                  pl.BlockSpec((blk, SB_WIDTH), lambda b, i: (b * nq + i, OFF_Q // SB_WIDTH)),
                  pl.BlockSpec((t, SB_WIDTH), lambda b, i: (b, OFF_K // SB_WIDTH)),
                  pl.BlockSpec((t, SB_WIDTH), lambda b, i: (b, OFF_V // SB_WIDTH)),
                  pl.BlockSpec((blk, blk + LANE), lambda b, i: (0, 0))],
        out_specs=pl.BlockSpec((blk, SB_WIDTH), lambda b, i: (b * nq + i, 0)),
        scratch_shapes=[pltpu.VMEM((SB_HEADS, blk, SB_WIDTH), BF16),
                        pltpu.VMEM((SB_HEADS, blk, SB_WIDTH), F32),
                        pltpu.VMEM((SB_HEADS, blk, LANE), F32)])
    return pl.pallas_call(
        _sb_prefill_body, grid_spec=grid_spec,
        out_shape=jax.ShapeDtypeStruct((bsz * t, SB_WIDTH), F32),
        compiler_params=_params(("parallel", "arbitrary")), name="sb_prefill",
    )(bias * LOG2E, u, u, u, _cumsum_matrix(blk))


def _sb_decode_body(pt_ref, q_ref, bias_ref, u_ref, *refs):
    g = DEC_PAGES_PER_STEP
    k_refs, v_refs = refs[:g], refs[g:2 * g]
    o_ref, acc_ref, carry_ref = refs[2 * g], refs[2 * g + 1], refs[2 * g + 2]
    s = pl.program_id(1)

    @pl.when(s == 0)
    def _():
        acc_ref[...] = jnp.zeros_like(acc_ref)
        carry_ref[...] = jnp.zeros_like(carry_ref)

    row = lax.broadcasted_iota(jnp.int32, (8, SB_WIDTH), 0)
    lane_head = lax.broadcasted_iota(jnp.int32, (8, SB_WIDTH), 1) // HEAD_DIM
    own = row == lane_head
    qm = jnp.where(own, q_ref[...] * (HEAD_DIM ** -0.5 * LOG2E), 0.0).astype(BF16)
    acc = acc_ref[...]
    carry = carry_ref[...]
    zs = [_dot(qm, k_refs[c][...].astype(BF16)) + bias_ref[...] for c in range(g)]
    log_keeps = [-_softplus2(z) for z in zs]
    lt_all = _split_dot(jnp.concatenate(log_keeps, axis=0), u_ref[...])
    ws = []
    for c in range(g):
        lt = lt_all[8 * c:8 * c + 8]
        ws.append(jnp.exp2(zs[c] + log_keeps[c] + lt[:, :LANE] + carry).astype(BF16))
        carry = carry + lt[:, LANE:]
    for c in range(g):
        acc = acc + _dot_nt(ws[c], v_refs[c][...].astype(BF16))
    acc_ref[...] = acc
    carry_ref[...] = carry

    @pl.when(s == pl.num_programs(1) - 1)
    def _():
        o_ref[...] = jnp.sum(jnp.where(own, acc, 0.0), axis=0, keepdims=True)


def _sb_decode_call(q, bias, cache_k, cache_v, page_table, layer):
    bsz = q.shape[0]
    n_pages = page_table.shape[1]
    g = DEC_PAGES_PER_STEP
    n_steps = n_pages // g
    bias8 = jnp.zeros((8, LANE), F32).at[:SB_HEADS].set(jnp.broadcast_to(bias[:, None] * LOG2E, (SB_HEADS, LANE)))

    def page_spec(c):
        return pl.BlockSpec((None, None, SB_WIDTH, PAGE_SIZE),
                            lambda b, s, pt: (layer, pt[b, n_pages - 1 - (s * g + c)], 0, 0))

    grid_spec = pltpu.PrefetchScalarGridSpec(
        num_scalar_prefetch=1, grid=(bsz, n_steps),
        in_specs=[pl.BlockSpec((None, 1, SB_WIDTH), lambda b, s, pt: (b, 0, 0)),
                  pl.BlockSpec((8, LANE), lambda b, s, pt: (0, 0)),
                  pl.BlockSpec((PAGE_SIZE, 2 * LANE), lambda b, s, pt: (0, 0))]
                 + [page_spec(c) for c in range(g)] + [page_spec(c) for c in range(g)],
        out_specs=pl.BlockSpec((None, 1, SB_WIDTH), lambda b, s, pt: (b, 0, 0)),
        scratch_shapes=[pltpu.VMEM((8, SB_WIDTH), F32), pltpu.VMEM((8, LANE), F32)])
    out = pl.pallas_call(
        _sb_decode_body, grid_spec=grid_spec,
        out_shape=jax.ShapeDtypeStruct((bsz, 1, SB_WIDTH), F32),
        compiler_params=_params(("parallel", "arbitrary")), name="sb_decode",
    )(page_table, q.reshape(bsz, 1, SB_WIDTH), bias8, _cumsum_matrix(PAGE_SIZE),
      *([cache_k] * g), *([cache_v] * g))
    return out.reshape(bsz, SB_WIDTH)


def _chunk_conv(x_ref, xfull_ref, wconv_ref, rows):
    xfull_ref[CONV_PAD:CONV_PAD + rows, :] = x_ref[...]
    acc = wconv_ref[CONV_WIDTH - 1:CONV_WIDTH, :] * xfull_ref[CONV_PAD:CONV_PAD + rows, :]
    for j in range(1, CONV_WIDTH):
        acc = acc + wconv_ref[CONV_WIDTH - 1 - j:CONV_WIDTH - j, :] * xfull_ref[CONV_PAD - j:CONV_PAD - j + rows, :]
    return acc


def _run_interleaved(stage_generators):
    results = [None] * len(stage_generators)
    live = list(range(len(stage_generators)))
    while live:
        for j in list(live):
            try:
                next(stage_generators[j])
            except StopIteration as done:
                results[j] = done.value
                live.remove(j)
    return results


GDN_SEQS_PER_STEP = 4


def _mm_split(a_parts, b_parts):
    (ah, al), (bh, bl) = a_parts, b_parts
    return _dot(ah, bh) + (_dot(ah, bl) + _dot(al, bh))


def _same_head_mask():
    r2 = lax.broadcasted_iota(jnp.int32, (GDN_WIDTH, GDN_WIDTH), 0)
    c2 = lax.broadcasted_iota(jnp.int32, (GDN_WIDTH, GDN_WIDTH), 1)
    return r2, c2, (r2 // GDN_CHUNK) == (c2 // GDN_CHUNK)


def _gdn_prepare(c, qkv_ref, small_ref, wconv_ref, alog_ref, dtb_ref,
                 tril_ref, expb_ref, expg_ref, bd_ref, xfull_ref, *, n_chunks, tv_last):
    L = GDN_CHUNK
    W = GDN_WIDTH
    r2, c2, bd = _same_head_mask()
    xs = _silu(_chunk_conv(qkv_ref, xfull_ref, wconv_ref, L))
    q, k, v = xs[:, 0:W], xs[:, W:2 * W], xs[:, 2 * W:3 * W]
    yield
    small = small_ref[...]
    lane = lax.broadcasted_iota(jnp.int32, (L, LANE), 1)
    beta_s = jnp.where(lane < SMALL_DECAY, jax.nn.sigmoid(small), 0.0)
    g_s = jnp.where((lane >= SMALL_DECAY) & (lane < SMALL_DT),
                    -jnp.exp(alog_ref[...]) * _softplus(small + dtb_ref[...]), 0.0)
    if tv_last < L:
        last_rows = jnp.where(c == n_chunks - 1, tv_last, L)
        keep_w = lax.broadcasted_iota(jnp.int32, (L, W), 0) < last_rows
        keep_s = lax.broadcasted_iota(jnp.int32, (L, LANE), 0) < last_rows
        q, k, v = jnp.where(keep_w, q, 0.0), jnp.where(keep_w, k, 0.0), jnp.where(keep_w, v, 0.0)
        beta_s, g_s = jnp.where(keep_s, beta_s, 0.0), jnp.where(keep_s, g_s, 0.0)
    bdm = bd_ref[...]
    qn = q * lax.rsqrt(_dot3_left(q * q, bdm) + RMS_EPS) * (GDN_DK ** -0.5)
    kn = k * lax.rsqrt(_dot3_left(k * k, bdm) + RMS_EPS)
    gcs_s = _dot3_right(tril_ref[...], g_s)
    beta_e = _dot3_left(beta_s, expb_ref[...])
    yield
    gcs_e = _dot3_left(gcs_s, expg_ref[...])
    yield
    eg = jnp.exp(gcs_e)
    kb = kn * beta_e
    tile4 = lambda a: jnp.concatenate([a] * GDN_HEADS, axis=0)
    stack = lambda a: jnp.where(bd, tile4(a), 0.0)
    rl = lax.broadcasted_iota(jnp.int32, (L, W), 0)
    cl = lax.broadcasted_iota(jnp.int32, (L, W), 1)
    grow = jnp.sum(jnp.where((cl % L) == rl, gcs_e, 0.0), axis=0, keepdims=True)
    incl = bd & ((r2 % L) >= (c2 % L))
    decay = jnp.exp(jnp.where(incl, tile4(gcs_e) - grow, NEG_BIG))
    kns = stack(kn).astype(BF16)
    yield
    a_mat = jnp.where((r2 % L) != (c2 % L), _dot_nt(stack(kb).astype(BF16), kns) * decay, 0.0)
    qk = _dot_nt(stack(qn).astype(BF16), kns) * decay
    yield
    return dict(a=a_mat, qk=qk.astype(BF16), q_dec=(qn * eg).astype(BF16), kn=kn, gcs_e=gcs_e,
                rhs_v=_split2(stack(v * beta_e)), rhs_k=_split2(stack(kb * eg)))


def _gdn_inverses(a_mats):
    r2, c2, _ = _same_head_mask()
    eye = jnp.where(r2 == c2, 1.0, 0.0)
    ps = [eye - a for a in a_mats]
    a_parts = [_split2(a) for a in a_mats]
    x_parts = [_split2(_mm_split(ap, ap)) for ap in a_parts]
    ps = [p + _mm_split(_split2(p), xp) for p, xp in zip(ps, x_parts)]
    for _ in range(4):
        x_parts = [_split2(_mm_split(xp, xp)) for xp in x_parts]
        ps = [p + _mm_split(_split2(p), xp) for p, xp in zip(ps, x_parts)]
    return [_split2(p) for p in ps]


def _gdn_finish(prep, p_parts, z_ref, gnorm_ref, bd_ref, o_ref, s_ref):
    L = GDN_CHUNK
    bd = _same_head_mask()[2]
    stack = lambda a: jnp.where(bd, jnp.concatenate([a] * GDN_HEADS, axis=0), 0.0)
    collapse = lambda m: (m[0:L] + m[L:2 * L]) + (m[2 * L:3 * L] + m[3 * L:4 * L])
    u_c = collapse(_mm_split(p_parts, prep["rhs_v"]))
    w_c = collapse(_mm_split(p_parts, prep["rhs_k"]))
    yield
    s = s_ref[...]
    sb = s.astype(BF16)
    v_new = u_c - _dot(w_c.astype(BF16), sb)
    yield
    o = _dot(prep["q_dec"], sb) + collapse(_dot(prep["qk"], stack(v_new).astype(BF16)))
    gcs_e = prep["gcs_e"]
    glast = gcs_e[L - 1:L, :]
    k_dec = prep["kn"] * jnp.exp(glast - gcs_e)
    s_ref[...] = s * jnp.exp(glast) + jnp.where(bd, _dot_tn(k_dec.astype(BF16), v_new.astype(BF16)), 0.0)
    yield
    on = o * lax.rsqrt(_dot3_left(o * o, bd_ref[...]) * (1.0 / HEAD_DIM) + RMS_EPS) * gnorm_ref[...]
    o_ref[...] = on * _silu(z_ref[...])


def _gdn_body(qkv_ref, z_ref, small_ref, wconv_ref, alog_ref, dtb_ref, gnorm_ref, conv0_ref, s0_ref,
              tril_ref, expb_ref, expg_ref, bd_ref, fold_ref, foldt_ref,
              o_ref, convn_ref, sn_ref, xfull_ref, s_ref, *, n_chunks, tv_last):
    L = GDN_CHUNK
    c = pl.program_id(1)
    seqs = range(GDN_SEQS_PER_STEP)

    @pl.when(c == 0)
    def _():
        bd = _same_head_mask()[2]
        for j in seqs:
            xfull_ref[j, 0:CONV_PAD, :] = conv0_ref[j]
            s_ref[j] = jnp.where(bd, _dot3_left(s0_ref[j], foldt_ref[...]), 0.0)

    preps = _run_interleaved([
        _gdn_prepare(c, qkv_ref.at[j], small_ref.at[j], wconv_ref, alog_ref, dtb_ref,
                     tril_ref, expb_ref, expg_ref, bd_ref, xfull_ref.at[j], n_chunks=n_chunks, tv_last=tv_last)
        for j in seqs])
    inverses = _gdn_inverses([prep["a"] for prep in preps])
    _run_interleaved([_gdn_finish(preps[j], inverses[j], z_ref.at[j], gnorm_ref, bd_ref, o_ref.at[j], s_ref.at[j])
                      for j in seqs])

    @pl.when(c == n_chunks - 1)
    def _():
        for j in seqs:
            convn_ref[j] = xfull_ref[j, CONV_PAD + tv_last - (CONV_WIDTH - 1):CONV_PAD + tv_last, :]
            sn_ref[j] = _dot3_left(s_ref[j], fold_ref[...])

    for j in seqs:
        xfull_ref[j, 0:CONV_PAD, :] = xfull_ref[j, L:L + CONV_PAD, :]


def _gdn_call(u, n_seq, rows_per_seq, t_valid, conv0, s0, w_conv, a_log, dt_bias, g_norm):
    L = GDN_CHUNK
    G = GDN_SEQS_PER_STEP
    n_chunks = -(-t_valid // L)
    tv_last = t_valid - (n_chunks - 1) * L
    lane_row = lambda vals, off: jnp.zeros((1, LANE), F32).at[0, off:off + vals.shape[0]].set(vals)
    conv0_p = jnp.pad(conv0, ((0, 0), (CONV_PAD - (CONV_WIDTH - 1), 0), (0, 0)))
    u3 = u.reshape(n_seq, rows_per_seq, U_WIDTH)
    row = lambda w, col: pl.BlockSpec((G, L, w), lambda b, c: (b, c, col))
    full = lambda r, cc: pl.BlockSpec((r, cc), lambda b, c: (0, 0))
    per_seq = lambda r, cc: pl.BlockSpec((G, r, cc), lambda b, c: (b, 0, 0))
    out = pl.pallas_call(
        functools.partial(_gdn_body, n_chunks=n_chunks, tv_last=tv_last),
        grid=(n_seq // G, n_chunks),
        in_specs=[row(GDN_CONV_DIM, OFF_QKVB // GDN_CONV_DIM), row(GDN_WIDTH, OFF_ZB // GDN_WIDTH),
                  row(LANE, OFF_SMALL // LANE),
                  full(CONV_WIDTH, GDN_CONV_DIM), full(1, LANE), full(1, LANE), full(1, GDN_WIDTH),
                  per_seq(CONV_PAD, GDN_CONV_DIM), per_seq(GDN_WIDTH, HEAD_DIM),
                  full(L, L), full(LANE, GDN_WIDTH), full(LANE, GDN_WIDTH), full(GDN_WIDTH, GDN_WIDTH),
                  full(GDN_WIDTH, HEAD_DIM), full(HEAD_DIM, GDN_WIDTH)],
        out_specs=[pl.BlockSpec((G, L, GDN_WIDTH), lambda b, c: (b, c, 0)),
                   per_seq(CONV_WIDTH - 1, GDN_CONV_DIM), per_seq(GDN_WIDTH, HEAD_DIM)],
        out_shape=[jax.ShapeDtypeStruct((n_seq, n_chunks * L, GDN_WIDTH), F32),
                   jax.ShapeDtypeStruct((n_seq, CONV_WIDTH - 1, GDN_CONV_DIM), F32),
                   jax.ShapeDtypeStruct((n_seq, GDN_WIDTH, HEAD_DIM), F32)],
        scratch_shapes=[pltpu.VMEM((G, L + CONV_PAD, GDN_CONV_DIM), F32),
                        pltpu.VMEM((G, GDN_WIDTH, GDN_WIDTH), F32)],
        compiler_params=_params(("parallel", "arbitrary")), name="gdn",
    )(u3, u3, u3, w_conv, lane_row(a_log, SMALL_DECAY), lane_row(dt_bias, SMALL_DECAY),
      jnp.tile(g_norm, GDN_HEADS).reshape(1, GDN_WIDTH), conv0_p, s0.reshape(n_seq, GDN_WIDTH, HEAD_DIM),
      _tril_incl(L), _expand_matrix(SMALL_BETA, GDN_HEADS, HEAD_DIM), _expand_matrix(SMALL_DECAY, GDN_HEADS, HEAD_DIM),
      _block_diag_ones(GDN_WIDTH, HEAD_DIM), _fold_matrix(GDN_HEADS, HEAD_DIM), _fold_matrix(GDN_HEADS, HEAD_DIM).T)
    o, conv_new, s_new = out
    return o.reshape(n_seq * n_chunks * L, GDN_WIDTH), conv_new, s_new.reshape(n_seq, GDN_HEADS, GDN_DK, HEAD_DIM)


SSD_SEQS_PER_STEP = 4


def _ssd_chunk(c, xbc_ref, z_ref, small_ref, wconv_ref, bconv_ref, alog_ref, dtb_ref, dskip_ref, gnorm_ref,
               tril_ref, expc_ref, o_ref, xfull_ref, ht_ref, *, n_chunks, tv_last):
    L = SSD_CHUNK
    W = SSD_WIDTH
    GW = W // SSD_GROUPS
    xs = _silu(_chunk_conv(xbc_ref, xfull_ref, wconv_ref, L) + bconv_ref[...])
    x_c, bm, cm = xs[:, 0:W], xs[:, W:W + GW], xs[:, W + GW:W + 2 * GW]
    yield
    small = small_ref[...]
    lane = lax.broadcasted_iota(jnp.int32, (L, LANE), 1)
    dt_s = jnp.where((lane >= SMALL_DT) & (lane < SMALL_DT + SSD_HEADS), _softplus(small + dtb_ref[...]), 0.0)
    if tv_last < L:
        last_rows = jnp.where(c == n_chunks - 1, tv_last, L)
        keep_s = lax.broadcasted_iota(jnp.int32, (L, LANE), 0) < last_rows
        keep_g = lax.broadcasted_iota(jnp.int32, (L, GW), 0) < last_rows
        dt_s = jnp.where(keep_s, dt_s, 0.0)
        bm, cm = jnp.where(keep_g, bm, 0.0), jnp.where(keep_g, cm, 0.0)
    ac_s = dt_s * (-jnp.exp(alog_ref[...]))
    acs_s = _dot3_right(tril_ref[...], ac_s)
    dt_e = _dot3_left(dt_s, expc_ref[...])
    yield
    acs_e = _dot3_left(acs_s, expc_ref[...])
    yield
    xdt = x_c * dt_e
    e = jnp.exp(acs_e)
    last = acs_e[L - 1:L, :]
    xw = xdt * jnp.exp(last - acs_e)
    dchunk = jnp.exp(last)
    acs_t = acs_s.T
    incl = lax.broadcasted_iota(jnp.int32, (L, L), 0) >= lax.broadcasted_iota(jnp.int32, (L, L), 1)
    first_half = lax.broadcasted_iota(jnp.int32, (L, LANE), 1) < HEAD_DIM
    ht = ht_ref[...]
    ys = []
    for g in range(SSD_GROUPS):
        gl = slice(g * GW, (g + 1) * GW)
        bg = bm[:, g * SSD_STATE:(g + 1) * SSD_STATE].astype(BF16)
        cg = cm[:, g * SSD_STATE:(g + 1) * SSD_STATE].astype(BF16)
        scores = _dot_nt(cg, bg)
        y_off = _dot(cg, ht[:, gl].astype(BF16)) * e[:, gl]
        ht_ref[:, gl] = ht[:, gl] * dchunk[:, gl] + _dot_tn(bg, xw[:, gl].astype(BF16))
        yield
        pairs = []
        for pp in range(GW // LANE):
            xp = xdt[:, g * GW + pp * LANE:g * GW + (pp + 1) * LANE]
            yp = jnp.zeros((L, LANE), F32)
            for hh in range(2):
                h = g * (SSD_HEADS // SSD_GROUPS) + pp * 2 + hh
                col = acs_s[:, SMALL_DT + h:SMALL_DT + h + 1]
                rowv = acs_t[SMALL_DT + h:SMALL_DT + h + 1, :]
                decay = jnp.exp(jnp.where(incl, col - rowv, NEG_BIG))
                xm = jnp.where(first_half if hh == 0 else jnp.logical_not(first_half), xp, 0.0)
                yp = yp + _dot((scores * decay).astype(BF16), xm.astype(BF16))
                yield
            pairs.append(yp)
        ys.append(jnp.concatenate(pairs, axis=1) + y_off)
    y = (jnp.concatenate(ys, axis=1) + dskip_ref[...] * x_c) * _silu(z_ref[...])
    outs = []
    for g in range(SSD_GROUPS):
        yg = y[:, g * GW:(g + 1) * GW]
        outs.append(yg * lax.rsqrt(jnp.mean(yg * yg, axis=-1, keepdims=True) + RMS_EPS))
    o_ref[...] = jnp.concatenate(outs, axis=1) * gnorm_ref[...]


def _ssd_body(xbc_ref, z_ref, small_ref, wconv_ref, bconv_ref, alog_ref, dtb_ref, dskip_ref, gnorm_ref,
              conv0_ref, h0_ref, tril_ref, expc_ref,
              o_ref, convn_ref, hn_ref, xfull_ref, ht_ref, *, n_chunks, tv_last):
    L = SSD_CHUNK
    c = pl.program_id(1)
    seqs = range(SSD_SEQS_PER_STEP)
    lane_blocks = range(SSD_WIDTH // LANE)

    @pl.when(c == 0)
    def _():
        for j in seqs:
            xfull_ref[j, 0:CONV_PAD, :] = conv0_ref[j]
            for i in lane_blocks:
                ht_ref[j, :, i * LANE:(i + 1) * LANE] = h0_ref[j, i * LANE:(i + 1) * LANE, :].T

    _run_interleaved([
        _ssd_chunk(c, xbc_ref.at[j], z_ref.at[j], small_ref.at[j], wconv_ref, bconv_ref, alog_ref, dtb_ref, dskip_ref,
                   gnorm_ref, tril_ref, expc_ref, o_ref.at[j], xfull_ref.at[j], ht_ref.at[j],
                   n_chunks=n_chunks, tv_last=tv_last)
        for j in seqs])

    @pl.when(c == n_chunks - 1)
    def _():
        for j in seqs:
            convn_ref[j] = xfull_ref[j, CONV_PAD + tv_last - (CONV_WIDTH - 1):CONV_PAD + tv_last, :]
            for i in lane_blocks:
                hn_ref[j, i * LANE:(i + 1) * LANE, :] = ht_ref[j, :, i * LANE:(i + 1) * LANE].T

    for j in seqs:
        xfull_ref[j, 0:CONV_PAD, :] = xfull_ref[j, L:L + CONV_PAD, :]


def _ssd_call(u, n_seq, rows_per_seq, t_valid, conv0, h0, w_conv, b_conv, a_log, dt_bias, d_skip, g_norm):
    L = SSD_CHUNK
    G = SSD_SEQS_PER_STEP
    n_chunks = -(-t_valid // L)
    tv_last = t_valid - (n_chunks - 1) * L
    lane_row = lambda vals, off: jnp.zeros((1, LANE), F32).at[0, off:off + vals.shape[0]].set(vals)
    conv0_p = jnp.pad(conv0, ((0, 0), (CONV_PAD - (CONV_WIDTH - 1), 0), (0, 0)))
    u3 = u.reshape(n_seq, rows_per_seq, U_WIDTH)
    row = lambda w, col: pl.BlockSpec((G, L, w), lambda b, c: (b, c, col))
    full = lambda r, cc: pl.BlockSpec((r, cc), lambda b, c: (0, 0))
    per_seq = lambda r, cc: pl.BlockSpec((G, r, cc), lambda b, c: (b, 0, 0))
    out = pl.pallas_call(
        functools.partial(_ssd_body, n_chunks=n_chunks, tv_last=tv_last),
        grid=(n_seq // G, n_chunks),
        in_specs=[row(SSD_XBC, OFF_XBC // SSD_XBC), row(SSD_WIDTH, OFF_ZC // SSD_WIDTH), row(LANE, OFF_SMALL // LANE),
                  full(CONV_WIDTH, SSD_XBC), full(1, SSD_XBC), full(1, LANE), full(1, LANE),
                  full(1, SSD_WIDTH), full(1, SSD_WIDTH),
                  per_seq(CONV_PAD, SSD_XBC), per_seq(SSD_WIDTH, SSD_STATE),
                  full(L, L), full(LANE, SSD_WIDTH)],
        out_specs=[pl.BlockSpec((G, L, SSD_WIDTH), lambda b, c: (b, c, 0)),
                   per_seq(CONV_WIDTH - 1, SSD_XBC), per_seq(SSD_WIDTH, SSD_STATE)],
        out_shape=[jax.ShapeDtypeStruct((n_seq, n_chunks * L, SSD_WIDTH), F32),
                   jax.ShapeDtypeStruct((n_seq, CONV_WIDTH - 1, SSD_XBC), F32),
                   jax.ShapeDtypeStruct((n_seq, SSD_WIDTH, SSD_STATE), F32)],
        scratch_shapes=[pltpu.VMEM((G, L + CONV_PAD, SSD_XBC), F32), pltpu.VMEM((G, SSD_STATE, SSD_WIDTH), F32)],
        compiler_params=_params(("parallel", "arbitrary")), name="ssd",
    )(u3, u3, u3, w_conv, b_conv.reshape(1, SSD_XBC), lane_row(a_log, SMALL_DT), lane_row(dt_bias, SMALL_DT),
      jnp.repeat(d_skip, HEAD_DIM).reshape(1, SSD_WIDTH), g_norm.reshape(1, SSD_WIDTH),
      conv0_p, h0.reshape(n_seq, SSD_WIDTH, SSD_STATE), _tril_incl(L), _expand_matrix(SMALL_DT, SSD_HEADS, HEAD_DIM))
    o, conv_new, h_new = out
    return (o.reshape(n_seq * n_chunks * L, SSD_WIDTH), conv_new,
            h_new.reshape(n_seq, SSD_HEADS, HEAD_DIM, SSD_STATE))


ROUTE_GATE, ROUTE_IDX, ROUTE_RANK = 0, TOP_K, 2 * TOP_K


def _strict_lower(n):
    return jnp.asarray(np.tril(np.ones((n, n), bool), -1), BF16)


def _proj_out_body(x_ref, oa_ref, ob_ref, oc_ref, w_ref, g_ref, b_ref, rw_ref, rb_ref, lower_ref,
                   x1_ref, route_ref, count_ref):
    mix = _dot(oa_ref[...].astype(BF16), w_ref[0:256, :])
    mix += _dot(ob_ref[...].astype(BF16), w_ref[256:512, :])
    mix += _dot(oc_ref[...].astype(BF16), w_ref[512:1024, :])
    x1 = _layer_norm_f32(DEEPNORM_ALPHA * x_ref[...] + mix, g_ref[...], b_ref[...])
    x1_ref[...] = x1
    logits = _mm3(x1, rw_ref[...]) + rb_ref[...]

    @pl.when(pl.program_id(0) == 0)
    def _():
        count_ref[...] = jnp.zeros_like(count_ref)

    lane = lax.broadcasted_iota(jnp.int32, logits.shape, 1)
    lg = jnp.where(lane < N_EXPERTS, logits, -jnp.inf)
    vals, idxs = [], []
    for _ in range(TOP_K):
        m = jnp.max(lg, axis=1, keepdims=True)
        idx = jnp.min(jnp.where(lg == m, lane, LANE), axis=1, keepdims=True)
        vals.append(m)
        idxs.append(idx)
        lg = jnp.where(lane == idx, -jnp.inf, lg)
    exps = [jnp.exp(v - vals[0]) for v in vals]
    denom = exps[0] + exps[1] + exps[2] + exps[3]
    hot = [lane == idx for idx in idxs]
    multi = jnp.where(hot[0] | hot[1] | hot[2] | hot[3], 1.0, 0.0)
    before = _dot(lower_ref[...], multi.astype(BF16)) + count_ref[...]
    route = jnp.zeros(logits.shape, F32)
    for k in range(TOP_K):
        rank = jnp.sum(jnp.where(hot[k], before, 0.0), axis=1, keepdims=True)
        route = jnp.where(lane == ROUTE_GATE + k, exps[k] / denom, route)
        route = jnp.where(lane == ROUTE_IDX + k, idxs[k].astype(F32), route)
        route = jnp.where(lane == ROUTE_RANK + k, rank, route)
    route_ref[...] = route
    count_ref[...] += jnp.sum(multi, axis=0, keepdims=True)


def _proj_out_call(x, o_a, o_b, o_c, w_bf16, g, b, router_w, router_b):
    n, d = x.shape
    bm = _row_block(n)
    rw = jnp.pad(router_w, ((0, 0), (0, LANE - N_EXPERTS)))
    rb = jnp.pad(router_b, (0, LANE - N_EXPERTS)).reshape(1, LANE)
    row = lambda w: pl.BlockSpec((bm, w), lambda i: (i, 0))
    full = lambda r, c: pl.BlockSpec((r, c), lambda i: (0, 0))
    return pl.pallas_call(
        _proj_out_body, grid=(n // bm,),
        in_specs=[row(d), row(SB_WIDTH), row(GDN_WIDTH), row(SSD_WIDTH), full(d, d), full(1, d), full(1, d),
                  full(d, LANE), full(1, LANE), full(bm, bm)],
        out_specs=[row(d), row(LANE), full(1, LANE)],
        out_shape=[jax.ShapeDtypeStruct((n, d), F32), jax.ShapeDtypeStruct((n, LANE), F32),
                   jax.ShapeDtypeStruct((1, LANE), F32)],
        compiler_params=_params(("arbitrary",)), name="proj_out",
    )(x, o_a, o_b, o_c, w_bf16, g.reshape(1, d), b.reshape(1, d), rw, rb, _strict_lower(bm))


def _moe_body(be_ref, nb_ref, rows_ref, wg_ref, bg_ref, wu_ref, bu_ref, wd_ref, bd_ref, o_ref, wg_s, wu_s, wd_s):
    i = pl.program_id(0)
    prev = be_ref[jnp.maximum(i - 1, 0)]

    @pl.when(jnp.logical_or(i == 0, be_ref[i] != prev))
    def _():
        wg_s[...] = wg_ref[...].astype(BF16)
        wu_s[...] = wu_ref[...].astype(BF16)
        wd_s[...] = wd_ref[...].astype(BF16)

    @pl.when(i < nb_ref[0])
    def _():
        xb = rows_ref[...].astype(BF16)
        glu = jnp.minimum(_dot(xb, wg_s[...]) + bg_ref[...], SWIGLU_LIMIT)
        lin = jnp.clip(_dot(xb, wu_s[...]) + bu_ref[...], -SWIGLU_LIMIT, SWIGLU_LIMIT)
        act = glu * jax.nn.sigmoid(SWIGLU_ALPHA * glu) * (lin + 1.0)
        o_ref[...] = _dot(act.astype(BF16), wd_s[...]) + bd_ref[...]

    @pl.when(i >= nb_ref[0])
    def _():
        o_ref[...] = jnp.zeros_like(o_ref)


def _moe_call(rows_in, block_expert, n_used, w_gate, b_gate, w_up, b_up, w_down, b_down, layer, blk):
    n_rows, d = rows_in.shape
    n_blocks = n_rows // blk
    wspec = lambda r, c: pl.BlockSpec((None, None, r, c), lambda i, be, nb: (layer, be[i], 0, 0))
    grid_spec = pltpu.PrefetchScalarGridSpec(
        num_scalar_prefetch=2, grid=(n_blocks,),
        in_specs=[pl.BlockSpec((blk, d), lambda i, be, nb: (i, 0)),
                  wspec(d, D_EXPERT), wspec(1, D_EXPERT), wspec(d, D_EXPERT), wspec(1, D_EXPERT),
                  wspec(D_EXPERT, d), wspec(1, d)],
        out_specs=pl.BlockSpec((blk, d), lambda i, be, nb: (i, 0)),
        scratch_shapes=[pltpu.VMEM((d, D_EXPERT), BF16), pltpu.VMEM((d, D_EXPERT), BF16),
                        pltpu.VMEM((D_EXPERT, d), BF16)])
    depth = w_gate.shape[0]
    return pl.pallas_call(
        _moe_body, grid_spec=grid_spec,
        out_shape=jax.ShapeDtypeStruct((n_rows, d), F32),
        compiler_params=_params(("arbitrary",)), name="moe_experts",
    )(block_expert, n_used, rows_in, w_gate, b_gate.reshape(depth, N_EXPERTS, 1, D_EXPERT),
      w_up, b_up.reshape(depth, N_EXPERTS, 1, D_EXPERT), w_down, b_down.reshape(depth, N_EXPERTS, 1, d))


def _route(route, counts, blk):
    n_tok = route.shape[0]
    gate = route[:, ROUTE_GATE:ROUTE_GATE + TOP_K]
    expert = route[:, ROUTE_IDX:ROUTE_IDX + TOP_K].astype(jnp.int32)
    rank = route[:, ROUTE_RANK:ROUTE_RANK + TOP_K].astype(jnp.int32)
    counts = counts[0, :N_EXPERTS].astype(jnp.int32)
    n_assign = n_tok * TOP_K
    n_blocks = -(-(n_assign + N_EXPERTS * (blk - 1)) // blk)
    padded = (counts + blk - 1) // blk * blk
    padded_end = jnp.cumsum(padded)
    padded_start = padded_end - padded
    start = jnp.cumsum(counts) - counts
    dest = padded_start[expert] + rank
    token = jnp.broadcast_to(jnp.arange(n_tok, dtype=jnp.int32)[:, None], (n_tok, TOP_K))
    block_first = jnp.arange(n_blocks, dtype=jnp.int32) * blk
    block_expert = jnp.minimum(jnp.sum(padded_end[None, :] <= block_first[:, None], axis=1, dtype=jnp.int32),
                               N_EXPERTS - 1)
    r = jnp.arange(n_blocks * blk, dtype=jnp.int32)
    if n_assign <= SMALL_ROUTE_ASSIGNMENTS:
        hit = dest.reshape(1, -1) == r[:, None]
        src = jnp.sum(jnp.where(hit, token.reshape(1, -1), 0), axis=1, dtype=jnp.int32)
    else:
        _, token_sorted = lax.sort((dest.reshape(-1), token.reshape(-1)), num_keys=1)
        e_r = jnp.repeat(block_expert, blk)
        src = token_sorted[jnp.clip(start[e_r] + r - padded_start[e_r], 0, n_assign - 1)]
    dest_t = dest.T.reshape(-1)
    n_used = (padded_end[-1] // blk).astype(jnp.int32).reshape(1)
    return gate, src, dest_t, block_expert, n_used


def _post_body(x1_ref, r0_ref, r1_ref, r2_ref, r3_ref, gate_ref, p_ref, g_ref, b_ref, wg_ref, bg_ref, wp_ref, o_ref):
    gate = gate_ref[...]
    ffn = gate[:, 0:1] * r0_ref[...]
    for k, r_ref in ((1, r1_ref), (2, r2_ref), (3, r3_ref)):
        ffn += gate[:, k:k + 1] * r_ref[...]
    x2 = _layer_norm_f32(DEEPNORM_ALPHA * x1_ref[...] + ffn, g_ref[...], b_ref[...])
    gate_logit = _dot(x2.astype(BF16), wg_ref[...]) + bg_ref[...]
    emb = _dot(p_ref[...].astype(BF16), wp_ref[...])
    o_ref[...] = x2 + jax.nn.sigmoid(gate_logit) * emb


def _post_call(x1, rows_k, gate, p, g, b, wg_bf16, bg, wp_bf16):
    n, d = x1.shape
    bm = _row_block(n)
    nb = n // bm
    row = lambda w: pl.BlockSpec((bm, w), lambda i: (i, 0))
    plane = lambda k: pl.BlockSpec((bm, d), lambda i: (k * nb + i, 0))
    full = lambda r, c: pl.BlockSpec((r, c), lambda i: (0, 0))
    return pl.pallas_call(
        _post_body, grid=(nb,),
        in_specs=[row(d)] + [plane(k) for k in range(TOP_K)] + [row(TOP_K), row(PLE_DIM), full(1, d), full(1, d),
                                                                 full(d, d), full(1, d), full(PLE_DIM, d)],
        out_specs=row(d),
        out_shape=jax.ShapeDtypeStruct((n, d), F32),
        compiler_params=_params(("parallel",)), name="post_ffn",
    )(x1, rows_k, rows_k, rows_k, rows_k, gate, p, g.reshape(1, d), b.reshape(1, d), wg_bf16, bg.reshape(1, d),
      wp_bf16)


def kernel(x_prompt, x_sample, cache_k_a, cache_v_a, state_conv_b, state_delta_b, state_conv_c, state_ssm_c, page_table, p_prompt, p_sample, ln_in_g, ln_in_b, w_in, sb_bias, w_conv_b, a_log_b, dt_bias_b, g_norm_b, w_conv_c, b_conv_c, a_log_c, dt_bias_c, d_skip_c, g_norm_c, w_out, ln1_g, ln1_b, router_w, router_b, w_gate, b_gate, w_up, b_up, w_down, b_down, ln2_g, ln2_b, ple_gate_w, ple_gate_b, ple_proj_w):
    depth = w_in.shape[0]
    pad_cols = U_WIDTH - sum(hi - lo for lo, hi in _IN_SEGMENTS)
    w_in_r = jnp.concatenate([w_in[:, :, lo:hi] for lo, hi in _IN_SEGMENTS]
                             + [jnp.zeros(w_in.shape[:2] + (pad_cols,), w_in.dtype)], axis=-1).astype(BF16)
    w_out_b = w_out.astype(BF16)
    ple_gate_w_b = ple_gate_w.astype(BF16)
    ple_proj_w_b = ple_proj_w.astype(BF16)
    n_pool = cache_k_a.shape[1]
    cache_k = cache_k_a.transpose(0, 1, 3, 4, 2).reshape(depth, n_pool, SB_WIDTH, PAGE_SIZE)
    cache_v = cache_v_a.transpose(0, 1, 3, 4, 2).reshape(depth, n_pool, SB_WIDTH, PAGE_SIZE)

    def mixers_and_router(i, x, bsz, t, sample):
        n = bsz * t
        u = _proj_in_call(x, w_in_r[i])
        k_a = u[:, OFF_K:OFF_K + SB_WIDTH].reshape(bsz, t, SB_HEADS, HEAD_DIM)
        v_a = u[:, OFF_V:OFF_V + SB_WIDTH].reshape(bsz, t, SB_HEADS, HEAD_DIM)
        if sample:
            o_a = _sb_decode_call(u[:, OFF_Q:OFF_Q + SB_WIDTH], sb_bias[i], cache_k, cache_v, page_table, i)
            conv_b, delta_b, conv_c, ssm_c = state_conv_b[i], state_delta_b[i], state_conv_c[i], state_ssm_c[i]
            rows_per_seq = SSD_CHUNK
            u_mix = jnp.pad(u.reshape(bsz, t, U_WIDTH), ((0, 0), (0, rows_per_seq - t), (0, 0)))
            u_mix = u_mix.reshape(bsz * rows_per_seq, U_WIDTH)
        else:
            o_a = _sb_prefill_call(u, sb_bias[i], bsz, t)
            conv_b = jnp.zeros((bsz, CONV_WIDTH - 1, GDN_CONV_DIM), F32)
            delta_b = jnp.zeros((bsz, GDN_HEADS, GDN_DK, HEAD_DIM), F32)
            conv_c = jnp.zeros((bsz, CONV_WIDTH - 1, SSD_XBC), F32)
            ssm_c = jnp.zeros((bsz, SSD_HEADS, HEAD_DIM, SSD_STATE), F32)
            rows_per_seq = t
            u_mix = u
        o_b, conv_b_new, delta_new = _gdn_call(u_mix, bsz, rows_per_seq, t, conv_b, delta_b, w_conv_b[i],
                                               a_log_b[i], dt_bias_b[i], g_norm_b[i])
        o_c, conv_c_new, ssm_new = _ssd_call(u_mix, bsz, rows_per_seq, t, conv_c, ssm_c, w_conv_c[i], b_conv_c[i],
                                             a_log_c[i], dt_bias_c[i], d_skip_c[i], g_norm_c[i])
        if sample:
            o_b = o_b.reshape(bsz, -1, GDN_WIDTH)[:, :t].reshape(n, GDN_WIDTH)
            o_c = o_c.reshape(bsz, -1, SSD_WIDTH)[:, :t].reshape(n, SSD_WIDTH)
        x1, route, counts = _proj_out_call(x, o_a, o_b, o_c, w_out_b[i], ln1_g[i], ln1_b[i], router_w[i], router_b[i])
        blk = MOE_BLOCK_PROMPT if n * TOP_K >= MOE_BLOCK_PROMPT * N_EXPERTS else MOE_BLOCK_SMALL
        gate, src, dest_t, block_expert, n_used = _route(route, counts, blk)
        return (x1, gate, dest_t, block_expert, n_used, blk, x1[src]), (k_a, v_a, conv_b_new, delta_new, conv_c_new,
                                                                        ssm_new)

    def experts(i, routed):
        x1, gate, dest_t, block_expert, n_used, blk, rows_in = routed
        rows_out = _moe_call(rows_in, block_expert, n_used, w_gate, b_gate, w_up, b_up, w_down, b_down, i, blk)
        return rows_out[dest_t]

    def combine(i, routed, rows_k, p):
        x1, gate = routed[0], routed[1]
        return _post_call(x1, rows_k, gate, p[i].reshape(x1.shape[0], PLE_DIM), ln2_g[i], ln2_b[i], ple_gate_w_b[i],
                          ple_gate_b[i], ple_proj_w_b[i])

    groups = [(x_prompt, p_prompt, False), (x_sample, p_sample, True)]
    xs = [_ln_call(x.reshape(-1, x.shape[-1]), ln_in_g, ln_in_b) for x, _, _ in groups]
    states = [[], []]
    for i in range(depth):
        routed = []
        for gi, (x, _, sample) in enumerate(groups):
            r, st = mixers_and_router(i, xs[gi], x.shape[0], x.shape[1], sample)
            routed.append(r)
            states[gi].append(st)
        rows_k = [experts(i, r) for r in routed]
        xs = [combine(i, routed[gi], rows_k[gi], groups[gi][1]) for gi in range(len(groups))]
    y_prompt, y_sample = (xs[gi].reshape(groups[gi][0].shape) for gi in range(len(groups)))
    new_prompt, new_sample = ([jnp.stack([s[j] for s in st]) for j in range(6)] for st in states)
    return (y_prompt, y_sample, *new_prompt, *new_sample)
```

```python
import functools

import jax
import jax.numpy as jnp
import numpy as np
from jax import lax
from jax.experimental import pallas as pl
from jax.experimental.pallas import tpu as pltpu

F32 = jnp.float32
BF16 = jnp.bfloat16

D_MODEL = 1024
DEPTH = 4
PAGE_SIZE = 128
HEAD_DIM = 64
CONV_WIDTH = 4
SB_WIDTH = 256
SB_HEADS = 4
GDN_WIDTH = 256
GDN_HEADS = 4
GDN_DK = 64
GDN_CONV_DIM = 768
GDN_CHUNK = 64
SSD_WIDTH = 512
SSD_HEADS = 8
SSD_GROUPS = 2
SSD_STATE = 128
SSD_XBC = 1024
SSD_CHUNK = 128
N_EXPERTS = 32
TOP_K = 4
D_EXPERT = 512
SWIGLU_ALPHA = 1.702
SWIGLU_LIMIT = 7.0
PLE_DIM = 256
DEEPNORM_ALPHA = (2 * DEPTH) ** 0.25
LN_EPS = 1e-5
RMS_EPS = 1e-6

_REF_WIDTHS = (256, 256, 256, 768, 256, 4, 4, 512, 1024, 8)
_REF_OFF = tuple(int(v) for v in np.cumsum((0,) + _REF_WIDTHS))
OFF_XBC, OFF_ZC, OFF_QKVB, OFF_Q, OFF_K, OFF_V, OFF_ZB, OFF_SMALL = 0, 1024, 1536, 2304, 2560, 2816, 3072, 3328
U_WIDTH = 3456
LANE = 128
SMALL_BETA, SMALL_DECAY, SMALL_DT = 0, GDN_HEADS, 2 * GDN_HEADS
_IN_SEGMENTS = (
    (_REF_OFF[8], _REF_OFF[9]),
    (_REF_OFF[7], _REF_OFF[8]),
    (_REF_OFF[3], _REF_OFF[4]),
    (_REF_OFF[0], _REF_OFF[3]),
    (_REF_OFF[4], _REF_OFF[5]),
    (_REF_OFF[5], _REF_OFF[7]),
    (_REF_OFF[9], _REF_OFF[10]),
)

VMEM_LIMIT = 48 * 1024 * 1024
MOE_BLOCK_PROMPT = 512
MOE_BLOCK_SMALL = 8
SMALL_ROUTE_ASSIGNMENTS = 1024
ROW_ALIGN = 8
SB_BLOCK = 256
DEC_PAGES_PER_STEP = 16
CONV_PAD = 8
NEG_BIG = -1e30
LOG2E = 1.4426950408889634


def _params(sem):
    return pltpu.CompilerParams(dimension_semantics=sem, vmem_limit_bytes=VMEM_LIMIT)


def _row_block(n):
    return n if n < 512 else 512


def _layer_norm_f32(x, g, b):
    mu = jnp.mean(x, axis=-1, keepdims=True)
    xc = x - mu
    var = jnp.mean(xc * xc, axis=-1, keepdims=True)
    return xc * lax.rsqrt(var + LN_EPS) * g + b


def _softplus(z):
    return jnp.maximum(z, 0.0) + jnp.log1p(jnp.exp(-jnp.abs(z)))


def _softplus2(z2):
    return jnp.maximum(z2, 0.0) + jnp.log2(1.0 + jnp.exp2(-jnp.abs(z2)))


def _silu(x):
    return x * jax.nn.sigmoid(x)


def _dot(a, b):
    return jnp.dot(a, b, preferred_element_type=F32)


def _dot_nt(a, b):
    return lax.dot_general(a, b, (((1,), (1,)), ((), ())), preferred_element_type=F32)


def _dot_tn(a, b):
    return lax.dot_general(a, b, (((0,), (0,)), ((), ())), preferred_element_type=F32)


def _split2(x):
    hi = x.astype(BF16)
    return hi, (x - hi.astype(F32)).astype(BF16)


def _split3(x):
    hi = x.astype(BF16)
    r = x - hi.astype(F32)
    mid = r.astype(BF16)
    return hi, mid, (r - mid.astype(F32)).astype(BF16)


def _split_dot(x, w_bf16):
    hi, lo = _split2(x)
    return _dot(hi, w_bf16) + _dot(lo, w_bf16)


def _dot3_left(x, w_bf16):
    hi, mid, lo = _split3(x)
    return _dot(hi, w_bf16) + (_dot(mid, w_bf16) + _dot(lo, w_bf16))


def _dot3_right(w_bf16, x):
    hi, mid, lo = _split3(x)
    return _dot(w_bf16, hi) + (_dot(w_bf16, mid) + _dot(w_bf16, lo))


def _mm3(a, b):
    ah, al = _split2(a)
    bh, bl = _split2(b)
    return _dot(ah, bh) + (_dot(ah, bl) + _dot(al, bh))


def _cumsum_matrix(n):
    j = np.arange(n)[:, None]
    s = np.arange(n)[None, :]
    return jnp.asarray(np.concatenate([(j > s), np.ones((n, LANE), bool)], axis=1), BF16)


def _tril_incl(n):
    return jnp.asarray(np.tril(np.ones((n, n), bool)), BF16)


def _expand_matrix(first_lane, heads, width):
    m = np.zeros((LANE, heads * width), bool)
    for h in range(heads):
        m[first_lane + h, h * width:(h + 1) * width] = True
    return jnp.asarray(m, BF16)


def _block_diag_ones(n, blk):
    i = np.arange(n)
    return jnp.asarray((i[:, None] // blk) == (i[None, :] // blk), BF16)


def _fold_matrix(heads, width):
    i = np.arange(heads * width)
    return jnp.asarray((i[:, None] % width) == np.arange(width)[None, :], BF16)


def _ln_body(x_ref, g_ref, b_ref, o_ref):
    o_ref[...] = _layer_norm_f32(x_ref[...], g_ref[...], b_ref[...])


def _ln_call(x, g, b):
    n, d = x.shape
    bm = _row_block(n)
    return pl.pallas_call(
        _ln_body, grid=(n // bm,),
        in_specs=[pl.BlockSpec((bm, d), lambda i: (i, 0)),
                  pl.BlockSpec((1, d), lambda i: (0, 0)), pl.BlockSpec((1, d), lambda i: (0, 0))],
        out_specs=pl.BlockSpec((bm, d), lambda i: (i, 0)),
        out_shape=jax.ShapeDtypeStruct((n, d), F32),
        compiler_params=_params(("parallel",)), name="ln_in",
    )(x, g.reshape(1, d), b.reshape(1, d))


_PROJ_CHUNK = 384


def _proj_in_body(x_ref, w_ref, o_ref):
    xb = x_ref[...].astype(BF16)
    for j in range(0, U_WIDTH, _PROJ_CHUNK):
        o_ref[:, j:j + _PROJ_CHUNK] = _dot(xb, w_ref[:, j:j + _PROJ_CHUNK])


def _proj_in_call(x, w_bf16):
    n, d = x.shape
    bm = _row_block(n)
    return pl.pallas_call(
        _proj_in_body, grid=(n // bm,),
        in_specs=[pl.BlockSpec((bm, d), lambda i: (i, 0)),
                  pl.BlockSpec((d, U_WIDTH), lambda i: (0, 0))],
        out_specs=pl.BlockSpec((bm, U_WIDTH), lambda i: (i, 0)),
        out_shape=jax.ShapeDtypeStruct((n, U_WIDTH), F32),
        compiler_params=_params(("parallel",)), name="proj_in",
    )(x, w_bf16)


def _sb_prefill_body(bias_ref, q_ref, k_ref, v_ref, u_ref, o_ref, qh_ref, acc_ref, carry_ref):
    blk = q_ref.shape[0]
    i = pl.program_id(1)
    q = q_ref[...] * (HEAD_DIM ** -0.5 * LOG2E)
    lane_head = lax.broadcasted_iota(jnp.int32, (1, SB_WIDTH), 1) // HEAD_DIM
    for h in range(SB_HEADS):
        qh_ref[h] = jnp.where(lane_head == h, q, 0.0).astype(BF16)
    acc_ref[...] = jnp.zeros_like(acc_ref)
    carry_ref[...] = jnp.zeros_like(carry_ref)
    rows = lax.broadcasted_iota(jnp.int32, (blk, blk), 0)
    cols = lax.broadcasted_iota(jnp.int32, (blk, blk), 1)

    def key_block(j, diagonal):
        start = pl.multiple_of(j * blk, blk)
        kj = k_ref[pl.ds(start, blk), :].astype(BF16)
        vj = v_ref[pl.ds(start, blk), :].astype(BF16)
        heads = range(SB_HEADS)
        visible = cols < rows
        zs = [_dot_nt(qh_ref[h], kj) + bias_ref[h] for h in heads]
        log_keeps = [-_softplus2(z) for z in zs]
        logits = [z + lk for z, lk in zip(zs, log_keeps)]
        if diagonal:
            log_keeps = [jnp.where(visible, lk, 0.0) for lk in log_keeps]
        lts = [_split_dot(lk, u_ref[...]) for lk in log_keeps]
        carries = [carry_ref[h] for h in heads]
        ws = [jnp.exp2(logits[h] + (lts[h][:, :blk] + jnp.concatenate([carries[h]] * (blk // LANE), axis=1)))
              for h in heads]
        if diagonal:
            ws = [jnp.where(visible, w, 0.0) for w in ws]
        for h in heads:
            acc_ref[h] += _dot(ws[h].astype(BF16), vj)
            carry_ref[h] = carries[h] + lts[h][:, blk:]

    key_block(i, True)

    def earlier(t, carry):
        key_block(i - 1 - t, False)
        return carry

    lax.fori_loop(0, i, earlier, 0)
    out = jnp.zeros((blk, SB_WIDTH), F32)
    for h in range(SB_HEADS):
        out = out + jnp.where(lane_head == h, acc_ref[h], 0.0)
    o_ref[...] = out


def _sb_prefill_call(u, bias, bsz, t):
    blk = SB_BLOCK
    nq = t // blk
    grid_spec = pltpu.PrefetchScalarGridSpec(
        num_scalar_prefetch=0, grid=(bsz, nq),
        in_specs=[pl.BlockSpec(memory_space=pltpu.SMEM),
                  pl.BlockSpec((blk, SB_WIDTH), lambda b, i: (b * nq + i, OFF_Q // SB_WIDTH)),
                  pl.BlockSpec((t, SB_WIDTH), lambda b, i: (b, OFF_K // SB_WIDTH)),
                  pl.BlockSpec((t, SB_WIDTH), lambda b, i: (b, OFF_V // SB_WIDTH)),
                  pl.BlockSpec((blk, blk + LANE), lambda b, i: (0, 0))],
        out_specs=pl.BlockSpec((blk, SB_WIDTH), lambda b, i: (b * nq + i, 0)),
        scratch_shapes=[pltpu.VMEM((SB_HEADS, blk, SB_WIDTH), BF16),
                        pltpu.VMEM((SB_HEADS, blk, SB_WIDTH), F32),
                        pltpu.VMEM((SB_HEADS, blk, LANE), F32)])
    return pl.pallas_call(
        _sb_prefill_body, grid_spec=grid_spec,
        out_shape=jax.ShapeDtypeStruct((bsz * t, SB_WIDTH), F32),
        compiler_params=_params(("parallel", "arbitrary")), name="sb_prefill",
    )(bias * LOG2E, u, u, u, _cumsum_matrix(blk))


def _sb_decode_body(pt_ref, q_ref, bias_ref, u_ref, *refs):
    g = DEC_PAGES_PER_STEP
    k_refs, v_refs = refs[:g], refs[g:2 * g]
    o_ref, acc_ref, carry_ref = refs[2 * g], refs[2 * g + 1], refs[2 * g + 2]
    s = pl.program_id(1)

    @pl.when(s == 0)
    def _():
        acc_ref[...] = jnp.zeros_like(acc_ref)
        carry_ref[...] = jnp.zeros_like(carry_ref)

    row = lax.broadcasted_iota(jnp.int32, (8, SB_WIDTH), 0)
    lane_head = lax.broadcasted_iota(jnp.int32, (8, SB_WIDTH), 1) // HEAD_DIM
    own = row == lane_head
    qm = jnp.where(own, q_ref[...] * (HEAD_DIM ** -0.5 * LOG2E), 0.0).astype(BF16)
    acc = acc_ref[...]
    carry = carry_ref[...]
    zs = [_dot(qm, k_refs[c][...].astype(BF16)) + bias_ref[...] for c in range(g)]
    log_keeps = [-_softplus2(z) for z in zs]
    lt_all = _split_dot(jnp.concatenate(log_keeps, axis=0), u_ref[...])
    ws = []
    for c in range(g):
        lt = lt_all[8 * c:8 * c + 8]
        ws.append(jnp.exp2(zs[c] + log_keeps[c] + lt[:, :LANE] + carry).astype(BF16))
        carry = carry + lt[:, LANE:]
    for c in range(g):
        acc = acc + _dot_nt(ws[c], v_refs[c][...].astype(BF16))
    acc_ref[...] = acc
    carry_ref[...] = carry

    @pl.when(s == pl.num_programs(1) - 1)
    def _():
        o_ref[...] = jnp.sum(jnp.where(own, acc, 0.0), axis=0, keepdims=True)


def _sb_decode_call(q, bias, cache_k, cache_v, page_table, layer):
    bsz = q.shape[0]
    n_pages = page_table.shape[1]
    g = DEC_PAGES_PER_STEP
    n_steps = n_pages // g
    bias8 = jnp.zeros((8, LANE), F32).at[:SB_HEADS].set(jnp.broadcast_to(bias[:, None] * LOG2E, (SB_HEADS, LANE)))

    def page_spec(c):
        return pl.BlockSpec((None, None, SB_WIDTH, PAGE_SIZE),
                            lambda b, s, pt: (layer, pt[b, n_pages - 1 - (s * g + c)], 0, 0))

    grid_spec = pltpu.PrefetchScalarGridSpec(
        num_scalar_prefetch=1, grid=(bsz, n_steps),
        in_specs=[pl.BlockSpec((None, 1, SB_WIDTH), lambda b, s, pt: (b, 0, 0)),
                  pl.BlockSpec((8, LANE), lambda b, s, pt: (0, 0)),
                  pl.BlockSpec((PAGE_SIZE, 2 * LANE), lambda b, s, pt: (0, 0))]
                 + [page_spec(c) for c in range(g)] + [page_spec(c) for c in range(g)],
        out_specs=pl.BlockSpec((None, 1, SB_WIDTH), lambda b, s, pt: (b, 0, 0)),
        scratch_shapes=[pltpu.VMEM((8, SB_WIDTH), F32), pltpu.VMEM((8, LANE), F32)])
    out = pl.pallas_call(
        _sb_decode_body, grid_spec=grid_spec,
        out_shape=jax.ShapeDtypeStruct((bsz, 1, SB_WIDTH), F32),
        compiler_params=_params(("parallel", "arbitrary")), name="sb_decode",
    )(page_table, q.reshape(bsz, 1, SB_WIDTH), bias8, _cumsum_matrix(PAGE_SIZE),
      *([cache_k] * g), *([cache_v] * g))
    return out.reshape(bsz, SB_WIDTH)


def _chunk_conv(x_ref, xfull_ref, wconv_ref, rows):
    xfull_ref[CONV_PAD:CONV_PAD + rows, :] = x_ref[...]
    acc = wconv_ref[CONV_WIDTH - 1:CONV_WIDTH, :] * xfull_ref[CONV_PAD:CONV_PAD + rows, :]
    for j in range(1, CONV_WIDTH):
        acc = acc + wconv_ref[CONV_WIDTH - 1 - j:CONV_WIDTH - j, :] * xfull_ref[CONV_PAD - j:CONV_PAD - j + rows, :]
    return acc


def _run_interleaved(stage_generators):
    results = [None] * len(stage_generators)
    live = list(range(len(stage_generators)))
    while live:
        for j in list(live):
            try:
                next(stage_generators[j])
            except StopIteration as done:
                results[j] = done.value
                live.remove(j)
    return results


GDN_SEQS_PER_STEP = 4


def _mm_split(a_parts, b_parts):
    (ah, al), (bh, bl) = a_parts, b_parts
    return _dot(ah, bh) + (_dot(ah, bl) + _dot(al, bh))


def _same_head_mask():
    r2 = lax.broadcasted_iota(jnp.int32, (GDN_WIDTH, GDN_WIDTH), 0)
    c2 = lax.broadcasted_iota(jnp.int32, (GDN_WIDTH, GDN_WIDTH), 1)
    return r2, c2, (r2 // GDN_CHUNK) == (c2 // GDN_CHUNK)


def _gdn_prepare(c, qkv_ref, small_ref, wconv_ref, alog_ref, dtb_ref,
                 tril_ref, expb_ref, expg_ref, bd_ref, xfull_ref, *, n_chunks, tv_last):
    L = GDN_CHUNK
    W = GDN_WIDTH
    r2, c2, bd = _same_head_mask()
    xs = _silu(_chunk_conv(qkv_ref, xfull_ref, wconv_ref, L))
    q, k, v = xs[:, 0:W], xs[:, W:2 * W], xs[:, 2 * W:3 * W]
    yield
    small = small_ref[...]
    lane = lax.broadcasted_iota(jnp.int32, (L, LANE), 1)
    beta_s = jnp.where(lane < SMALL_DECAY, jax.nn.sigmoid(small), 0.0)
    g_s = jnp.where((lane >= SMALL_DECAY) & (lane < SMALL_DT),
                    -jnp.exp(alog_ref[...]) * _softplus(small + dtb_ref[...]), 0.0)
    if tv_last < L:
        last_rows = jnp.where(c == n_chunks - 1, tv_last, L)
        keep_w = lax.broadcasted_iota(jnp.int32, (L, W), 0) < last_rows
        keep_s = lax.broadcasted_iota(jnp.int32, (L, LANE), 0) < last_rows
        q, k, v = jnp.where(keep_w, q, 0.0), jnp.where(keep_w, k, 0.0), jnp.where(keep_w, v, 0.0)
        beta_s, g_s = jnp.where(keep_s, beta_s, 0.0), jnp.where(keep_s, g_s, 0.0)
    bdm = bd_ref[...]
    qn = q * lax.rsqrt(_dot3_left(q * q, bdm) + RMS_EPS) * (GDN_DK ** -0.5)
    kn = k * lax.rsqrt(_dot3_left(k * k, bdm) + RMS_EPS)
    gcs_s = _dot3_right(tril_ref[...], g_s)
    beta_e = _dot3_left(beta_s, expb_ref[...])
    yield
    gcs_e = _dot3_left(gcs_s, expg_ref[...])
    yield
    eg = jnp.exp(gcs_e)
    kb = kn * beta_e
    tile4 = lambda a: jnp.concatenate([a] * GDN_HEADS, axis=0)
    stack = lambda a: jnp.where(bd, tile4(a), 0.0)
    rl = lax.broadcasted_iota(jnp.int32, (L, W), 0)
    cl = lax.broadcasted_iota(jnp.int32, (L, W), 1)
    grow = jnp.sum(jnp.where((cl % L) == rl, gcs_e, 0.0), axis=0, keepdims=True)
    incl = bd & ((r2 % L) >= (c2 % L))
    decay = jnp.exp(jnp.where(incl, tile4(gcs_e) - grow, NEG_BIG))
    kns = stack(kn).astype(BF16)
    yield
    a_mat = jnp.where((r2 % L) != (c2 % L), _dot_nt(stack(kb).astype(BF16), kns) * decay, 0.0)
    qk = _dot_nt(stack(qn).astype(BF16), kns) * decay
    yield
    return dict(a=a_mat, qk=qk.astype(BF16), q_dec=(qn * eg).astype(BF16), kn=kn, gcs_e=gcs_e,
                rhs_v=_split2(stack(v * beta_e)), rhs_k=_split2(stack(kb * eg)))


def _gdn_inverses(a_mats):
    r2, c2, _ = _same_head_mask()
    eye = jnp.where(r2 == c2, 1.0, 0.0)
    ps = [eye - a for a in a_mats]
    a_parts = [_split2(a) for a in a_mats]
    x_parts = [_split2(_mm_split(ap, ap)) for ap in a_parts]
    ps = [p + _mm_split(_split2(p), xp) for p, xp in zip(ps, x_parts)]
    x_parts = [_split2(_mm_split(xp, xp)) for xp in x_parts]
    ps = [p + _mm_split(_split2(p), xp) for p, xp in zip(ps, x_parts)]
    xs = [xp[0] for xp in x_parts]
    for _ in range(3):
        xs = [_dot(x, x).astype(BF16) for x in xs]
        ps = [p + _dot(p.astype(BF16), x) for p, x in zip(ps, xs)]
    return [_split2(p) for p in ps]


def _gdn_finish(prep, p_parts, z_ref, gnorm_ref, bd_ref, o_ref, s_ref):
    L = GDN_CHUNK
    bd = _same_head_mask()[2]
    stack = lambda a: jnp.where(bd, jnp.concatenate([a] * GDN_HEADS, axis=0), 0.0)
    collapse = lambda m: (m[0:L] + m[L:2 * L]) + (m[2 * L:3 * L] + m[3 * L:4 * L])
    u_c = collapse(_mm_split(p_parts, prep["rhs_v"]))
    w_c = collapse(_mm_split(p_parts, prep["rhs_k"]))
    yield
    s = s_ref[...]
    sb = s.astype(BF16)
    v_new = u_c - _dot(w_c.astype(BF16), sb)
    yield
    o = _dot(prep["q_dec"], sb) + collapse(_dot(prep["qk"], stack(v_new).astype(BF16)))
    gcs_e = prep["gcs_e"]
    glast = gcs_e[L - 1:L, :]
    k_dec = prep["kn"] * jnp.exp(glast - gcs_e)
    s_ref[...] = s * jnp.exp(glast) + jnp.where(bd, _dot_tn(k_dec.astype(BF16), v_new.astype(BF16)), 0.0)
    yield
    on = o * lax.rsqrt(_dot3_left(o * o, bd_ref[...]) * (1.0 / HEAD_DIM) + RMS_EPS) * gnorm_ref[...]
    o_ref[...] = on * _silu(z_ref[...])


def _gdn_body(qkv_ref, z_ref, small_ref, wconv_ref, alog_ref, dtb_ref, gnorm_ref, conv0_ref, s0_ref,
              tril_ref, expb_ref, expg_ref, bd_ref, fold_ref, foldt_ref,
              o_ref, convn_ref, sn_ref, xfull_ref, s_ref, *, n_chunks, tv_last):
    L = GDN_CHUNK
    c = pl.program_id(1)
    seqs = range(GDN_SEQS_PER_STEP)

    @pl.when(c == 0)
    def _():
        bd = _same_head_mask()[2]
        for j in seqs:
            xfull_ref[j, 0:CONV_PAD, :] = conv0_ref[j]
            s_ref[j] = jnp.where(bd, _dot3_left(s0_ref[j], foldt_ref[...]), 0.0)

    preps = _run_interleaved([
        _gdn_prepare(c, qkv_ref.at[j], small_ref.at[j], wconv_ref, alog_ref, dtb_ref,
                     tril_ref, expb_ref, expg_ref, bd_ref, xfull_ref.at[j], n_chunks=n_chunks, tv_last=tv_last)
        for j in seqs])
    inverses = _gdn_inverses([prep["a"] for prep in preps])
    _run_interleaved([_gdn_finish(preps[j], inverses[j], z_ref.at[j], gnorm_ref, bd_ref, o_ref.at[j], s_ref.at[j])
                      for j in seqs])

    @pl.when(c == n_chunks - 1)
    def _():
        for j in seqs:
            convn_ref[j] = xfull_ref[j, CONV_PAD + tv_last - (CONV_WIDTH - 1):CONV_PAD + tv_last, :]
            sn_ref[j] = _dot3_left(s_ref[j], fold_ref[...])

    for j in seqs:
        xfull_ref[j, 0:CONV_PAD, :] = xfull_ref[j, L:L + CONV_PAD, :]


def _gdn_call(u, n_seq, rows_per_seq, t_valid, conv0, s0, w_conv, a_log, dt_bias, g_norm):
    L = GDN_CHUNK
    G = GDN_SEQS_PER_STEP
    n_chunks = -(-t_valid // L)
    tv_last = t_valid - (n_chunks - 1) * L
    lane_row = lambda vals, off: jnp.zeros((1, LANE), F32).at[0, off:off + vals.shape[0]].set(vals)
    conv0_p = jnp.pad(conv0, ((0, 0), (CONV_PAD - (CONV_WIDTH - 1), 0), (0, 0)))
    u3 = u.reshape(n_seq, rows_per_seq, U_WIDTH)
    row = lambda w, col: pl.BlockSpec((G, L, w), lambda b, c: (b, c, col))
    full = lambda r, cc: pl.BlockSpec((r, cc), lambda b, c: (0, 0))
    per_seq = lambda r, cc: pl.BlockSpec((G, r, cc), lambda b, c: (b, 0, 0))
    out = pl.pallas_call(
        functools.partial(_gdn_body, n_chunks=n_chunks, tv_last=tv_last),
        grid=(n_seq // G, n_chunks),
        in_specs=[row(GDN_CONV_DIM, OFF_QKVB // GDN_CONV_DIM), row(GDN_WIDTH, OFF_ZB // GDN_WIDTH),
                  row(LANE, OFF_SMALL // LANE),
                  full(CONV_WIDTH, GDN_CONV_DIM), full(1, LANE), full(1, LANE), full(1, GDN_WIDTH),
                  per_seq(CONV_PAD, GDN_CONV_DIM), per_seq(GDN_WIDTH, HEAD_DIM),
                  full(L, L), full(LANE, GDN_WIDTH), full(LANE, GDN_WIDTH), full(GDN_WIDTH, GDN_WIDTH),
                  full(GDN_WIDTH, HEAD_DIM), full(HEAD_DIM, GDN_WIDTH)],
        out_specs=[pl.BlockSpec((G, L, GDN_WIDTH), lambda b, c: (b, c, 0)),
                   per_seq(CONV_WIDTH - 1, GDN_CONV_DIM), per_seq(GDN_WIDTH, HEAD_DIM)],
        out_shape=[jax.ShapeDtypeStruct((n_seq, n_chunks * L, GDN_WIDTH), F32),
                   jax.ShapeDtypeStruct((n_seq, CONV_WIDTH - 1, GDN_CONV_DIM), F32),
                   jax.ShapeDtypeStruct((n_seq, GDN_WIDTH, HEAD_DIM), F32)],
        scratch_shapes=[pltpu.VMEM((G, L + CONV_PAD, GDN_CONV_DIM), F32),
                        pltpu.VMEM((G, GDN_WIDTH, GDN_WIDTH), F32)],
        compiler_params=_params(("parallel", "arbitrary")), name="gdn",
    )(u3, u3, u3, w_conv, lane_row(a_log, SMALL_DECAY), lane_row(dt_bias, SMALL_DECAY),
      jnp.tile(g_norm, GDN_HEADS).reshape(1, GDN_WIDTH), conv0_p, s0.reshape(n_seq, GDN_WIDTH, HEAD_DIM),
      _tril_incl(L), _expand_matrix(SMALL_BETA, GDN_HEADS, HEAD_DIM), _expand_matrix(SMALL_DECAY, GDN_HEADS, HEAD_DIM),
      _block_diag_ones(GDN_WIDTH, HEAD_DIM), _fold_matrix(GDN_HEADS, HEAD_DIM), _fold_matrix(GDN_HEADS, HEAD_DIM).T)
    o, conv_new, s_new = out
    return o.reshape(n_seq * n_chunks * L, GDN_WIDTH), conv_new, s_new.reshape(n_seq, GDN_HEADS, GDN_DK, HEAD_DIM)


SSD_SEQS_PER_STEP = 4


def _ssd_chunk(c, xbc_ref, z_ref, small_ref, wconv_ref, bconv_ref, alog_ref, dtb_ref, dskip_ref, gnorm_ref,
               tril_ref, expc_ref, o_ref, xfull_ref, ht_ref, *, n_chunks, tv_last):
    L = SSD_CHUNK
    W = SSD_WIDTH
    GW = W // SSD_GROUPS
    xs = _silu(_chunk_conv(xbc_ref, xfull_ref, wconv_ref, L) + bconv_ref[...])
    x_c, bm, cm = xs[:, 0:W], xs[:, W:W + GW], xs[:, W + GW:W + 2 * GW]
    yield
    small = small_ref[...]
    lane = lax.broadcasted_iota(jnp.int32, (L, LANE), 1)
    dt_s = jnp.where((lane >= SMALL_DT) & (lane < SMALL_DT + SSD_HEADS), _softplus(small + dtb_ref[...]), 0.0)
    if tv_last < L:
        last_rows = jnp.where(c == n_chunks - 1, tv_last, L)
        keep_s = lax.broadcasted_iota(jnp.int32, (L, LANE), 0) < last_rows
        keep_g = lax.broadcasted_iota(jnp.int32, (L, GW), 0) < last_rows
        dt_s = jnp.where(keep_s, dt_s, 0.0)
        bm, cm = jnp.where(keep_g, bm, 0.0), jnp.where(keep_g, cm, 0.0)
    ac_s = dt_s * (-jnp.exp(alog_ref[...]))
    acs_s = _dot3_right(tril_ref[...], ac_s)
    dt_e = _dot3_left(dt_s, expc_ref[...])
    yield
    acs_e = _dot3_left(acs_s, expc_ref[...])
    yield
    xdt = x_c * dt_e
    e = jnp.exp(acs_e)
    last = acs_e[L - 1:L, :]
    xw = xdt * jnp.exp(last - acs_e)
    dchunk = jnp.exp(last)
    acs_t = acs_s.T
    incl = lax.broadcasted_iota(jnp.int32, (L, L), 0) >= lax.broadcasted_iota(jnp.int32, (L, L), 1)
    first_half = lax.broadcasted_iota(jnp.int32, (L, LANE), 1) < HEAD_DIM
    ht = ht_ref[...]
    ys = []
    for g in range(SSD_GROUPS):
        gl = slice(g * GW, (g + 1) * GW)
        bg = bm[:, g * SSD_STATE:(g + 1) * SSD_STATE].astype(BF16)
        cg = cm[:, g * SSD_STATE:(g + 1) * SSD_STATE].astype(BF16)
        scores = _dot_nt(cg, bg)
        y_off = _dot(cg, ht[:, gl].astype(BF16)) * e[:, gl]
        ht_ref[:, gl] = ht[:, gl] * dchunk[:, gl] + _dot_tn(bg, xw[:, gl].astype(BF16))
        yield
        pairs = []
        for pp in range(GW // LANE):
            xp = xdt[:, g * GW + pp * LANE:g * GW + (pp + 1) * LANE]
            yp = jnp.zeros((L, LANE), F32)
            for hh in range(2):
                h = g * (SSD_HEADS // SSD_GROUPS) + pp * 2 + hh
                col = acs_s[:, SMALL_DT + h:SMALL_DT + h + 1]
                rowv = acs_t[SMALL_DT + h:SMALL_DT + h + 1, :]
                decay = jnp.exp(jnp.where(incl, col - rowv, NEG_BIG))
                xm = jnp.where(first_half if hh == 0 else jnp.logical_not(first_half), xp, 0.0)
                yp = yp + _dot((scores * decay).astype(BF16), xm.astype(BF16))
                yield
            pairs.append(yp)
        ys.append(jnp.concatenate(pairs, axis=1) + y_off)
    y = (jnp.concatenate(ys, axis=1) + dskip_ref[...] * x_c) * _silu(z_ref[...])
    outs = []
    for g in range(SSD_GROUPS):
        yg = y[:, g * GW:(g + 1) * GW]
        outs.append(yg * lax.rsqrt(jnp.mean(yg * yg, axis=-1, keepdims=True) + RMS_EPS))
    o_ref[...] = jnp.concatenate(outs, axis=1) * gnorm_ref[...]


def _ssd_body(xbc_ref, z_ref, small_ref, wconv_ref, bconv_ref, alog_ref, dtb_ref, dskip_ref, gnorm_ref,
              conv0_ref, h0_ref, tril_ref, expc_ref,
              o_ref, convn_ref, hn_ref, xfull_ref, ht_ref, *, n_chunks, tv_last):
    L = SSD_CHUNK
    c = pl.program_id(1)
    seqs = range(SSD_SEQS_PER_STEP)
    lane_blocks = range(SSD_WIDTH // LANE)

    @pl.when(c == 0)
    def _():
        for j in seqs:
            xfull_ref[j, 0:CONV_PAD, :] = conv0_ref[j]
            for i in lane_blocks:
                ht_ref[j, :, i * LANE:(i + 1) * LANE] = h0_ref[j, i * LANE:(i + 1) * LANE, :].T

    _run_interleaved([
        _ssd_chunk(c, xbc_ref.at[j], z_ref.at[j], small_ref.at[j], wconv_ref, bconv_ref, alog_ref, dtb_ref, dskip_ref,
                   gnorm_ref, tril_ref, expc_ref, o_ref.at[j], xfull_ref.at[j], ht_ref.at[j],
                   n_chunks=n_chunks, tv_last=tv_last)
        for j in seqs])

    @pl.when(c == n_chunks - 1)
    def _():
        for j in seqs:
            convn_ref[j] = xfull_ref[j, CONV_PAD + tv_last - (CONV_WIDTH - 1):CONV_PAD + tv_last, :]
            for i in lane_blocks:
                hn_ref[j, i * LANE:(i + 1) * LANE, :] = ht_ref[j, :, i * LANE:(i + 1) * LANE].T

    for j in seqs:
        xfull_ref[j, 0:CONV_PAD, :] = xfull_ref[j, L:L + CONV_PAD, :]


def _ssd_call(u, n_seq, rows_per_seq, t_valid, conv0, h0, w_conv, b_conv, a_log, dt_bias, d_skip, g_norm):
    L = SSD_CHUNK
    G = SSD_SEQS_PER_STEP
    n_chunks = -(-t_valid // L)
    tv_last = t_valid - (n_chunks - 1) * L
    lane_row = lambda vals, off: jnp.zeros((1, LANE), F32).at[0, off:off + vals.shape[0]].set(vals)
    conv0_p = jnp.pad(conv0, ((0, 0), (CONV_PAD - (CONV_WIDTH - 1), 0), (0, 0)))
    u3 = u.reshape(n_seq, rows_per_seq, U_WIDTH)
    row = lambda w, col: pl.BlockSpec((G, L, w), lambda b, c: (b, c, col))
    full = lambda r, cc: pl.BlockSpec((r, cc), lambda b, c: (0, 0))
    per_seq = lambda r, cc: pl.BlockSpec((G, r, cc), lambda b, c: (b, 0, 0))
    out = pl.pallas_call(
        functools.partial(_ssd_body, n_chunks=n_chunks, tv_last=tv_last),
        grid=(n_seq // G, n_chunks),
        in_specs=[row(SSD_XBC, OFF_XBC // SSD_XBC), row(SSD_WIDTH, OFF_ZC // SSD_WIDTH), row(LANE, OFF_SMALL // LANE),
                  full(CONV_WIDTH, SSD_XBC), full(1, SSD_XBC), full(1, LANE), full(1, LANE),
                  full(1, SSD_WIDTH), full(1, SSD_WIDTH),
                  per_seq(CONV_PAD, SSD_XBC), per_seq(SSD_WIDTH, SSD_STATE),
                  full(L, L), full(LANE, SSD_WIDTH)],
        out_specs=[pl.BlockSpec((G, L, SSD_WIDTH), lambda b, c: (b, c, 0)),
                   per_seq(CONV_WIDTH - 1, SSD_XBC), per_seq(SSD_WIDTH, SSD_STATE)],
        out_shape=[jax.ShapeDtypeStruct((n_seq, n_chunks * L, SSD_WIDTH), F32),
                   jax.ShapeDtypeStruct((n_seq, CONV_WIDTH - 1, SSD_XBC), F32),
                   jax.ShapeDtypeStruct((n_seq, SSD_WIDTH, SSD_STATE), F32)],
        scratch_shapes=[pltpu.VMEM((G, L + CONV_PAD, SSD_XBC), F32), pltpu.VMEM((G, SSD_STATE, SSD_WIDTH), F32)],
        compiler_params=_params(("parallel", "arbitrary")), name="ssd",
    )(u3, u3, u3, w_conv, b_conv.reshape(1, SSD_XBC), lane_row(a_log, SMALL_DT), lane_row(dt_bias, SMALL_DT),
      jnp.repeat(d_skip, HEAD_DIM).reshape(1, SSD_WIDTH), g_norm.reshape(1, SSD_WIDTH),
      conv0_p, h0.reshape(n_seq, SSD_WIDTH, SSD_STATE), _tril_incl(L), _expand_matrix(SMALL_DT, SSD_HEADS, HEAD_DIM))
    o, conv_new, h_new = out
    return (o.reshape(n_seq * n_chunks * L, SSD_WIDTH), conv_new,
            h_new.reshape(n_seq, SSD_HEADS, HEAD_DIM, SSD_STATE))


ROUTE_GATE, ROUTE_IDX, ROUTE_RANK = 0, TOP_K, 2 * TOP_K


def _strict_lower(n):
    return jnp.asarray(np.tril(np.ones((n, n), bool), -1), BF16)


def _proj_out_body(x_ref, oa_ref, ob_ref, oc_ref, w_ref, g_ref, b_ref, rw_ref, rb_ref, lower_ref,
                   x1_ref, route_ref, count_ref):
    mix = _dot(oa_ref[...].astype(BF16), w_ref[0:256, :])
    mix += _dot(ob_ref[...].astype(BF16), w_ref[256:512, :])
    mix += _dot(oc_ref[...].astype(BF16), w_ref[512:1024, :])
    x1 = _layer_norm_f32(DEEPNORM_ALPHA * x_ref[...] + mix, g_ref[...], b_ref[...])
    x1_ref[...] = x1
    logits = _mm3(x1, rw_ref[...]) + rb_ref[...]

    @pl.when(pl.program_id(0) == 0)
    def _():
        count_ref[...] = jnp.zeros_like(count_ref)

    lane = lax.broadcasted_iota(jnp.int32, logits.shape, 1)
    lg = jnp.where(lane < N_EXPERTS, logits, -jnp.inf)
    vals, idxs = [], []
    for _ in range(TOP_K):
        m = jnp.max(lg, axis=1, keepdims=True)
        idx = jnp.min(jnp.where(lg == m, lane, LANE), axis=1, keepdims=True)
        vals.append(m)
        idxs.append(idx)
        lg = jnp.where(lane == idx, -jnp.inf, lg)
    exps = [jnp.exp(v - vals[0]) for v in vals]
    denom = exps[0] + exps[1] + exps[2] + exps[3]
    hot = [lane == idx for idx in idxs]
    multi = jnp.where(hot[0] | hot[1] | hot[2] | hot[3], 1.0, 0.0)
    before = _dot(lower_ref[...], multi.astype(BF16)) + count_ref[...]
    route = jnp.zeros(logits.shape, F32)
    for k in range(TOP_K):
        rank = jnp.sum(jnp.where(hot[k], before, 0.0), axis=1, keepdims=True)
        route = jnp.where(lane == ROUTE_GATE + k, exps[k] / denom, route)
        route = jnp.where(lane == ROUTE_IDX + k, idxs[k].astype(F32), route)
        route = jnp.where(lane == ROUTE_RANK + k, rank, route)
    route_ref[...] = route
    count_ref[...] += jnp.sum(multi, axis=0, keepdims=True)


def _proj_out_call(x, o_a, o_b, o_c, w_bf16, g, b, router_w, router_b):
    n, d = x.shape
    bm = _row_block(n)
    rw = jnp.pad(router_w, ((0, 0), (0, LANE - N_EXPERTS)))
    rb = jnp.pad(router_b, (0, LANE - N_EXPERTS)).reshape(1, LANE)
    row = lambda w: pl.BlockSpec((bm, w), lambda i: (i, 0))
    full = lambda r, c: pl.BlockSpec((r, c), lambda i: (0, 0))
    return pl.pallas_call(
        _proj_out_body, grid=(n // bm,),
        in_specs=[row(d), row(SB_WIDTH), row(GDN_WIDTH), row(SSD_WIDTH), full(d, d), full(1, d), full(1, d),
                  full(d, LANE), full(1, LANE), full(bm, bm)],
        out_specs=[row(d), row(LANE), full(1, LANE)],
        out_shape=[jax.ShapeDtypeStruct((n, d), F32), jax.ShapeDtypeStruct((n, LANE), F32),
                   jax.ShapeDtypeStruct((1, LANE), F32)],
        compiler_params=_params(("arbitrary",)), name="proj_out",
    )(x, o_a, o_b, o_c, w_bf16, g.reshape(1, d), b.reshape(1, d), rw, rb, _strict_lower(bm))


def _moe_body(be_ref, nb_ref, rs_ref, rows_ref, wg_ref, bg_ref, wu_ref, bu_ref, wd_ref, bd_ref, o_ref, wg_s, wu_s, wd_s):
    i = pl.program_id(0)
    prev = be_ref[jnp.maximum(i - 1, 0)]

    @pl.when(jnp.logical_or(i == 0, be_ref[i] != prev))
    def _():
        wg_s[...] = wg_ref[...].astype(BF16)
        wu_s[...] = wu_ref[...].astype(BF16)
        wd_s[...] = wd_ref[...].astype(BF16)

    @pl.when(i < nb_ref[0])
    def _():
        xb = rows_ref[...].astype(BF16)
        glu = jnp.minimum(_dot(xb, wg_s[...]) + bg_ref[...], SWIGLU_LIMIT)
        lin = jnp.clip(_dot(xb, wu_s[...]) + bu_ref[...], -SWIGLU_LIMIT, SWIGLU_LIMIT)
        act = glu * jax.nn.sigmoid(SWIGLU_ALPHA * glu) * (lin + 1.0)
        o_ref[...] = _dot(act.astype(BF16), wd_s[...]) + bd_ref[...]

    @pl.when(i >= nb_ref[0])
    def _():
        o_ref[...] = jnp.zeros_like(o_ref)


def _moe_call(rows_in, block_expert, n_used, row_start, w_gate, b_gate, w_up, b_up, w_down, b_down, layer, blk):
    d = rows_in.shape[1]
    n_blocks = block_expert.shape[0]
    n_rows = n_blocks * blk
    wspec = lambda r, c: pl.BlockSpec((None, None, r, c), lambda i, be, nb, rs: (layer, be[i], 0, 0))
    grid_spec = pltpu.PrefetchScalarGridSpec(
        num_scalar_prefetch=3, grid=(n_blocks,),
        in_specs=[pl.BlockSpec((pl.Element(blk), pl.Element(d)), lambda i, be, nb, rs: (rs[i] * ROW_ALIGN, 0)),
                  wspec(d, D_EXPERT), wspec(1, D_EXPERT), wspec(d, D_EXPERT), wspec(1, D_EXPERT),
                  wspec(D_EXPERT, d), wspec(1, d)],
        out_specs=pl.BlockSpec((blk, d), lambda i, be, nb, rs: (i, 0)),
        scratch_shapes=[pltpu.VMEM((d, D_EXPERT), BF16), pltpu.VMEM((d, D_EXPERT), BF16),
                        pltpu.VMEM((D_EXPERT, d), BF16)])
    depth = w_gate.shape[0]
    return pl.pallas_call(
        _moe_body, grid_spec=grid_spec,
        out_shape=jax.ShapeDtypeStruct((n_rows, d), F32),
        compiler_params=_params(("arbitrary",)), name="moe_experts",
    )(block_expert, n_used, row_start, rows_in, w_gate, b_gate.reshape(depth, N_EXPERTS, 1, D_EXPERT),
      w_up, b_up.reshape(depth, N_EXPERTS, 1, D_EXPERT), w_down, b_down.reshape(depth, N_EXPERTS, 1, d))


def _route(route, counts, blk):
    n_tok = route.shape[0]
    gate = route[:, ROUTE_GATE:ROUTE_GATE + TOP_K]
    expert = route[:, ROUTE_IDX:ROUTE_IDX + TOP_K].astype(jnp.int32)
    rank = route[:, ROUTE_RANK:ROUTE_RANK + TOP_K].astype(jnp.int32)
    counts = counts[0, :N_EXPERTS].astype(jnp.int32)
    n_assign = n_tok * TOP_K
    n_blocks = -(-(n_assign + N_EXPERTS * (blk - 1)) // blk)
    padded = (counts + blk - 1) // blk * blk
    padded_end = jnp.cumsum(padded)
    padded_start = padded_end - padded
    dest = padded_start[expert] + rank
    token = jnp.broadcast_to(jnp.arange(n_tok, dtype=jnp.int32)[:, None], (n_tok, TOP_K))
    block_first = jnp.arange(n_blocks, dtype=jnp.int32) * blk
    block_expert = jnp.minimum(jnp.sum(padded_end[None, :] <= block_first[:, None], axis=1, dtype=jnp.int32),
                               N_EXPERTS - 1)
    aligned = (counts + ROW_ALIGN - 1) // ROW_ALIGN * ROW_ALIGN
    aligned_start = jnp.cumsum(aligned) - aligned
    position = (aligned_start[expert] + rank).reshape(-1)
    n_gap = N_EXPERTS * ROW_ALIGN
    if n_assign <= SMALL_ROUTE_ASSIGNMENTS:
        hit = position[None, :] == jnp.arange(n_assign + n_gap, dtype=jnp.int32)[:, None]
        token_sorted = jnp.sum(jnp.where(hit, token.reshape(1, -1), 0), axis=1, dtype=jnp.int32)
    else:
        j = jnp.arange(ROW_ALIGN, dtype=jnp.int32)[None, :]
        gap_key = jnp.where(j < (aligned - counts)[:, None], (aligned_start + counts)[:, None] + j, n_assign + n_gap)
        keys = jnp.concatenate([position, gap_key.reshape(-1)])
        vals = jnp.concatenate([token.reshape(-1), jnp.zeros((n_gap,), jnp.int32)])
        _, token_sorted = lax.sort((keys, vals), num_keys=1)
    src = jnp.concatenate([token_sorted, jnp.zeros((blk,), jnp.int32)])
    dest_t = dest.T.reshape(-1)
    n_used = (padded_end[-1] // blk).astype(jnp.int32)
    row_start = jnp.where(jnp.arange(n_blocks, dtype=jnp.int32) < n_used,
                          aligned_start[block_expert] + block_first - padded_start[block_expert], 0)
    return gate, src, dest_t, block_expert, n_used.reshape(1), row_start // ROW_ALIGN


def _post_body(x1_ref, r0_ref, r1_ref, r2_ref, r3_ref, gate_ref, p_ref, g_ref, b_ref, wg_ref, bg_ref, wp_ref, o_ref):
    gate = gate_ref[...]
    ffn = gate[:, 0:1] * r0_ref[...]
    for k, r_ref in ((1, r1_ref), (2, r2_ref), (3, r3_ref)):
        ffn += gate[:, k:k + 1] * r_ref[...]
    x2 = _layer_norm_f32(DEEPNORM_ALPHA * x1_ref[...] + ffn, g_ref[...], b_ref[...])
    gate_logit = _dot(x2.astype(BF16), wg_ref[...]) + bg_ref[...]
    emb = _dot(p_ref[...].astype(BF16), wp_ref[...])
    o_ref[...] = x2 + jax.nn.sigmoid(gate_logit) * emb


def _post_call(x1, rows_k, gate, p, g, b, wg_bf16, bg, wp_bf16):
    n, d = x1.shape
    bm = _row_block(n)
    nb = n // bm
    row = lambda w: pl.BlockSpec((bm, w), lambda i: (i, 0))
    plane = lambda k: pl.BlockSpec((bm, d), lambda i: (k * nb + i, 0))
    full = lambda r, c: pl.BlockSpec((r, c), lambda i: (0, 0))
    return pl.pallas_call(
        _post_body, grid=(nb,),
        in_specs=[row(d)] + [plane(k) for k in range(TOP_K)] + [row(TOP_K), row(PLE_DIM), full(1, d), full(1, d),
                                                                 full(d, d), full(1, d), full(PLE_DIM, d)],
        out_specs=row(d),
        out_shape=jax.ShapeDtypeStruct((n, d), F32),
        compiler_params=_params(("parallel",)), name="post_ffn",
    )(x1, rows_k, rows_k, rows_k, rows_k, gate, p, g.reshape(1, d), b.reshape(1, d), wg_bf16, bg.reshape(1, d),
      wp_bf16)


def kernel(x_prompt, x_sample, cache_k_a, cache_v_a, state_conv_b, state_delta_b, state_conv_c, state_ssm_c, page_table, p_prompt, p_sample, ln_in_g, ln_in_b, w_in, sb_bias, w_conv_b, a_log_b, dt_bias_b, g_norm_b, w_conv_c, b_conv_c, a_log_c, dt_bias_c, d_skip_c, g_norm_c, w_out, ln1_g, ln1_b, router_w, router_b, w_gate, b_gate, w_up, b_up, w_down, b_down, ln2_g, ln2_b, ple_gate_w, ple_gate_b, ple_proj_w):
    depth = w_in.shape[0]
    pad_cols =U_WIDTH - sum(hi - lo for lo, hi in _IN_SEGMENTS)
    w_in_r = jnp.concatenate([w_in[:, :, lo:hi] for lo, hi in _IN_SEGMENTS]
                             + [jnp.zeros(w_in.shape[:2] + (pad_cols,), w_in.dtype)], axis=-1).astype(BF16)
    w_out_b = w_out.astype(BF16)
    ple_gate_w_b = ple_gate_w.astype(BF16)
    ple_proj_w_b = ple_proj_w.astype(BF16)
    n_pool = cache_k_a.shape[1]
    cache_k = cache_k_a.transpose(0, 1, 3, 4, 2).reshape(depth, n_pool, SB_WIDTH, PAGE_SIZE)
    cache_v = cache_v_a.transpose(0, 1, 3, 4, 2).reshape(depth, n_pool, SB_WIDTH, PAGE_SIZE)

    def mixers_and_router(i, x, bsz, t, sample):
        n = bsz * t
        u = _proj_in_call(x, w_in_r[i])
        k_a = u[:, OFF_K:OFF_K + SB_WIDTH].reshape(bsz, t, SB_HEADS, HEAD_DIM)
        v_a = u[:, OFF_V:OFF_V + SB_WIDTH].reshape(bsz, t, SB_HEADS, HEAD_DIM)
        if sample:
            o_a = _sb_decode_call(u[:, OFF_Q:OFF_Q + SB_WIDTH], sb_bias[i], cache_k, cache_v, page_table, i)
            conv_b, delta_b, conv_c, ssm_c = state_conv_b[i], state_delta_b[i], state_conv_c[i], state_ssm_c[i]
            rows_per_seq = SSD_CHUNK
            u_mix = jnp.pad(u.reshape(bsz, t, U_WIDTH), ((0, 0), (0, rows_per_seq - t), (0, 0)))
            u_mix = u_mix.reshape(bsz * rows_per_seq, U_WIDTH)
        else:
            o_a = _sb_prefill_call(u, sb_bias[i], bsz, t)
            conv_b = jnp.zeros((bsz, CONV_WIDTH - 1, GDN_CONV_DIM), F32)
            delta_b = jnp.zeros((bsz, GDN_HEADS, GDN_DK, HEAD_DIM), F32)
            conv_c = jnp.zeros((bsz, CONV_WIDTH - 1, SSD_XBC), F32)
            ssm_c = jnp.zeros((bsz, SSD_HEADS, HEAD_DIM, SSD_STATE), F32)
            rows_per_seq = t
            u_mix = u
        o_b, conv_b_new, delta_new = _gdn_call(u_mix, bsz, rows_per_seq, t, conv_b, delta_b, w_conv_b[i],
                                               a_log_b[i], dt_bias_b[i], g_norm_b[i])
        o_c, conv_c_new, ssm_new = _ssd_call(u_mix, bsz, rows_per_seq, t, conv_c, ssm_c, w_conv_c[i], b_conv_c[i],
                                             a_log_c[i], dt_bias_c[i], d_skip_c[i], g_norm_c[i])
        if sample:
            o_b = o_b.reshape(bsz, -1, GDN_WIDTH)[:, :t].reshape(n, GDN_WIDTH)
            o_c = o_c.reshape(bsz, -1, SSD_WIDTH)[:, :t].reshape(n, SSD_WIDTH)
        x1, route, counts = _proj_out_call(x, o_a, o_b, o_c, w_out_b[i], ln1_g[i], ln1_b[i], router_w[i], router_b[i])
        blk = MOE_BLOCK_PROMPT if n * TOP_K >= MOE_BLOCK_PROMPT * N_EXPERTS else MOE_BLOCK_SMALL
        gate, src, dest_t, block_expert, n_used, row_start = _route(route, counts, blk)
        return (x1, gate, dest_t, block_expert, n_used, row_start, blk, x1[src]), (k_a, v_a, conv_b_new, delta_new,
                                                                                   conv_c_new, ssm_new)

    def experts(i, routed):
        x1, gate, dest_t, block_expert, n_used, row_start, blk, rows_in = routed
        rows_out = _moe_call(rows_in, block_expert, n_used, row_start, w_gate, b_gate, w_up, b_up, w_down, b_down, i,
                             blk)
        return rows_out[dest_t]

    def combine(i, routed, rows_k, p):
        x1, gate = routed[0], routed[1]
        return _post_call(x1, rows_k, gate, p[i].reshape(x1.shape[0], PLE_DIM), ln2_g[i], ln2_b[i], ple_gate_w_b[i],
                          ple_gate_b[i], ple_proj_w_b[i])

    groups = [(x_prompt, p_prompt, False), (x_sample, p_sample, True)]
    xs = [_ln_call(x.reshape(-1, x.shape[-1]), ln_in_g, ln_in_b) for x, _, _ in groups]
    states = [[], []]
    for i in range(depth):
        routed = []
        for gi, (x, _, sample) in enumerate(groups):
            r, st = mixers_and_router(i, xs[gi], x.shape[0], x.shape[1], sample)
            routed.append(r)
            states[gi].append(st)
        rows_k = [experts(i, r) for r in routed]
        xs = [combine(i, routed[gi], rows_k[gi], groups[gi][1]) for gi in range(len(groups))]
    y_prompt, y_sample = (xs[gi].reshape(groups[gi][0].shape) for gi in range(len(groups)))
    new_prompt, new_sample = ([jnp.stack([s[j] for s in st]) for j in range(6)] for st in states)
    return (y_prompt, y_sample, *new_prompt, *new_sample)
```

```python
import functools

import jax
import jax.numpy as jnp
import numpy as np
from jax import lax
from jax.experimental import pallas as pl
from jax.experimental.pallas import tpu as pltpu

F32 = jnp.float32
BF16 = jnp.bfloat16

D_MODEL = 1024
DEPTH = 4
PAGE_SIZE = 128
HEAD_DIM = 64
CONV_WIDTH = 4
SB_WIDTH = 256
SB_HEADS = 4
GDN_WIDTH = 256
GDN_HEADS = 4
GDN_DK = 64
GDN_CONV_DIM = 768
GDN_CHUNK = 64
SSD_WIDTH = 512
SSD_HEADS = 8
SSD_GROUPS = 2
SSD_STATE = 128
SSD_XBC = 1024
SSD_CHUNK = 128
N_EXPERTS = 32
TOP_K = 4
D_EXPERT = 512
SWIGLU_ALPHA = 1.702
SWIGLU_LIMIT = 7.0
PLE_DIM = 256
DEEPNORM_ALPHA = (2 * DEPTH) ** 0.25
LN_EPS = 1e-5
RMS_EPS = 1e-6

_REF_WIDTHS = (256, 256, 256, 768, 256, 4, 4, 512, 1024, 8)
_REF_OFF = tuple(int(v) for v in np.cumsum((0,) + _REF_WIDTHS))
OFF_XBC, OFF_ZC, OFF_QKVB, OFF_Q, OFF_K, OFF_V, OFF_ZB, OFF_SMALL = 0, 1024, 1536, 2304, 2560, 2816, 3072, 3328
U_WIDTH = 3456
LANE = 128
SMALL_BETA, SMALL_DECAY, SMALL_DT = 0, GDN_HEADS, 2 * GDN_HEADS
_IN_SEGMENTS = (
    (_REF_OFF[8], _REF_OFF[9]),
    (_REF_OFF[7], _REF_OFF[8]),
    (_REF_OFF[3], _REF_OFF[4]),
    (_REF_OFF[0], _REF_OFF[3]),
    (_REF_OFF[4], _REF_OFF[5]),
    (_REF_OFF[5], _REF_OFF[7]),
    (_REF_OFF[9], _REF_OFF[10]),
)

VMEM_LIMIT = 48 * 1024 * 1024
MOE_BLOCK_PROMPT = 512
MOE_BLOCK_SMALL = 8
SMALL_ROUTE_ASSIGNMENTS = 1024
ROW_ALIGN = 8
SB_BLOCK = 256
DEC_PAGES_PER_STEP = 16
CONV_PAD = 8
NEG_BIG = -1e30
LOG2E = 1.4426950408889634


def _params(sem):
    return pltpu.CompilerParams(dimension_semantics=sem, vmem_limit_bytes=VMEM_LIMIT)


def _row_block(n):
    return n if n < 512 else 512


def _layer_norm_f32(x, g, b):
    mu = jnp.mean(x, axis=-1, keepdims=True)
    xc = x - mu
    var = jnp.mean(xc * xc, axis=-1, keepdims=True)
    return xc * lax.rsqrt(var + LN_EPS) * g + b


def _softplus(z):
    return jnp.maximum(z, 0.0) + jnp.log1p(jnp.exp(-jnp.abs(z)))


def _softplus2(z2):
    return jnp.maximum(z2, 0.0) + jnp.log2(1.0 + jnp.exp2(-jnp.abs(z2)))


def _silu(x):
    return x * jax.nn.sigmoid(x)


def _dot(a, b):
    return jnp.dot(a, b, preferred_element_type=F32)


def _dot_nt(a, b):
    return lax.dot_general(a, b, (((1,), (1,)), ((), ())), preferred_element_type=F32)


def _dot_tn(a, b):
    return lax.dot_general(a, b, (((0,), (0,)), ((), ())), preferred_element_type=F32)


def _split2(x):
    hi = x.astype(BF16)
    return hi, (x - hi.astype(F32)).astype(BF16)


def _split3(x):
    hi = x.astype(BF16)
    r = x - hi.astype(F32)
    mid = r.astype(BF16)
    return hi, mid, (r - mid.astype(F32)).astype(BF16)


def _split_dot(x, w_bf16):
    hi, lo = _split2(x)
    return _dot(hi, w_bf16) + _dot(lo, w_bf16)


def _dot3_left(x, w_bf16):
    hi, mid, lo = _split3(x)
    return _dot(hi, w_bf16) + (_dot(mid, w_bf16) + _dot(lo, w_bf16))


def _dot3_right(w_bf16, x):
    hi, mid, lo = _split3(x)
    return _dot(w_bf16, hi) + (_dot(w_bf16, mid) + _dot(w_bf16, lo))


def _mm3(a, b):
    ah, al = _split2(a)
    bh, bl = _split2(b)
    return _dot(ah, bh) + (_dot(ah, bl) + _dot(al, bh))


def _cumsum_matrix(n):
    j = np.arange(n)[:, None]
    s = np.arange(n)[None, :]
    return jnp.asarray(np.concatenate([(j > s), np.ones((n, LANE), bool)], axis=1), BF16)


def _tril_incl(n):
    return jnp.asarray(np.tril(np.ones((n, n), bool)), BF16)


def _expand_matrix(first_lane, heads, width):
    m = np.zeros((LANE, heads * width), bool)
    for h in range(heads):
        m[first_lane + h, h * width:(h + 1) * width] = True
    return jnp.asarray(m, BF16)


def _block_diag_ones(n, blk):
    i = np.arange(n)
    return jnp.asarray((i[:, None] // blk) == (i[None, :] // blk), BF16)


def _fold_matrix(heads, width):
    i = np.arange(heads * width)
    return jnp.asarray((i[:, None] % width) == np.arange(width)[None, :], BF16)


def _ln_body(x_ref, g_ref, b_ref, o_ref):
    o_ref[...] = _layer_norm_f32(x_ref[...], g_ref[...], b_ref[...])


def _ln_call(x, g, b):
    n, d = x.shape
    bm = _row_block(n)
    return pl.pallas_call(
        _ln_body, grid=(n // bm,),
        in_specs=[pl.BlockSpec((bm, d), lambda i: (i, 0)),
                  pl.BlockSpec((1, d), lambda i: (0, 0)), pl.BlockSpec((1, d), lambda i: (0, 0))],
        out_specs=pl.BlockSpec((bm, d), lambda i: (i, 0)),
        out_shape=jax.ShapeDtypeStruct((n, d), F32),
        compiler_params=_params(("parallel",)), name="ln_in",
    )(x, g.reshape(1, d), b.reshape(1, d))


_PROJ_CHUNK = 384


def _proj_in_body(x_ref, w_ref, o_ref):
    xb = x_ref[...].astype(BF16)
    for j in range(0, U_WIDTH, _PROJ_CHUNK):
        o_ref[:, j:j + _PROJ_CHUNK] = _dot(xb, w_ref[:, j:j + _PROJ_CHUNK])


def _proj_in_call(x, w_bf16):
    n, d = x.shape
    bm = _row_block(n)
    return pl.pallas_call(
        _proj_in_body, grid=(n // bm,),
        in_specs=[pl.BlockSpec((bm, d), lambda i: (i, 0)),
                  pl.BlockSpec((d, U_WIDTH), lambda i: (0, 0))],
        out_specs=pl.BlockSpec((bm, U_WIDTH), lambda i: (i, 0)),
        out_shape=jax.ShapeDtypeStruct((n, U_WIDTH), F32),
        compiler_params=_params(("parallel",)), name="proj_in",
    )(x, w_bf16)


def _sb_prefill_body(bias_ref, q_ref, k_ref, v_ref, u_ref, o_ref, qh_ref, acc_ref, carry_ref):
    blk = q_ref.shape[0]
    i = pl.program_id(1)
    q = q_ref[...] * (HEAD_DIM ** -0.5 * LOG2E)
    lane_head = lax.broadcasted_iota(jnp.int32, (1, SB_WIDTH), 1) // HEAD_DIM
    for h in range(SB_HEADS):
        qh_ref[h] = jnp.where(lane_head == h, q, 0.0).astype(BF16)
    acc_ref[...] = jnp.zeros_like(acc_ref)
    carry_ref[...] = jnp.zeros_like(carry_ref)
    rows = lax.broadcasted_iota(jnp.int32, (blk, blk), 0)
    cols = lax.broadcasted_iota(jnp.int32, (blk, blk), 1)

    def key_block(j, diagonal):
        start = pl.multiple_of(j * blk, blk)
        kj = k_ref[pl.ds(start, blk), :].astype(BF16)
        vj = v_ref[pl.ds(start, blk), :].astype(BF16)
        heads = range(SB_HEADS)
        visible = cols < rows
        zs = [_dot_nt(qh_ref[h], kj) + bias_ref[h] for h in heads]
        log_keeps = [-_softplus2(z) for z in zs]
        logits = [z + lk for z, lk in zip(zs, log_keeps)]
        if diagonal:
            log_keeps = [jnp.where(visible, lk, 0.0) for lk in log_keeps]
        lts = [_split_dot(lk, u_ref[...]) for lk in log_keeps]
        carries = [carry_ref[h] for h in heads]
        ws = [jnp.exp2(logits[h] + (lts[h][:, :blk] + jnp.concatenate([carries[h]] * (blk // LANE), axis=1)))
              for h in heads]
        if diagonal:
            ws = [jnp.where(visible, w, 0.0) for w in ws]
        for h in heads:
            acc_ref[h] += _dot(ws[h].astype(BF16), vj)
            carry_ref[h] = carries[h] + lts[h][:, blk:]

    key_block(i, True)

    def earlier(t, carry):
        key_block(i - 1 - t, False)
        return carry

    lax.fori_loop(0, i, earlier, 0)
    out = jnp.zeros((blk, SB_WIDTH), F32)
    for h in range(SB_HEADS):
        out = out + jnp.where(lane_head == h, acc_ref[h], 0.0)
    o_ref[...] = out


def _sb_prefill_call(u, bias, bsz, t):
    blk = SB_BLOCK
    nq = t // blk
    grid_spec = pltpu.PrefetchScalarGridSpec(
        num_scalar_prefetch=0, grid=(bsz, nq),
        in_specs=[pl.BlockSpec(memory_space=pltpu.SMEM),
                  pl.BlockSpec((blk, SB_WIDTH), lambda b, i: (b * nq + i, OFF_Q // SB_WIDTH)),
                  pl.BlockSpec((t, SB_WIDTH), lambda b, i: (b, OFF_K // SB_WIDTH)),
                  pl.BlockSpec((t, SB_WIDTH), lambda b, i: (b, OFF_V // SB_WIDTH)),
                  pl.BlockSpec((blk, blk + LANE), lambda b, i: (0, 0))],
        out_specs=pl.BlockSpec((blk, SB_WIDTH), lambda b, i: (b * nq + i, 0)),
        scratch_shapes=[pltpu.VMEM((SB_HEADS, blk, SB_WIDTH), BF16),
                        pltpu.VMEM((SB_HEADS, blk, SB_WIDTH), F32),
                        pltpu.VMEM((SB_HEADS, blk, LANE), F32)])
    return pl.pallas_call(
        _sb_prefill_body, grid_spec=grid_spec,
        out_shape=jax.ShapeDtypeStruct((bsz * t, SB_WIDTH), F32),
        compiler_params=_params(("parallel", "arbitrary")), name="sb_prefill",
    )(bias * LOG2E, u, u, u, _cumsum_matrix(blk))


def _sb_decode_body(pt_ref, q_ref, bias_ref, u_ref, *refs):
    g = DEC_PAGES_PER_STEP
    k_refs, v_refs = refs[:g], refs[g:2 * g]
    o_ref, acc_ref, carry_ref = refs[2 * g], refs[2 * g + 1], refs[2 * g + 2]
    s = pl.program_id(1)

    @pl.when(s == 0)
    def _():
        acc_ref[...] = jnp.zeros_like(acc_ref)
        carry_ref[...] = jnp.zeros_like(carry_ref)

    row = lax.broadcasted_iota(jnp.int32, (8, SB_WIDTH), 0)
    lane_head = lax.broadcasted_iota(jnp.int32, (8, SB_WIDTH), 1) // HEAD_DIM
    own = row == lane_head
    qm = jnp.where(own, q_ref[...] * (HEAD_DIM ** -0.5 * LOG2E), 0.0).astype(BF16)
    acc = acc_ref[...]
    carry = carry_ref[...]
    zs = [_dot(qm, k_refs[c][...].astype(BF16)) + bias_ref[...] for c in range(g)]
    log_keeps = [-_softplus2(z) for z in zs]
    lt_all = _split_dot(jnp.concatenate(log_keeps, axis=0), u_ref[...])
    ws = []
    for c in range(g):
        lt = lt_all[8 * c:8 * c + 8]
        ws.append(jnp.exp2(zs[c] + log_keeps[c] + lt[:, :LANE] + carry).astype(BF16))
        carry = carry + lt[:, LANE:]
    for c in range(g):
        acc = acc + _dot_nt(ws[c], v_refs[c][...].astype(BF16))
    acc_ref[...] = acc
    carry_ref[...] = carry

    @pl.when(s == pl.num_programs(1) - 1)
    def _():
        o_ref[...] = jnp.sum(jnp.where(own, acc, 0.0), axis=0, keepdims=True)


def _sb_decode_call(q, bias, cache_k, cache_v, page_table, layer):
    bsz = q.shape[0]
    n_pages = page_table.shape[1]
    g = DEC_PAGES_PER_STEP
    n_steps = n_pages // g
    bias8 = jnp.zeros((8, LANE), F32).at[:SB_HEADS].set(jnp.broadcast_to(bias[:, None] * LOG2E, (SB_HEADS, LANE)))

    def page_spec(c):
        return pl.BlockSpec((None, None, SB_WIDTH, PAGE_SIZE),
                            lambda b, s, pt: (layer, pt[b, n_pages - 1 - (s * g + c)], 0, 0))

    grid_spec = pltpu.PrefetchScalarGridSpec(
        num_scalar_prefetch=1, grid=(bsz, n_steps),
        in_specs=[pl.BlockSpec((None, 1, SB_WIDTH), lambda b, s, pt: (b, 0, 0)),
                  pl.BlockSpec((8, LANE), lambda b, s, pt: (0, 0)),
                  pl.BlockSpec((PAGE_SIZE, 2 * LANE), lambda b, s, pt: (0, 0))]
                 + [page_spec(c) for c in range(g)] + [page_spec(c) for c in range(g)],
        out_specs=pl.BlockSpec((None, 1, SB_WIDTH), lambda b, s, pt: (b, 0, 0)),
        scratch_shapes=[pltpu.VMEM((8, SB_WIDTH), F32), pltpu.VMEM((8, LANE), F32)])
    out = pl.pallas_call(
        _sb_decode_body, grid_spec=grid_spec,
        out_shape=jax.ShapeDtypeStruct((bsz, 1, SB_WIDTH), F32),
        compiler_params=_params(("parallel", "arbitrary")), name="sb_decode",
    )(page_table, q.reshape(bsz, 1, SB_WIDTH), bias8, _cumsum_matrix(PAGE_SIZE),
      *([cache_k] * g), *([cache_v] * g))
    return out.reshape(bsz, SB_WIDTH)


def _chunk_conv(x_ref, xfull_ref, wconv_ref, rows):
    xfull_ref[CONV_PAD:CONV_PAD + rows, :] = x_ref[...]
    acc = wconv_ref[CONV_WIDTH - 1:CONV_WIDTH, :] * xfull_ref[CONV_PAD:CONV_PAD + rows, :]
    for j in range(1, CONV_WIDTH):
        acc = acc + wconv_ref[CONV_WIDTH - 1 - j:CONV_WIDTH - j, :] * xfull_ref[CONV_PAD - j:CONV_PAD - j + rows, :]
    return acc


def _run_interleaved(stage_generators):
    results = [None] * len(stage_generators)
    live = list(range(len(stage_generators)))
    while live:
        for j in list(live):
            try:
                next(stage_generators[j])
            except StopIteration as done:
                results[j] = done.value
                live.remove(j)
    return results


GDN_SEQS_PER_STEP = 4


def _mm_split(a_parts, b_parts):
    (ah, al), (bh, bl) = a_parts, b_parts
    return _dot(ah, bh) + (_dot(ah, bl) + _dot(al, bh))


def _same_head_mask():
    r2 = lax.broadcasted_iota(jnp.int32, (GDN_WIDTH, GDN_WIDTH), 0)
    c2 = lax.broadcasted_iota(jnp.int32, (GDN_WIDTH, GDN_WIDTH), 1)
    return r2, c2, (r2 // GDN_CHUNK) == (c2 // GDN_CHUNK)


def _gdn_prepare(c, qkv_ref, small_ref, wconv_ref, alog_ref, dtb_ref,
                 tril_ref, expb_ref, expg_ref, bd_ref, xfull_ref, *, n_chunks, tv_last):
    L = GDN_CHUNK
    W = GDN_WIDTH
    r2, c2, bd = _same_head_mask()
    xs = _silu(_chunk_conv(qkv_ref, xfull_ref, wconv_ref, L))
    q, k, v = xs[:, 0:W], xs[:, W:2 * W], xs[:, 2 * W:3 * W]
    yield
    small = small_ref[...]
    lane = lax.broadcasted_iota(jnp.int32, (L, LANE), 1)
    beta_s = jnp.where(lane < SMALL_DECAY, jax.nn.sigmoid(small), 0.0)
    g_s = jnp.where((lane >= SMALL_DECAY) & (lane < SMALL_DT),
                    -jnp.exp(alog_ref[...]) * _softplus(small + dtb_ref[...]), 0.0)
    if tv_last < L:
        last_rows = jnp.where(c == n_chunks - 1, tv_last, L)
        keep_w = lax.broadcasted_iota(jnp.int32, (L, W), 0) < last_rows
        keep_s = lax.broadcasted_iota(jnp.int32, (L, LANE), 0) < last_rows
        q, k, v = jnp.where(keep_w, q, 0.0), jnp.where(keep_w, k, 0.0), jnp.where(keep_w, v, 0.0)
        beta_s, g_s = jnp.where(keep_s, beta_s, 0.0), jnp.where(keep_s, g_s, 0.0)
    bdm = bd_ref[...]
    qn = q * lax.rsqrt(_dot3_left(q * q, bdm) + RMS_EPS) * (GDN_DK ** -0.5)
    kn = k * lax.rsqrt(_dot3_left(k * k, bdm) + RMS_EPS)
    gcs_s = _dot3_right(tril_ref[...], g_s)
    beta_e = _dot3_left(beta_s, expb_ref[...])
    yield
    gcs_e = _dot3_left(gcs_s, expg_ref[...])
    yield
    eg = jnp.exp(gcs_e)
    kb = kn * beta_e
    tile4 = lambda a: jnp.concatenate([a] * GDN_HEADS, axis=0)
    stack = lambda a: jnp.where(bd, tile4(a), 0.0)
    rl = lax.broadcasted_iota(jnp.int32, (L, W), 0)
    cl = lax.broadcasted_iota(jnp.int32, (L, W), 1)
    grow = jnp.sum(jnp.where((cl % L) == rl, gcs_e, 0.0), axis=0, keepdims=True)
    incl = bd & ((r2 % L) >= (c2 % L))
    decay = jnp.exp(jnp.where(incl, tile4(gcs_e) - grow, NEG_BIG))
    kns = stack(kn).astype(BF16)
    yield
    a_mat = jnp.where((r2 % L) != (c2 % L), _dot_nt(stack(kb).astype(BF16), kns) * decay, 0.0)
    qk = _dot_nt(stack(qn).astype(BF16), kns) * decay
    yield
    return dict(a=a_mat, qk=qk.astype(BF16), q_dec=(qn * eg).astype(BF16), kn=kn, gcs_e=gcs_e,
                rhs_v=_split2(stack(v * beta_e)), rhs_k=_split2(stack(kb * eg)))


def _gdn_inverses(a_mats):
    r2, c2, _ = _same_head_mask()
    eye = jnp.where(r2 == c2, 1.0, 0.0)
    ps = [eye - a for a in a_mats]
    a_parts = [_split2(a) for a in a_mats]
    x_parts = [_split2(_mm_split(ap, ap)) for ap in a_parts]
    ps = [p + _mm_split(_split2(p), xp) for p, xp in zip(ps, x_parts)]
    x_parts = [_split2(_mm_split(xp, xp)) for xp in x_parts]
    ps = [p + _mm_split(_split2(p), xp) for p, xp in zip(ps, x_parts)]
    xs = [xp[0] for xp in x_parts]
    for _ in range(3):
        xs = [_dot(x, x).astype(BF16) for x in xs]
        ps = [p + _dot(p.astype(BF16), x) for p, x in zip(ps, xs)]
    return [_split2(p) for p in ps]


def _gdn_finish(prep, p_parts, z_ref, gnorm_ref, bd_ref, o_ref, s_ref):
    L = GDN_CHUNK
    bd = _same_head_mask()[2]
    stack = lambda a: jnp.where(bd, jnp.concatenate([a] * GDN_HEADS, axis=0), 0.0)
    collapse = lambda m: (m[0:L] + m[L:2 * L]) + (m[2 * L:3 * L] + m[3 * L:4 * L])
    u_c = collapse(_mm_split(p_parts, prep["rhs_v"]))
    w_c = collapse(_mm_split(p_parts, prep["rhs_k"]))
    yield
    s = s_ref[...]
    sb = s.astype(BF16)
    v_new = u_c - _dot(w_c.astype(BF16), sb)
    yield
    o = _dot(prep["q_dec"], sb) + collapse(_dot(prep["qk"], stack(v_new).astype(BF16)))
    gcs_e = prep["gcs_e"]
    glast = gcs_e[L - 1:L, :]
    k_dec = prep["kn"] * jnp.exp(glast - gcs_e)
    s_ref[...] = s * jnp.exp(glast) + jnp.where(bd, _dot_tn(k_dec.astype(BF16), v_new.astype(BF16)), 0.0)
    yield
    on = o * lax.rsqrt(_dot3_left(o * o, bd_ref[...]) * (1.0 / HEAD_DIM) + RMS_EPS) * gnorm_ref[...]
    o_ref[...] = on * _silu(z_ref[...])


def _gdn_body(qkv_ref, z_ref, small_ref, wconv_ref, alog_ref, dtb_ref, gnorm_ref, conv0_ref, s0_ref,
              tril_ref, expb_ref, expg_ref, bd_ref, fold_ref, foldt_ref,
              o_ref, convn_ref, sn_ref, xfull_ref, s_ref, *, n_chunks, tv_last):
    L = GDN_CHUNK
    c = pl.program_id(1)
    seqs = range(GDN_SEQS_PER_STEP)

    @pl.when(c == 0)
    def _():
        bd = _same_head_mask()[2]
        for j in seqs:
            xfull_ref[j, 0:CONV_PAD, :] = conv0_ref[j]
            s_ref[j] = jnp.where(bd, _dot3_left(s0_ref[j], foldt_ref[...]), 0.0)

    preps = _run_interleaved([
        _gdn_prepare(c, qkv_ref.at[j], small_ref.at[j], wconv_ref, alog_ref, dtb_ref,
                     tril_ref, expb_ref, expg_ref, bd_ref, xfull_ref.at[j], n_chunks=n_chunks, tv_last=tv_last)
        for j in seqs])
    inverses = _gdn_inverses([prep["a"] for prep in preps])
    _run_interleaved([_gdn_finish(preps[j], inverses[j], z_ref.at[j], gnorm_ref, bd_ref, o_ref.at[j], s_ref.at[j])
                      for j in seqs])

    @pl.when(c == n_chunks - 1)
    def _():
        for j in seqs:
            convn_ref[j] = xfull_ref[j, CONV_PAD + tv_last - (CONV_WIDTH - 1):CONV_PAD + tv_last, :]
            sn_ref[j] = _dot3_left(s_ref[j], fold_ref[...])

    for j in seqs:
        xfull_ref[j, 0:CONV_PAD, :] = xfull_ref[j, L:L + CONV_PAD, :]


def _gdn_call(u, n_seq, rows_per_seq, t_valid, conv0, s0, w_conv, a_log, dt_bias, g_norm):
    L = GDN_CHUNK
    G = GDN_SEQS_PER_STEP
    n_chunks = -(-t_valid // L)
    tv_last = t_valid - (n_chunks - 1) * L
    lane_row = lambda vals, off: jnp.zeros((1, LANE), F32).at[0, off:off + vals.shape[0]].set(vals)
    conv0_p = jnp.pad(conv0, ((0, 0), (CONV_PAD - (CONV_WIDTH - 1), 0), (0, 0)))
    u3 = u.reshape(n_seq, rows_per_seq, U_WIDTH)
    row = lambda w, col: pl.BlockSpec((G, L, w), lambda b, c: (b, c, col))
    full = lambda r, cc: pl.BlockSpec((r, cc), lambda b, c: (0, 0))
    per_seq = lambda r, cc: pl.BlockSpec((G, r, cc), lambda b, c: (b, 0, 0))
    out = pl.pallas_call(
        functools.partial(_gdn_body, n_chunks=n_chunks, tv_last=tv_last),
        grid=(n_seq // G, n_chunks),
        in_specs=[row(GDN_CONV_DIM, OFF_QKVB // GDN_CONV_DIM), row(GDN_WIDTH, OFF_ZB // GDN_WIDTH),
                  row(LANE, OFF_SMALL // LANE),
                  full(CONV_WIDTH, GDN_CONV_DIM), full(1, LANE), full(1, LANE), full(1, GDN_WIDTH),
                  per_seq(CONV_PAD, GDN_CONV_DIM), per_seq(GDN_WIDTH, HEAD_DIM),
                  full(L, L), full(LANE, GDN_WIDTH), full(LANE, GDN_WIDTH), full(GDN_WIDTH, GDN_WIDTH),
                  full(GDN_WIDTH, HEAD_DIM), full(HEAD_DIM, GDN_WIDTH)],
        out_specs=[pl.BlockSpec((G, L, GDN_WIDTH), lambda b, c: (b, c, 0)),
                   per_seq(CONV_WIDTH - 1, GDN_CONV_DIM), per_seq(GDN_WIDTH, HEAD_DIM)],
        out_shape=[jax.ShapeDtypeStruct((n_seq, n_chunks * L, GDN_WIDTH), F32),
                   jax.ShapeDtypeStruct((n_seq, CONV_WIDTH - 1, GDN_CONV_DIM), F32),
                   jax.ShapeDtypeStruct((n_seq, GDN_WIDTH, HEAD_DIM), F32)],
        scratch_shapes=[pltpu.VMEM((G, L + CONV_PAD, GDN_CONV_DIM), F32),
                        pltpu.VMEM((G, GDN_WIDTH, GDN_WIDTH), F32)],
        compiler_params=_params(("parallel", "arbitrary")), name="gdn",
    )(u3, u3, u3, w_conv, lane_row(a_log, SMALL_DECAY), lane_row(dt_bias, SMALL_DECAY),
      jnp.tile(g_norm, GDN_HEADS).reshape(1, GDN_WIDTH), conv0_p, s0.reshape(n_seq, GDN_WIDTH, HEAD_DIM),
      _tril_incl(L), _expand_matrix(SMALL_BETA, GDN_HEADS, HEAD_DIM), _expand_matrix(SMALL_DECAY, GDN_HEADS, HEAD_DIM),
      _block_diag_ones(GDN_WIDTH, HEAD_DIM), _fold_matrix(GDN_HEADS, HEAD_DIM), _fold_matrix(GDN_HEADS, HEAD_DIM).T)
    o, conv_new, s_new = out
    return o.reshape(n_seq * n_chunks * L, GDN_WIDTH), conv_new, s_new.reshape(n_seq, GDN_HEADS, GDN_DK, HEAD_DIM)


SSD_SEQS_PER_STEP = 4


def _ssd_chunk(c, xbc_ref, z_ref, small_ref, wconv_ref, bconv_ref, alog_ref, dtb_ref, dskip_ref, gnorm_ref,
               tril_ref, expc_ref, o_ref, xfull_ref, ht_ref, *, n_chunks, tv_last):
    L = SSD_CHUNK
    W = SSD_WIDTH
    GW = W // SSD_GROUPS
    xs = _silu(_chunk_conv(xbc_ref, xfull_ref, wconv_ref, L) + bconv_ref[...])
    x_c, bm, cm = xs[:, 0:W], xs[:, W:W + GW], xs[:, W + GW:W + 2 * GW]
    yield
    small = small_ref[...]
    lane = lax.broadcasted_iota(jnp.int32, (L, LANE), 1)
    dt_s = jnp.where((lane >= SMALL_DT) & (lane < SMALL_DT + SSD_HEADS), _softplus(small + dtb_ref[...]), 0.0)
    if tv_last < L:
        last_rows = jnp.where(c == n_chunks - 1, tv_last, L)
        keep_s = lax.broadcasted_iota(jnp.int32, (L, LANE), 0) < last_rows
        keep_g = lax.broadcasted_iota(jnp.int32, (L, GW), 0) < last_rows
        dt_s = jnp.where(keep_s, dt_s, 0.0)
        bm, cm = jnp.where(keep_g, bm, 0.0), jnp.where(keep_g, cm, 0.0)
    ac_s = dt_s * (-jnp.exp(alog_ref[...]))
    acs_s = _dot3_right(tril_ref[...], ac_s)
    dt_e = _dot3_left(dt_s, expc_ref[...])
    yield
    acs_e = _dot3_left(acs_s, expc_ref[...])
    yield
    xdt = x_c * dt_e
    e = jnp.exp(acs_e)
    last = acs_e[L - 1:L, :]
    xw = xdt * jnp.exp(last - acs_e)
    dchunk = jnp.exp(last)
    acs_t = acs_s.T
    incl = lax.broadcasted_iota(jnp.int32, (L, L), 0) >= lax.broadcasted_iota(jnp.int32, (L, L), 1)
    first_half = lax.broadcasted_iota(jnp.int32, (L, LANE), 1) < HEAD_DIM
    ht = ht_ref[...]
    ys = []
    for g in range(SSD_GROUPS):
        gl = slice(g * GW, (g + 1) * GW)
        bg = bm[:, g * SSD_STATE:(g + 1) * SSD_STATE].astype(BF16)
        cg = cm[:, g * SSD_STATE:(g + 1) * SSD_STATE].astype(BF16)
        scores = _dot_nt(cg, bg)
        y_off = _dot(cg, ht[:, gl].astype(BF16)) * e[:, gl]
        ht_ref[:, gl] = ht[:, gl] * dchunk[:, gl] + _dot_tn(bg, xw[:, gl].astype(BF16))
        yield
        pairs = []
        for pp in range(GW // LANE):
            xp = xdt[:, g * GW + pp * LANE:g * GW + (pp + 1) * LANE]
            yp = jnp.zeros((L, LANE), F32)
            for hh in range(2):
                h = g * (SSD_HEADS // SSD_GROUPS) + pp * 2 + hh
                col = acs_s[:, SMALL_DT + h:SMALL_DT + h + 1]
                rowv = acs_t[SMALL_DT + h:SMALL_DT + h + 1, :]
                decay = jnp.exp(jnp.where(incl, col - rowv, NEG_BIG))
                xm = jnp.where(first_half if hh == 0 else jnp.logical_not(first_half), xp, 0.0)
                yp = yp + _dot((scores * decay).astype(BF16), xm.astype(BF16))
                yield
            pairs.append(yp)
        ys.append(jnp.concatenate(pairs, axis=1) + y_off)
    y = (jnp.concatenate(ys, axis=1) + dskip_ref[...] * x_c) * _silu(z_ref[...])
    outs = []
    for g in range(SSD_GROUPS):
        yg = y[:, g * GW:(g + 1) * GW]
        outs.append(yg * lax.rsqrt(jnp.mean(yg * yg, axis=-1, keepdims=True) + RMS_EPS))
    o_ref[...] = jnp.concatenate(outs, axis=1) * gnorm_ref[...]


def _ssd_body(xbc_ref, z_ref, small_ref, wconv_ref, bconv_ref, alog_ref, dtb_ref, dskip_ref, gnorm_ref,
              conv0_ref, h0_ref, tril_ref, expc_ref,
              o_ref, convn_ref, hn_ref, xfull_ref, ht_ref, *, n_chunks, tv_last):
    L = SSD_CHUNK
    c = pl.program_id(1)
    seqs = range(SSD_SEQS_PER_STEP)
    lane_blocks = range(SSD_WIDTH // LANE)

    @pl.when(c == 0)
    def _():
        for j in seqs:
            xfull_ref[j, 0:CONV_PAD, :] = conv0_ref[j]
            for i in lane_blocks:
                ht_ref[j, :, i * LANE:(i + 1) * LANE] = h0_ref[j, i * LANE:(i + 1) * LANE, :].T

    _run_interleaved([
        _ssd_chunk(c, xbc_ref.at[j], z_ref.at[j], small_ref.at[j], wconv_ref, bconv_ref, alog_ref, dtb_ref, dskip_ref,
                   gnorm_ref, tril_ref, expc_ref, o_ref.at[j], xfull_ref.at[j], ht_ref.at[j],
                   n_chunks=n_chunks, tv_last=tv_last)
        for j in seqs])

    @pl.when(c == n_chunks - 1)
    def _():
        for j in seqs:
            convn_ref[j] = xfull_ref[j, CONV_PAD + tv_last - (CONV_WIDTH - 1):CONV_PAD + tv_last, :]
            for i in lane_blocks:
                hn_ref[j, i * LANE:(i + 1) * LANE, :] = ht_ref[j, :, i * LANE:(i + 1) * LANE].T

    for j in seqs:
        xfull_ref[j, 0:CONV_PAD, :] = xfull_ref[j, L:L + CONV_PAD, :]


def _ssd_call(u, n_seq, rows_per_seq, t_valid, conv0, h0, w_conv, b_conv, a_log, dt_bias, d_skip, g_norm):
    L = SSD_CHUNK
    G = SSD_SEQS_PER_STEP
    n_chunks = -(-t_valid // L)
    tv_last = t_valid - (n_chunks - 1) * L
    lane_row = lambda vals, off: jnp.zeros((1, LANE), F32).at[0, off:off + vals.shape[0]].set(vals)
    conv0_p = jnp.pad(conv0, ((0, 0), (CONV_PAD - (CONV_WIDTH - 1), 0), (0, 0)))
    u3 = u.reshape(n_seq, rows_per_seq, U_WIDTH)
    row = lambda w, col: pl.BlockSpec((G, L, w), lambda b, c: (b, c, col))
    full = lambda r, cc: pl.BlockSpec((r, cc), lambda b, c: (0, 0))
    per_seq = lambda r, cc: pl.BlockSpec((G, r, cc), lambda b, c: (b, 0, 0))
    out = pl.pallas_call(
        functools.partial(_ssd_body, n_chunks=n_chunks, tv_last=tv_last),
        grid=(n_seq // G, n_chunks),
        in_specs=[row(SSD_XBC, OFF_XBC // SSD_XBC), row(SSD_WIDTH, OFF_ZC // SSD_WIDTH), row(LANE, OFF_SMALL // LANE),
                  full(CONV_WIDTH, SSD_XBC), full(1, SSD_XBC), full(1, LANE), full(1, LANE),
                  full(1, SSD_WIDTH), full(1, SSD_WIDTH),
                  per_seq(CONV_PAD, SSD_XBC), per_seq(SSD_WIDTH, SSD_STATE),
                  full(L, L), full(LANE, SSD_WIDTH)],
        out_specs=[pl.BlockSpec((G, L, SSD_WIDTH), lambda b, c: (b, c, 0)),
                   per_seq(CONV_WIDTH - 1, SSD_XBC), per_seq(SSD_WIDTH, SSD_STATE)],
        out_shape=[jax.ShapeDtypeStruct((n_seq, n_chunks * L, SSD_WIDTH), F32),
                   jax.ShapeDtypeStruct((n_seq, CONV_WIDTH - 1, SSD_XBC), F32),
                   jax.ShapeDtypeStruct((n_seq, SSD_WIDTH, SSD_STATE), F32)],
        scratch_shapes=[pltpu.VMEM((G, L + CONV_PAD, SSD_XBC), F32), pltpu.VMEM((G, SSD_STATE, SSD_WIDTH), F32)],
        compiler_params=_params(("parallel", "arbitrary")), name="ssd",
    )(u3, u3, u3, w_conv, b_conv.reshape(1, SSD_XBC), lane_row(a_log, SMALL_DT), lane_row(dt_bias, SMALL_DT),
      jnp.repeat(d_skip, HEAD_DIM).reshape(1, SSD_WIDTH), g_norm.reshape(1, SSD_WIDTH),
      conv0_p, h0.reshape(n_seq, SSD_WIDTH, SSD_STATE), _tril_incl(L), _expand_matrix(SMALL_DT, SSD_HEADS, HEAD_DIM))
    o, conv_new, h_new = out
    return (o.reshape(n_seq * n_chunks * L, SSD_WIDTH), conv_new,
            h_new.reshape(n_seq, SSD_HEADS, HEAD_DIM, SSD_STATE))


ROUTE_GATE, ROUTE_IDX, ROUTE_RANK = 0, TOP_K, 2 * TOP_K


def _strict_lower(n):
    return jnp.asarray(np.tril(np.ones((n, n), bool), -1), BF16)


def _proj_out_body(x_ref, oa_ref, ob_ref, oc_ref, w_ref, g_ref, b_ref, rw_ref, rb_ref, lower_ref,
                   x1_ref, route_ref, count_ref):
    mix = _dot(oa_ref[...].astype(BF16), w_ref[0:256, :])
    mix += _dot(ob_ref[...].astype(BF16), w_ref[256:512, :])
    mix += _dot(oc_ref[...].astype(BF16), w_ref[512:1024, :])
    x1 = _layer_norm_f32(DEEPNORM_ALPHA * x_ref[...] + mix, g_ref[...], b_ref[...])
    x1_ref[...] = x1
    logits = _mm3(x1, rw_ref[...]) + rb_ref[...]

    @pl.when(pl.program_id(0) == 0)
    def _():
        count_ref[...] = jnp.zeros_like(count_ref)

    lane = lax.broadcasted_iota(jnp.int32, logits.shape, 1)
    lg = jnp.where(lane < N_EXPERTS, logits, -jnp.inf)
    vals, idxs = [], []
    for _ in range(TOP_K):
        m = jnp.max(lg, axis=1, keepdims=True)
        idx = jnp.min(jnp.where(lg == m, lane, LANE), axis=1, keepdims=True)
        vals.append(m)
        idxs.append(idx)
        lg = jnp.where(lane == idx, -jnp.inf, lg)
    exps = [jnp.exp(v - vals[0]) for v in vals]
    denom = exps[0] + exps[1] + exps[2] + exps[3]
    hot = [lane == idx for idx in idxs]
    multi = jnp.where(hot[0] | hot[1] | hot[2] | hot[3], 1.0, 0.0)
    before = _dot(lower_ref[...], multi.astype(BF16)) + count_ref[...]
    route = jnp.zeros(logits.shape, F32)
    for k in range(TOP_K):
        rank = jnp.sum(jnp.where(hot[k], before, 0.0), axis=1, keepdims=True)
        route = jnp.where(lane == ROUTE_GATE + k, exps[k] / denom, route)
        route = jnp.where(lane == ROUTE_IDX + k, idxs[k].astype(F32), route)
        route = jnp.where(lane == ROUTE_RANK + k, rank, route)
    route_ref[...] = route
    count_ref[...] += jnp.sum(multi, axis=0, keepdims=True)


def _proj_out_call(x, o_a, o_b, o_c, w_bf16, g, b, router_w, router_b):
    n, d = x.shape
    bm = _row_block(n)
    rw = jnp.pad(router_w, ((0, 0), (0, LANE - N_EXPERTS)))
    rb = jnp.pad(router_b, (0, LANE - N_EXPERTS)).reshape(1, LANE)
    row = lambda w: pl.BlockSpec((bm, w), lambda i: (i, 0))
    full = lambda r, c: pl.BlockSpec((r, c), lambda i: (0, 0))
    return pl.pallas_call(
        _proj_out_body, grid=(n // bm,),
        in_specs=[row(d), row(SB_WIDTH), row(GDN_WIDTH), row(SSD_WIDTH), full(d, d), full(1, d), full(1, d),
                  full(d, LANE), full(1, LANE), full(bm, bm)],
        out_specs=[row(d), row(LANE), full(1, LANE)],
        out_shape=[jax.ShapeDtypeStruct((n, d), F32), jax.ShapeDtypeStruct((n, LANE), F32),
                   jax.ShapeDtypeStruct((1, LANE), F32)],
        compiler_params=_params(("arbitrary",)), name="proj_out",
    )(x, o_a, o_b, o_c, w_bf16, g.reshape(1, d), b.reshape(1, d), rw, rb, _strict_lower(bm))


def _moe_body(be_ref, nb_ref, rs_ref, rows_ref, wg_ref, bg_ref, wu_ref, bu_ref, wd_ref, bd_ref, o_ref, wg_s, wu_s, wd_s):
    i = pl.program_id(0)
    prev = be_ref[jnp.maximum(i - 1, 0)]

    @pl.when(jnp.logical_or(i == 0, be_ref[i] != prev))
    def _():
        wg_s[...] = wg_ref[...].astype(BF16)
        wu_s[...] = wu_ref[...].astype(BF16)
        wd_s[...] = wd_ref[...].astype(BF16)

    @pl.when(i < nb_ref[0])
    def _():
        xb = rows_ref[...].astype(BF16)
        glu = jnp.minimum(_dot(xb, wg_s[...]) + bg_ref[...], SWIGLU_LIMIT)
        lin = jnp.clip(_dot(xb, wu_s[...]) + bu_ref[...], -SWIGLU_LIMIT, SWIGLU_LIMIT)
        act = glu * jax.nn.sigmoid(SWIGLU_ALPHA * glu) * (lin + 1.0)
        o_ref[...] = _dot(act.astype(BF16), wd_s[...]) + bd_ref[...]

    @pl.when(i >= nb_ref[0])
    def _():
        o_ref[...] = jnp.zeros_like(o_ref)


def _moe_call(rows_in, block_expert, n_used, row_start, w_gate, b_gate, w_up, b_up, w_down, b_down, layer, blk):
    d = rows_in.shape[1]
    n_blocks = block_expert.shape[0]
    n_rows = n_blocks * blk
    wspec = lambda r, c: pl.BlockSpec((None, None, r, c), lambda i, be, nb, rs: (layer, be[i], 0, 0))
    grid_spec = pltpu.PrefetchScalarGridSpec(
        num_scalar_prefetch=3, grid=(n_blocks,),
        in_specs=[pl.BlockSpec((pl.Element(blk), pl.Element(d)), lambda i, be, nb, rs: (rs[i] * ROW_ALIGN, 0)),
                  wspec(d, D_EXPERT), wspec(1, D_EXPERT), wspec(d, D_EXPERT), wspec(1, D_EXPERT),
                  wspec(D_EXPERT, d), wspec(1, d)],
        out_specs=pl.BlockSpec((blk, d), lambda i, be, nb, rs: (i, 0)),
        scratch_shapes=[pltpu.VMEM((d, D_EXPERT), BF16), pltpu.VMEM((d, D_EXPERT), BF16),
                        pltpu.VMEM((D_EXPERT, d), BF16)])
    depth = w_gate.shape[0]
    return pl.pallas_call(
        _moe_body, grid_spec=grid_spec,
        out_shape=jax.ShapeDtypeStruct((n_rows, d), F32),
        compiler_params=_params(("arbitrary",)), name="moe_experts",
    )(block_expert, n_used, row_start, rows_in, w_gate, b_gate.reshape(depth, N_EXPERTS, 1, D_EXPERT),
      w_up, b_up.reshape(depth, N_EXPERTS, 1, D_EXPERT), w_down, b_down.reshape(depth, N_EXPERTS, 1, d))


def _route(route, counts, blk):
    n_tok = route.shape[0]
    gate = route[:, ROUTE_GATE:ROUTE_GATE + TOP_K]
    expert = route[:, ROUTE_IDX:ROUTE_IDX + TOP_K].astype(jnp.int32)
    rank = route[:, ROUTE_RANK:ROUTE_RANK + TOP_K].astype(jnp.int32)
    counts = counts[0, :N_EXPERTS].astype(jnp.int32)
    n_assign = n_tok * TOP_K
    n_blocks = -(-(n_assign + N_EXPERTS * (blk - 1 + ROW_ALIGN - 1)) // blk)
    start = jnp.cumsum(counts) - counts
    lead = start % ROW_ALIGN
    padded = (lead + counts + blk - 1) // blk * blk
    padded_end = jnp.cumsum(padded)
    padded_start = padded_end - padded
    per_expert = lambda table: jnp.sum(
        jnp.where(expert[..., None] == jnp.arange(N_EXPERTS, dtype=jnp.int32), table, 0), axis=-1, dtype=jnp.int32)
    dest = per_expert(padded_start + lead) + rank
    token = jnp.broadcast_to(jnp.arange(n_tok, dtype=jnp.int32)[:, None], (n_tok, TOP_K))
    block_first = jnp.arange(n_blocks, dtype=jnp.int32) * blk
    block_expert = jnp.minimum(jnp.sum(padded_end[None, :] <= block_first[:, None], axis=1, dtype=jnp.int32),
                               N_EXPERTS - 1)
    if n_assign <= SMALL_ROUTE_ASSIGNMENTS:
        position = (per_expert(start) + rank).reshape(1, -1)
        hit = position == jnp.arange(n_assign, dtype=jnp.int32)[:, None]
        token_sorted = jnp.sum(jnp.where(hit, token.reshape(1, -1), 0), axis=1, dtype=jnp.int32)
    else:
        _, token_sorted = lax.sort((dest.reshape(-1), token.reshape(-1)), num_keys=1)
    src = jnp.concatenate([token_sorted, jnp.zeros((blk,), jnp.int32)])
    dest_t = dest.T.reshape(-1)
    n_used = (padded_end[-1] // blk).astype(jnp.int32)
    row_start = jnp.where(jnp.arange(n_blocks, dtype=jnp.int32) < n_used,
                          (start - lead)[block_expert] + block_first - padded_start[block_expert], 0)
    return gate, src, dest_t, block_expert, n_used.reshape(1), row_start // ROW_ALIGN


def _post_body(x1_ref, r0_ref, r1_ref, r2_ref, r3_ref, gate_ref, p_ref, g_ref, b_ref, wg_ref, bg_ref, wp_ref, o_ref):
    gate = gate_ref[...]
    ffn = gate[:, 0:1] * r0_ref[...]
    for k, r_ref in ((1, r1_ref), (2, r2_ref), (3, r3_ref)):
        ffn += gate[:, k:k + 1] * r_ref[...]
    x2 = _layer_norm_f32(DEEPNORM_ALPHA * x1_ref[...] + ffn, g_ref[...], b_ref[...])
    gate_logit = _dot(x2.astype(BF16), wg_ref[...]) + bg_ref[...]
    emb = _dot(p_ref[...].astype(BF16), wp_ref[...])
    o_ref[...] = x2 + jax.nn.sigmoid(gate_logit) * emb


def _post_call(x1, rows_k, gate, p, g, b, wg_bf16, bg, wp_bf16):
    n, d = x1.shape
    bm = _row_block(n)
    nb = n // bm
    row = lambda w: pl.BlockSpec((bm, w), lambda i: (i, 0))
    plane = lambda k: pl.BlockSpec((bm, d), lambda i: (k * nb + i, 0))
    full = lambda r, c: pl.BlockSpec((r, c), lambda i: (0, 0))
    return pl.pallas_call(
        _post_body, grid=(nb,),
        in_specs=[row(d)] + [plane(k) for k in range(TOP_K)] + [row(TOP_K), row(PLE_DIM), full(1, d), full(1, d),
                                                                 full(d, d), full(1, d), full(PLE_DIM, d)],
        out_specs=row(d),
        out_shape=jax.ShapeDtypeStruct((n, d), F32),
        compiler_params=_params(("parallel",)), name="post_ffn",
    )(x1, rows_k, rows_k, rows_k, rows_k, gate, p, g.reshape(1, d), b.reshape(1, d), wg_bf16, bg.reshape(1, d),
      wp_bf16)


def kernel(x_prompt, x_sample, cache_k_a, cache_v_a, state_conv_b, state_delta_b, state_conv_c, state_ssm_c, page_table, p_prompt, p_sample, ln_in_g, ln_in_b, w_in, sb_bias, w_conv_b, a_log_b, dt_bias_b, g_norm_b, w_conv_c, b_conv_c, a_log_c, dt_bias_c, d_skip_c, g_norm_c, w_out, ln1_g, ln1_b, router_w, router_b, w_gate, b_gate, w_up, b_up, w_down, b_down, ln2_g, ln2_b, ple_gate_w, ple_gate_b, ple_proj_w):
    depth = w_in.shape[0]
    pad_cols =U_WIDTH - sum(hi - lo for lo, hi in _IN_SEGMENTS)
    w_in_r = jnp.concatenate([w_in[:, :, lo:hi] for lo, hi in _IN_SEGMENTS]
                             + [jnp.zeros(w_in.shape[:2] + (pad_cols,), w_in.dtype)], axis=-1).astype(BF16)
    w_out_b = w_out.astype(BF16)
    ple_gate_w_b = ple_gate_w.astype(BF16)
    ple_proj_w_b = ple_proj_w.astype(BF16)
    n_pool = cache_k_a.shape[1]
    cache_k = cache_k_a.transpose(0, 1, 3, 4, 2).reshape(depth, n_pool, SB_WIDTH, PAGE_SIZE)
    cache_v = cache_v_a.transpose(0, 1, 3, 4, 2).reshape(depth, n_pool, SB_WIDTH, PAGE_SIZE)

    def mixers_and_router(i, x, bsz, t, sample):
        n = bsz * t
        u = _proj_in_call(x, w_in_r[i])
        k_a = u[:, OFF_K:OFF_K + SB_WIDTH].reshape(bsz, t, SB_HEADS, HEAD_DIM)
        v_a = u[:, OFF_V:OFF_V + SB_WIDTH].reshape(bsz, t, SB_HEADS, HEAD_DIM)
        if sample:
            o_a = _sb_decode_call(u[:, OFF_Q:OFF_Q + SB_WIDTH], sb_bias[i], cache_k, cache_v, page_table, i)
            conv_b, delta_b, conv_c, ssm_c = state_conv_b[i], state_delta_b[i], state_conv_c[i], state_ssm_c[i]
            rows_per_seq = SSD_CHUNK
            u_mix = jnp.pad(u.reshape(bsz, t, U_WIDTH), ((0, 0), (0, rows_per_seq - t), (0, 0)))
            u_mix = u_mix.reshape(bsz * rows_per_seq, U_WIDTH)
        else:
            o_a = _sb_prefill_call(u, sb_bias[i], bsz, t)
            conv_b = jnp.zeros((bsz, CONV_WIDTH - 1, GDN_CONV_DIM), F32)
            delta_b = jnp.zeros((bsz, GDN_HEADS, GDN_DK, HEAD_DIM), F32)
            conv_c = jnp.zeros((bsz, CONV_WIDTH - 1, SSD_XBC), F32)
            ssm_c = jnp.zeros((bsz, SSD_HEADS, HEAD_DIM, SSD_STATE), F32)
            rows_per_seq = t
            u_mix = u
        o_b, conv_b_new, delta_new = _gdn_call(u_mix, bsz, rows_per_seq, t, conv_b, delta_b, w_conv_b[i],
                                               a_log_b[i], dt_bias_b[i], g_norm_b[i])
        o_c, conv_c_new, ssm_new = _ssd_call(u_mix, bsz, rows_per_seq, t, conv_c, ssm_c, w_conv_c[i], b_conv_c[i],
                                             a_log_c[i], dt_bias_c[i], d_skip_c[i], g_norm_c[i])
        if sample:
            o_b = o_b.reshape(bsz, -1, GDN_WIDTH)[:, :t].reshape(n, GDN_WIDTH)
            o_c = o_c.reshape(bsz, -1, SSD_WIDTH)[:, :t].reshape(n, SSD_WIDTH)
        x1, route, counts = _proj_out_call(x, o_a, o_b, o_c, w_out_b[i], ln1_g[i], ln1_b[i], router_w[i], router_b[i])
        blk = MOE_BLOCK_PROMPT if n * TOP_K >= MOE_BLOCK_PROMPT * N_EXPERTS else MOE_BLOCK_SMALL
        gate, src, dest_t, block_expert, n_used, row_start = _route(route, counts, blk)
        return (x1, gate, dest_t, block_expert, n_used, row_start, blk, x1[src]), (k_a, v_a, conv_b_new, delta_new,
                                                                                   conv_c_new, ssm_new)

    def experts(i, routed):
        x1, gate, dest_t, block_expert, n_used, row_start, blk, rows_in = routed
        rows_out = _moe_call(rows_in, block_expert, n_used, row_start, w_gate, b_gate, w_up, b_up, w_down, b_down, i,
                             blk)
        return rows_out[dest_t]

    def combine(i, routed, rows_k, p):
        x1, gate = routed[0], routed[1]
        return _post_call(x1, rows_k, gate, p[i].reshape(x1.shape[0], PLE_DIM), ln2_g[i], ln2_b[i], ple_gate_w_b[i],
                          ple_gate_b[i], ple_proj_w_b[i])

    groups = [(x_prompt, p_prompt, False), (x_sample, p_sample, True)]
    xs = [_ln_call(x.reshape(-1, x.shape[-1]), ln_in_g, ln_in_b) for x, _, _ in groups]
    states = [[], []]
    for i in range(depth):
        routed = []
        for gi, (x, _, sample) in enumerate(groups):
            r, st = mixers_and_router(i, xs[gi], x.shape[0], x.shape[1], sample)
            routed.append(r)
            states[gi].append(st)
        rows_k = [experts(i, r) for r in routed]
        xs = [combine(i, routed[gi], rows_k[gi], groups[gi][1]) for gi in range(len(groups))]
    y_prompt, y_sample = (xs[gi].reshape(groups[gi][0].shape) for gi in range(len(groups)))
    new_prompt, new_sample = ([jnp.stack([s[j] for s in st]) for j in range(6)] for st in states)
    return (y_prompt, y_sample, *new_prompt, *new_sample)
```

```python
import functools

import jax
import jax.numpy as jnp
import numpy as np
from jax import lax
from jax.experimental import pallas as pl
from jax.experimental.pallas import tpu as pltpu

F32 = jnp.float32
BF16 = jnp.bfloat16

D_MODEL = 1024
DEPTH = 4
PAGE_SIZE = 128
HEAD_DIM = 64
CONV_WIDTH = 4
SB_WIDTH = 256
SB_HEADS = 4
GDN_WIDTH = 256
GDN_HEADS = 4
GDN_DK = 64
GDN_CONV_DIM = 768
GDN_CHUNK = 64
SSD_WIDTH = 512
SSD_HEADS = 8
SSD_GROUPS = 2
SSD_STATE = 128
SSD_XBC = 1024
SSD_CHUNK = 128
N_EXPERTS = 32
TOP_K = 4
D_EXPERT = 512
SWIGLU_ALPHA = 1.702
SWIGLU_LIMIT = 7.0
PLE_DIM = 256
DEEPNORM_ALPHA = (2 * DEPTH) ** 0.25
LN_EPS = 1e-5
RMS_EPS = 1e-6

_REF_WIDTHS = (256, 256, 256, 768, 256, 4, 4, 512, 1024, 8)
_REF_OFF = tuple(int(v) for v in np.cumsum((0,) + _REF_WIDTHS))
OFF_XBC, OFF_ZC, OFF_QKVB, OFF_Q, OFF_K, OFF_V, OFF_ZB, OFF_SMALL = 0, 1024, 1536, 2304, 2560, 2816, 3072, 3328
U_WIDTH = 3456
LANE = 128
SMALL_BETA, SMALL_DECAY, SMALL_DT = 0, GDN_HEADS, 2 * GDN_HEADS
_IN_SEGMENTS = (
    (_REF_OFF[8], _REF_OFF[9]),
    (_REF_OFF[7], _REF_OFF[8]),
    (_REF_OFF[3], _REF_OFF[4]),
    (_REF_OFF[0], _REF_OFF[3]),
    (_REF_OFF[4], _REF_OFF[5]),
    (_REF_OFF[5], _REF_OFF[7]),
    (_REF_OFF[9], _REF_OFF[10]),
)

VMEM_LIMIT = 48 * 1024 * 1024
MOE_BLOCK_PROMPT = 512
MOE_BLOCK_SMALL = 8
SMALL_ROUTE_ASSIGNMENTS = 1024
ROW_ALIGN = 8
SB_BLOCK = 256
DEC_PAGES_PER_STEP = 16
CONV_PAD = 8
NEG_BIG = -1e30
LOG2E = 1.4426950408889634


def _params(sem):
    return pltpu.CompilerParams(dimension_semantics=sem, vmem_limit_bytes=VMEM_LIMIT)


def _row_block(n):
    return n if n < 512 else 512


def _layer_norm_f32(x, g, b):
    mu = jnp.mean(x, axis=-1, keepdims=True)
    xc = x - mu
    var = jnp.mean(xc * xc, axis=-1, keepdims=True)
    return xc * lax.rsqrt(var + LN_EPS) * g + b


def _softplus(z):
    return jnp.maximum(z, 0.0) + jnp.log1p(jnp.exp(-jnp.abs(z)))


def _softplus2(z2):
    return jnp.maximum(z2, 0.0) + jnp.log2(1.0 + jnp.exp2(-jnp.abs(z2)))


def _silu(x):
    return x * jax.nn.sigmoid(x)


def _dot(a, b):
    return jnp.dot(a, b, preferred_element_type=F32)


def _dot_nt(a, b):
    return lax.dot_general(a, b, (((1,), (1,)), ((), ())), preferred_element_type=F32)


def _dot_tn(a, b):
    return lax.dot_general(a, b, (((0,), (0,)), ((), ())), preferred_element_type=F32)


def _split2(x):
    hi = x.astype(BF16)
    return hi, (x - hi.astype(F32)).astype(BF16)


def _split3(x):
    hi = x.astype(BF16)
    r = x - hi.astype(F32)
    mid = r.astype(BF16)
    return hi, mid, (r - mid.astype(F32)).astype(BF16)


def _split_dot(x, w_bf16):
    hi, lo = _split2(x)
    return _dot(hi, w_bf16) + _dot(lo, w_bf16)


def _dot3_left(x, w_bf16):
    hi, mid, lo = _split3(x)
    return _dot(hi, w_bf16) + (_dot(mid, w_bf16) + _dot(lo, w_bf16))


def _dot3_right(w_bf16, x):
    hi, mid, lo = _split3(x)
    return _dot(w_bf16, hi) + (_dot(w_bf16, mid) + _dot(w_bf16, lo))


def _mm3(a, b):
    ah, al = _split2(a)
    bh, bl = _split2(b)
    return _dot(ah, bh) + (_dot(ah, bl) + _dot(al, bh))


def _cumsum_matrix(n):
    j = np.arange(n)[:, None]
    s = np.arange(n)[None, :]
    return jnp.asarray(np.concatenate([(j > s), np.ones((n, LANE), bool)], axis=1), BF16)


def _tril_incl(n):
    return jnp.asarray(np.tril(np.ones((n, n), bool)), BF16)


def _expand_matrix(first_lane, heads, width):
    m = np.zeros((LANE, heads * width), bool)
    for h in range(heads):
        m[first_lane + h, h * width:(h + 1) * width] = True
    return jnp.asarray(m, BF16)


def _block_diag_ones(n, blk):
    i = np.arange(n)
    return jnp.asarray((i[:, None] // blk) == (i[None, :] // blk), BF16)


def _fold_matrix(heads, width):
    i = np.arange(heads * width)
    return jnp.asarray((i[:, None] % width) == np.arange(width)[None, :], BF16)


def _ln_body(x_ref, g_ref, b_ref, o_ref):
    o_ref[...] = _layer_norm_f32(x_ref[...], g_ref[...], b_ref[...])


def _ln_call(x, g, b):
    n, d = x.shape
    bm = _row_block(n)
    return pl.pallas_call(
        _ln_body, grid=(n // bm,),
        in_specs=[pl.BlockSpec((bm, d), lambda i: (i, 0)),
                  pl.BlockSpec((1, d), lambda i: (0, 0)), pl.BlockSpec((1, d), lambda i: (0, 0))],
        out_specs=pl.BlockSpec((bm, d), lambda i: (i, 0)),
        out_shape=jax.ShapeDtypeStruct((n, d), F32),
        compiler_params=_params(("parallel",)), name="ln_in",
    )(x, g.reshape(1, d), b.reshape(1, d))


_PROJ_CHUNK = 384


def _proj_in_body(x_ref, w_ref, o_ref):
    xb = x_ref[...].astype(BF16)
    for j in range(0, U_WIDTH, _PROJ_CHUNK):
        o_ref[:, j:j + _PROJ_CHUNK] = _dot(xb, w_ref[:, j:j + _PROJ_CHUNK])


def _proj_in_call(x, w_bf16):
    n, d = x.shape
    bm = _row_block(n)
    return pl.pallas_call(
        _proj_in_body, grid=(n // bm,),
        in_specs=[pl.BlockSpec((bm, d), lambda i: (i, 0)),
                  pl.BlockSpec((d, U_WIDTH), lambda i: (0, 0))],
        out_specs=pl.BlockSpec((bm, U_WIDTH), lambda i: (i, 0)),
        out_shape=jax.ShapeDtypeStruct((n, U_WIDTH), F32),
        compiler_params=_params(("parallel",)), name="proj_in",
    )(x, w_bf16)


def _sb_prefill_body(bias_ref, q_ref, k_ref, v_ref, u_ref, o_ref, qh_ref, acc_ref, carry_ref):
    blk = q_ref.shape[0]
    i = pl.program_id(1)
    q = q_ref[...] * (HEAD_DIM ** -0.5 * LOG2E)
    lane_head = lax.broadcasted_iota(jnp.int32, (1, SB_WIDTH), 1) // HEAD_DIM
    for h in range(SB_HEADS):
        qh_ref[h] = jnp.where(lane_head == h, q, 0.0).astype(BF16)
    acc_ref[...] = jnp.zeros_like(acc_ref)
    carry_ref[...] = jnp.zeros_like(carry_ref)
    rows = lax.broadcasted_iota(jnp.int32, (blk, blk), 0)
    cols = lax.broadcasted_iota(jnp.int32, (blk, blk), 1)

    def key_block(j, diagonal):
        start = pl.multiple_of(j * blk, blk)
        kj = k_ref[pl.ds(start, blk), :].astype(BF16)
        vj = v_ref[pl.ds(start, blk), :].astype(BF16)
        heads = range(SB_HEADS)
        visible = cols < rows
        zs = [_dot_nt(qh_ref[h], kj) + bias_ref[h] for h in heads]
        log_keeps = [-_softplus2(z) for z in zs]
        logits = [z + lk for z, lk in zip(zs, log_keeps)]
        if diagonal:
            log_keeps = [jnp.where(visible, lk, 0.0) for lk in log_keeps]
        lts = [_split_dot(lk, u_ref[...]) for lk in log_keeps]
        carries = [carry_ref[h] for h in heads]
        ws = [jnp.exp2(logits[h] + (lts[h][:, :blk] + jnp.concatenate([carries[h]] * (blk // LANE), axis=1)))
              for h in heads]
        if diagonal:
            ws = [jnp.where(visible, w, 0.0) for w in ws]
        for h in heads:
            acc_ref[h] += _dot(ws[h].astype(BF16), vj)
            carry_ref[h] = carries[h] + lts[h][:, blk:]

    key_block(i, True)

    def earlier(t, carry):
        key_block(i - 1 - t, False)
        return carry

    lax.fori_loop(0, i, earlier, 0)
    out = jnp.zeros((blk, SB_WIDTH), F32)
    for h in range(SB_HEADS):
        out = out + jnp.where(lane_head == h, acc_ref[h], 0.0)
    o_ref[...] = out


def _sb_prefill_call(u, bias, bsz, t):
    blk = SB_BLOCK
    nq = t // blk
    grid_spec = pltpu.PrefetchScalarGridSpec(
        num_scalar_prefetch=0, grid=(bsz, nq),
        in_specs=[pl.BlockSpec(memory_space=pltpu.SMEM),
                  pl.BlockSpec((blk, SB_WIDTH), lambda b, i: (b * nq + i, OFF_Q // SB_WIDTH)),
                  pl.BlockSpec((t, SB_WIDTH), lambda b, i: (b, OFF_K // SB_WIDTH)),
                  pl.BlockSpec((t, SB_WIDTH), lambda b, i: (b, OFF_V // SB_WIDTH)),
                  pl.BlockSpec((blk, blk + LANE), lambda b, i: (0, 0))],
        out_specs=pl.BlockSpec((blk, SB_WIDTH), lambda b, i: (b * nq + i, 0)),
        scratch_shapes=[pltpu.VMEM((SB_HEADS, blk, SB_WIDTH), BF16),
                        pltpu.VMEM((SB_HEADS, blk, SB_WIDTH), F32),
                        pltpu.VMEM((SB_HEADS, blk, LANE), F32)])
    return pl.pallas_call(
        _sb_prefill_body, grid_spec=grid_spec,
        out_shape=jax.ShapeDtypeStruct((bsz * t, SB_WIDTH), F32),
        compiler_params=_params(("parallel", "arbitrary")), name="sb_prefill",
    )(bias * LOG2E, u, u, u, _cumsum_matrix(blk))


def _sb_decode_body(pt_ref, q_ref, bias_ref, u_ref, *refs):
    g = DEC_PAGES_PER_STEP
    k_refs, v_refs = refs[:g], refs[g:2 * g]
    o_ref, acc_ref, carry_ref = refs[2 * g], refs[2 * g + 1], refs[2 * g + 2]
    s = pl.program_id(1)

    @pl.when(s == 0)
    def _():
        acc_ref[...] = jnp.zeros_like(acc_ref)
        carry_ref[...] = jnp.zeros_like(carry_ref)

    row = lax.broadcasted_iota(jnp.int32, (8, SB_WIDTH), 0)
    lane_head = lax.broadcasted_iota(jnp.int32, (8, SB_WIDTH), 1) // HEAD_DIM
    own = row == lane_head
    qm = jnp.where(own, q_ref[...] * (HEAD_DIM ** -0.5 * LOG2E), 0.0).astype(BF16)
    acc = acc_ref[...]
    carry = carry_ref[...]
    zs = [_dot(qm, k_refs[c][...].astype(BF16)) + bias_ref[...] for c in range(g)]
    log_keeps = [-_softplus2(z) for z in zs]
    lt_all = _split_dot(jnp.concatenate(log_keeps, axis=0), u_ref[...])
    ws = []
    for c in range(g):
        lt = lt_all[8 * c:8 * c + 8]
        ws.append(jnp.exp2(zs[c] + log_keeps[c] + lt[:, :LANE] + carry).astype(BF16))
        carry = carry + lt[:, LANE:]
    for c in range(g):
        acc = acc + _dot_nt(ws[c], v_refs[c][...].astype(BF16))
    acc_ref[...] = acc
    carry_ref[...] = carry

    @pl.when(s == pl.num_programs(1) - 1)
    def _():
        o_ref[...] = jnp.sum(jnp.where(own, acc, 0.0), axis=0, keepdims=True)


def _sb_decode_call(q, bias, cache_k, cache_v, page_table, layer):
    bsz = q.shape[0]
    n_pages = page_table.shape[1]
    g = DEC_PAGES_PER_STEP
    n_steps = n_pages // g
    bias8 = jnp.zeros((8, LANE), F32).at[:SB_HEADS].set(jnp.broadcast_to(bias[:, None] * LOG2E, (SB_HEADS, LANE)))

    def page_spec(c):
        return pl.BlockSpec((None, None, SB_WIDTH, PAGE_SIZE),
                            lambda b, s, pt: (layer, pt[b, n_pages - 1 - (s * g + c)], 0, 0))

    grid_spec = pltpu.PrefetchScalarGridSpec(
        num_scalar_prefetch=1, grid=(bsz, n_steps),
        in_specs=[pl.BlockSpec((None, 1, SB_WIDTH), lambda b, s, pt: (b, 0, 0)),
                  pl.BlockSpec((8, LANE), lambda b, s, pt: (0, 0)),
                  pl.BlockSpec((PAGE_SIZE, 2 * LANE), lambda b, s, pt: (0, 0))]
                 + [page_spec(c) for c in range(g)] + [page_spec(c) for c in range(g)],
        out_specs=pl.BlockSpec((None, 1, SB_WIDTH), lambda b, s, pt: (b, 0, 0)),
        scratch_shapes=[pltpu.VMEM((8, SB_WIDTH), F32), pltpu.VMEM((8, LANE), F32)])
    out = pl.pallas_call(
        _sb_decode_body, grid_spec=grid_spec,
        out_shape=jax.ShapeDtypeStruct((bsz, 1, SB_WIDTH), F32),
        compiler_params=_params(("parallel", "arbitrary")), name="sb_decode",
    )(page_table, q.reshape(bsz, 1, SB_WIDTH), bias8, _cumsum_matrix(PAGE_SIZE),
      *([cache_k] * g), *([cache_v] * g))
    return out.reshape(bsz, SB_WIDTH)


def _chunk_conv(x_ref, xfull_ref, wconv_ref, rows):
    xfull_ref[CONV_PAD:CONV_PAD + rows, :] = x_ref[...]
    acc = wconv_ref[CONV_WIDTH - 1:CONV_WIDTH, :] * xfull_ref[CONV_PAD:CONV_PAD + rows, :]
    for j in range(1, CONV_WIDTH):
        acc = acc + wconv_ref[CONV_WIDTH - 1 - j:CONV_WIDTH - j, :] * xfull_ref[CONV_PAD - j:CONV_PAD - j + rows, :]
    return acc


def _run_interleaved(stage_generators):
    results = [None] * len(stage_generators)
    live = list(range(len(stage_generators)))
    while live:
        for j in list(live):
            try:
                next(stage_generators[j])
            except StopIteration as done:
                results[j] = done.value
                live.remove(j)
    return results


GDN_SEQS_PER_STEP = 8


def _mm_split(a_parts, b_parts):
    (ah, al), (bh, bl) = a_parts, b_parts
    return _dot(ah, bh) + (_dot(ah, bl) + _dot(al, bh))


def _same_head_mask():
    r2 = lax.broadcasted_iota(jnp.int32, (GDN_WIDTH, GDN_WIDTH), 0)
    c2 = lax.broadcasted_iota(jnp.int32, (GDN_WIDTH, GDN_WIDTH), 1)
    return r2, c2, (r2 // GDN_CHUNK) == (c2 // GDN_CHUNK)


def _gdn_prepare(c, qkv_ref, small_ref, wconv_ref, alog_ref, dtb_ref,
                 tril_ref, expb_ref, expg_ref, bd_ref, xfull_ref, *, n_chunks, tv_last):
    L = GDN_CHUNK
    W = GDN_WIDTH
    r2, c2, bd = _same_head_mask()
    xs = _silu(_chunk_conv(qkv_ref, xfull_ref, wconv_ref, L))
    q, k, v = xs[:, 0:W], xs[:, W:2 * W], xs[:, 2 * W:3 * W]
    yield
    small = small_ref[...]
    lane = lax.broadcasted_iota(jnp.int32, (L, LANE), 1)
    beta_s = jnp.where(lane < SMALL_DECAY, jax.nn.sigmoid(small), 0.0)
    g_s = jnp.where((lane >= SMALL_DECAY) & (lane < SMALL_DT),
                    -jnp.exp(alog_ref[...]) * _softplus(small + dtb_ref[...]), 0.0)
    if tv_last < L:
        last_rows = jnp.where(c == n_chunks - 1, tv_last, L)
        keep_w = lax.broadcasted_iota(jnp.int32, (L, W), 0) < last_rows
        keep_s = lax.broadcasted_iota(jnp.int32, (L, LANE), 0) < last_rows
        q, k, v = jnp.where(keep_w, q, 0.0), jnp.where(keep_w, k, 0.0), jnp.where(keep_w, v, 0.0)
        beta_s, g_s = jnp.where(keep_s, beta_s, 0.0), jnp.where(keep_s, g_s, 0.0)
    bdm = bd_ref[...]
    qn = q * lax.rsqrt(_dot3_left(q * q, bdm) + RMS_EPS) * (GDN_DK ** -0.5)
    kn = k * lax.rsqrt(_dot3_left(k * k, bdm) + RMS_EPS)
    gcs_s = _dot3_right(tril_ref[...], g_s)
    beta_e = _dot3_left(beta_s, expb_ref[...])
    yield
    gcs_e = _dot3_left(gcs_s, expg_ref[...])
    yield
    eg = jnp.exp(gcs_e)
    kb = kn * beta_e
    tile4 = lambda a: jnp.concatenate([a] * GDN_HEADS, axis=0)
    stack = lambda a: jnp.where(bd, tile4(a), 0.0)
    rl = lax.broadcasted_iota(jnp.int32, (L, W), 0)
    cl = lax.broadcasted_iota(jnp.int32, (L, W), 1)
    grow = jnp.sum(jnp.where((cl % L) == rl, gcs_e, 0.0), axis=0, keepdims=True)
    incl = bd & ((r2 % L) >= (c2 % L))
    decay = jnp.exp(jnp.where(incl, tile4(gcs_e) - grow, NEG_BIG))
    kns = stack(kn).astype(BF16)
    yield
    a_mat = jnp.where((r2 % L) != (c2 % L), _dot_nt(stack(kb).astype(BF16), kns) * decay, 0.0)
    qk = _dot_nt(stack(qn).astype(BF16), kns) * decay
    yield
    return dict(a=a_mat, qk=qk.astype(BF16), q_dec=(qn * eg).astype(BF16), kn=kn, gcs_e=gcs_e,
                rhs_v=_split2(stack(v * beta_e)), rhs_k=_split2(stack(kb * eg)))


def _gdn_inverses(a_mats):
    r2, c2, _ = _same_head_mask()
    eye = jnp.where(r2 == c2, 1.0, 0.0)
    ps = [eye - a for a in a_mats]
    a_parts = [_split2(a) for a in a_mats]
    x_parts = [_split2(_mm_split(ap, ap)) for ap in a_parts]
    ps = [p + _mm_split(_split2(p), xp) for p, xp in zip(ps, x_parts)]
    x_parts = [_split2(_mm_split(xp, xp)) for xp in x_parts]
    ps = [p + _mm_split(_split2(p), xp) for p, xp in zip(ps, x_parts)]
    xs = [xp[0] for xp in x_parts]
    for _ in range(3):
        xs = [_dot(x, x).astype(BF16) for x in xs]
        ps = [p + _dot(p.astype(BF16), x) for p, x in zip(ps, xs)]
    return [_split2(p) for p in ps]


def _gdn_finish(prep, p_parts, z_ref, gnorm_ref, bd_ref, o_ref, s_ref):
    L = GDN_CHUNK
    bd = _same_head_mask()[2]
    stack = lambda a: jnp.where(bd, jnp.concatenate([a] * GDN_HEADS, axis=0), 0.0)
    collapse = lambda m: (m[0:L] + m[L:2 * L]) + (m[2 * L:3 * L] + m[3 * L:4 * L])
    u_c = collapse(_mm_split(p_parts, prep["rhs_v"]))
    w_c = collapse(_mm_split(p_parts, prep["rhs_k"]))
    yield
    s = s_ref[...]
    sb = s.astype(BF16)
    v_new = u_c - _dot(w_c.astype(BF16), sb)
    yield
    o = _dot(prep["q_dec"], sb) + collapse(_dot(prep["qk"], stack(v_new).astype(BF16)))
    gcs_e = prep["gcs_e"]
    glast = gcs_e[L - 1:L, :]
    k_dec = prep["kn"] * jnp.exp(glast - gcs_e)
    s_ref[...] = s * jnp.exp(glast) + jnp.where(bd, _dot_tn(k_dec.astype(BF16), v_new.astype(BF16)), 0.0)
    yield
    on = o * lax.rsqrt(_dot3_left(o * o, bd_ref[...]) * (1.0 / HEAD_DIM) + RMS_EPS) * gnorm_ref[...]
    o_ref[...] = on * _silu(z_ref[...])


def _gdn_body(qkv_ref, z_ref, small_ref, wconv_ref, alog_ref, dtb_ref, gnorm_ref, conv0_ref, s0_ref,
              tril_ref, expb_ref, expg_ref, bd_ref, fold_ref, foldt_ref,
              o_ref, convn_ref, sn_ref, xfull_ref, s_ref, *, n_chunks, tv_last):
    L = GDN_CHUNK
    c = pl.program_id(1)
    seqs = range(GDN_SEQS_PER_STEP)

    @pl.when(c == 0)
    def _():
        bd = _same_head_mask()[2]
        for j in seqs:
            xfull_ref[j, 0:CONV_PAD, :] = conv0_ref[j]
            s_ref[j] = jnp.where(bd, _dot3_left(s0_ref[j], foldt_ref[...]), 0.0)

    preps = _run_interleaved([
        _gdn_prepare(c, qkv_ref.at[j], small_ref.at[j], wconv_ref, alog_ref, dtb_ref,
                     tril_ref, expb_ref, expg_ref, bd_ref, xfull_ref.at[j], n_chunks=n_chunks, tv_last=tv_last)
        for j in seqs])
    inverses = _gdn_inverses([prep["a"] for prep in preps])
    _run_interleaved([_gdn_finish(preps[j], inverses[j], z_ref.at[j], gnorm_ref, bd_ref, o_ref.at[j], s_ref.at[j])
                      for j in seqs])

    @pl.when(c == n_chunks - 1)
    def _():
        for j in seqs:
            convn_ref[j] = xfull_ref[j, CONV_PAD + tv_last - (CONV_WIDTH - 1):CONV_PAD + tv_last, :]
            sn_ref[j] = _dot3_left(s_ref[j], fold_ref[...])

    for j in seqs:
        xfull_ref[j, 0:CONV_PAD, :] = xfull_ref[j, L:L + CONV_PAD, :]


def _gdn_call(u, n_seq, rows_per_seq, t_valid, conv0, s0, w_conv, a_log, dt_bias, g_norm):
    L = GDN_CHUNK
    G = GDN_SEQS_PER_STEP
    n_chunks = -(-t_valid // L)
    tv_last = t_valid - (n_chunks - 1) * L
    lane_row = lambda vals, off: jnp.zeros((1, LANE), F32).at[0, off:off + vals.shape[0]].set(vals)
    conv0_p = jnp.pad(conv0, ((0, 0), (CONV_PAD - (CONV_WIDTH - 1), 0), (0, 0)))
    u3 = u.reshape(n_seq, rows_per_seq, U_WIDTH)
    row = lambda w, col: pl.BlockSpec((G, L, w), lambda b, c: (b, c, col))
    full = lambda r, cc: pl.BlockSpec((r, cc), lambda b, c: (0, 0))
    per_seq = lambda r, cc: pl.BlockSpec((G, r, cc), lambda b, c: (b, 0, 0))
    out = pl.pallas_call(
        functools.partial(_gdn_body, n_chunks=n_chunks, tv_last=tv_last),
        grid=(n_seq // G, n_chunks),
        in_specs=[row(GDN_CONV_DIM, OFF_QKVB // GDN_CONV_DIM), row(GDN_WIDTH, OFF_ZB // GDN_WIDTH),
                  row(LANE, OFF_SMALL // LANE),
                  full(CONV_WIDTH, GDN_CONV_DIM), full(1, LANE), full(1, LANE), full(1, GDN_WIDTH),
                  per_seq(CONV_PAD, GDN_CONV_DIM), per_seq(GDN_WIDTH, HEAD_DIM),
                  full(L, L), full(LANE, GDN_WIDTH), full(LANE, GDN_WIDTH), full(GDN_WIDTH, GDN_WIDTH),
                  full(GDN_WIDTH, HEAD_DIM), full(HEAD_DIM, GDN_WIDTH)],
        out_specs=[pl.BlockSpec((G, L, GDN_WIDTH), lambda b, c: (b, c, 0)),
                   per_seq(CONV_WIDTH - 1, GDN_CONV_DIM), per_seq(GDN_WIDTH, HEAD_DIM)],
        out_shape=[jax.ShapeDtypeStruct((n_seq, n_chunks * L, GDN_WIDTH), F32),
                   jax.ShapeDtypeStruct((n_seq, CONV_WIDTH - 1, GDN_CONV_DIM), F32),
                   jax.ShapeDtypeStruct((n_seq, GDN_WIDTH, HEAD_DIM), F32)],
        scratch_shapes=[pltpu.VMEM((G, L + CONV_PAD, GDN_CONV_DIM), F32),
                        pltpu.VMEM((G, GDN_WIDTH, GDN_WIDTH), F32)],
        compiler_params=_params(("parallel", "arbitrary")), name="gdn",
    )(u3, u3, u3, w_conv, lane_row(a_log, SMALL_DECAY), lane_row(dt_bias, SMALL_DECAY),
      jnp.tile(g_norm, GDN_HEADS).reshape(1, GDN_WIDTH), conv0_p, s0.reshape(n_seq, GDN_WIDTH, HEAD_DIM),
      _tril_incl(L), _expand_matrix(SMALL_BETA, GDN_HEADS, HEAD_DIM), _expand_matrix(SMALL_DECAY, GDN_HEADS, HEAD_DIM),
      _block_diag_ones(GDN_WIDTH, HEAD_DIM), _fold_matrix(GDN_HEADS, HEAD_DIM), _fold_matrix(GDN_HEADS, HEAD_DIM).T)
    o, conv_new, s_new = out
    return o.reshape(n_seq * n_chunks * L, GDN_WIDTH), conv_new, s_new.reshape(n_seq, GDN_HEADS, GDN_DK, HEAD_DIM)


SSD_SEQS_PER_STEP = 8


def _ssd_chunk(c, xbc_ref, z_ref, small_ref, wconv_ref, bconv_ref, alog_ref, dtb_ref, dskip_ref, gnorm_ref,
               tril_ref, expc_ref, o_ref, xfull_ref, ht_ref, *, n_chunks, tv_last):
    L = SSD_CHUNK
    W = SSD_WIDTH
    GW = W // SSD_GROUPS
    xs = _silu(_chunk_conv(xbc_ref, xfull_ref, wconv_ref, L) + bconv_ref[...])
    x_c, bm, cm = xs[:, 0:W], xs[:, W:W + GW], xs[:, W + GW:W + 2 * GW]
    yield
    small = small_ref[...]
    lane = lax.broadcasted_iota(jnp.int32, (L, LANE), 1)
    dt_s = jnp.where((lane >= SMALL_DT) & (lane < SMALL_DT + SSD_HEADS), _softplus(small + dtb_ref[...]), 0.0)
    if tv_last < L:
        last_rows = jnp.where(c == n_chunks - 1, tv_last, L)
        keep_s = lax.broadcasted_iota(jnp.int32, (L, LANE), 0) < last_rows
        keep_g = lax.broadcasted_iota(jnp.int32, (L, GW), 0) < last_rows
        dt_s = jnp.where(keep_s, dt_s, 0.0)
        bm, cm = jnp.where(keep_g, bm, 0.0), jnp.where(keep_g, cm, 0.0)
    ac_s = dt_s * (-jnp.exp(alog_ref[...]))
    acs_s = _dot3_right(tril_ref[...], ac_s)
    dt_e = _dot3_left(dt_s, expc_ref[...])
    yield
    acs_e = _dot3_left(acs_s, expc_ref[...])
    yield
    xdt = x_c * dt_e
    e = jnp.exp(acs_e)
    last = acs_e[L - 1:L, :]
    xw = xdt * jnp.exp(last - acs_e)
    dchunk = jnp.exp(last)
    acs_t = acs_s.T
    incl = lax.broadcasted_iota(jnp.int32, (L, L), 0) >= lax.broadcasted_iota(jnp.int32, (L, L), 1)
    first_half = lax.broadcasted_iota(jnp.int32, (L, LANE), 1) < HEAD_DIM
    ht = ht_ref[...]
    ys = []
    for g in range(SSD_GROUPS):
        gl = slice(g * GW, (g + 1) * GW)
        bg = bm[:, g * SSD_STATE:(g + 1) * SSD_STATE].astype(BF16)
        cg = cm[:, g * SSD_STATE:(g + 1) * SSD_STATE].astype(BF16)
        scores = _dot_nt(cg, bg)
        y_off = _dot(cg, ht[:, gl].astype(BF16)) * e[:, gl]
        ht_ref[:, gl] = ht[:, gl] * dchunk[:, gl] + _dot_tn(bg, xw[:, gl].astype(BF16))
        yield
        pairs = []
        for pp in range(GW // LANE):
            xp = xdt[:, g * GW + pp * LANE:g * GW + (pp + 1) * LANE]
            yp = jnp.zeros((L, LANE), F32)
            for hh in range(2):
                h = g * (SSD_HEADS // SSD_GROUPS) + pp * 2 + hh
                col = acs_s[:, SMALL_DT + h:SMALL_DT + h + 1]
                rowv = acs_t[SMALL_DT + h:SMALL_DT + h + 1, :]
                decay = jnp.exp(jnp.where(incl, col - rowv, NEG_BIG))
                xm = jnp.where(first_half if hh == 0 else jnp.logical_not(first_half), xp, 0.0)
                yp = yp + _dot((scores * decay).astype(BF16), xm.astype(BF16))
                yield
            pairs.append(yp)
        ys.append(jnp.concatenate(pairs, axis=1) + y_off)
    y = (jnp.concatenate(ys, axis=1) + dskip_ref[...] * x_c) * _silu(z_ref[...])
    outs = []
    for g in range(SSD_GROUPS):
        yg = y[:, g * GW:(g + 1) * GW]
        outs.append(yg * lax.rsqrt(jnp.mean(yg * yg, axis=-1, keepdims=True) + RMS_EPS))
    o_ref[...] = jnp.concatenate(outs, axis=1) * gnorm_ref[...]


def _ssd_body(xbc_ref, z_ref, small_ref, wconv_ref, bconv_ref, alog_ref, dtb_ref, dskip_ref, gnorm_ref,
              conv0_ref, h0_ref, tril_ref, expc_ref,
              o_ref, convn_ref, hn_ref, xfull_ref, ht_ref, *, n_chunks, tv_last):
    L = SSD_CHUNK
    c = pl.program_id(1)
    seqs = range(SSD_SEQS_PER_STEP)
    lane_blocks = range(SSD_WIDTH // LANE)

    @pl.when(c == 0)
    def _():
        for j in seqs:
            xfull_ref[j, 0:CONV_PAD, :] = conv0_ref[j]
            for i in lane_blocks:
                ht_ref[j, :, i * LANE:(i + 1) * LANE] = h0_ref[j, i * LANE:(i + 1) * LANE, :].T

    _run_interleaved([
        _ssd_chunk(c, xbc_ref.at[j], z_ref.at[j], small_ref.at[j], wconv_ref, bconv_ref, alog_ref, dtb_ref, dskip_ref,
                   gnorm_ref, tril_ref, expc_ref, o_ref.at[j], xfull_ref.at[j], ht_ref.at[j],
                   n_chunks=n_chunks, tv_last=tv_last)
        for j in seqs])

    @pl.when(c == n_chunks - 1)
    def _():
        for j in seqs:
            convn_ref[j] = xfull_ref[j, CONV_PAD + tv_last - (CONV_WIDTH - 1):CONV_PAD + tv_last, :]
            for i in lane_blocks:
                hn_ref[j, i * LANE:(i + 1) * LANE, :] = ht_ref[j, :, i * LANE:(i + 1) * LANE].T

    for j in seqs:
        xfull_ref[j, 0:CONV_PAD, :] = xfull_ref[j, L:L + CONV_PAD, :]


def _ssd_call(u, n_seq, rows_per_seq, t_valid, conv0, h0, w_conv, b_conv, a_log, dt_bias, d_skip, g_norm):
    L = SSD_CHUNK
    G = SSD_SEQS_PER_STEP
    n_chunks = -(-t_valid // L)
    tv_last = t_valid - (n_chunks - 1) * L
    lane_row = lambda vals, off: jnp.zeros((1, LANE), F32).at[0, off:off + vals.shape[0]].set(vals)
    conv0_p = jnp.pad(conv0, ((0, 0), (CONV_PAD - (CONV_WIDTH - 1), 0), (0, 0)))
    u3 = u.reshape(n_seq, rows_per_seq, U_WIDTH)
    row = lambda w, col: pl.BlockSpec((G, L, w), lambda b, c: (b, c, col))
    full = lambda r, cc: pl.BlockSpec((r, cc), lambda b, c: (0, 0))
    per_seq = lambda r, cc: pl.BlockSpec((G, r, cc), lambda b, c: (b, 0, 0))
    out = pl.pallas_call(
        functools.partial(_ssd_body, n_chunks=n_chunks, tv_last=tv_last),
        grid=(n_seq // G, n_chunks),
        in_specs=[row(SSD_XBC, OFF_XBC // SSD_XBC), row(SSD_WIDTH, OFF_ZC // SSD_WIDTH), row(LANE, OFF_SMALL // LANE),
                  full(CONV_WIDTH, SSD_XBC), full(1, SSD_XBC), full(1, LANE), full(1, LANE),
                  full(1, SSD_WIDTH), full(1, SSD_WIDTH),
                  per_seq(CONV_PAD, SSD_XBC), per_seq(SSD_WIDTH, SSD_STATE),
                  full(L, L), full(LANE, SSD_WIDTH)],
        out_specs=[pl.BlockSpec((G, L, SSD_WIDTH), lambda b, c: (b, c, 0)),
                   per_seq(CONV_WIDTH - 1, SSD_XBC), per_seq(SSD_WIDTH, SSD_STATE)],
        out_shape=[jax.ShapeDtypeStruct((n_seq, n_chunks * L, SSD_WIDTH), F32),
                   jax.ShapeDtypeStruct((n_seq, CONV_WIDTH - 1, SSD_XBC), F32),
                   jax.ShapeDtypeStruct((n_seq, SSD_WIDTH, SSD_STATE), F32)],
        scratch_shapes=[pltpu.VMEM((G, L + CONV_PAD, SSD_XBC), F32), pltpu.VMEM((G, SSD_STATE, SSD_WIDTH), F32)],
        compiler_params=_params(("parallel", "arbitrary")), name="ssd",
    )(u3, u3, u3, w_conv, b_conv.reshape(1, SSD_XBC), lane_row(a_log, SMALL_DT), lane_row(dt_bias, SMALL_DT),
      jnp.repeat(d_skip, HEAD_DIM).reshape(1, SSD_WIDTH), g_norm.reshape(1, SSD_WIDTH),
      conv0_p, h0.reshape(n_seq, SSD_WIDTH, SSD_STATE), _tril_incl(L), _expand_matrix(SMALL_DT, SSD_HEADS, HEAD_DIM))
    o, conv_new, h_new = out
    return (o.reshape(n_seq * n_chunks * L, SSD_WIDTH), conv_new,
            h_new.reshape(n_seq, SSD_HEADS, HEAD_DIM, SSD_STATE))


ROUTE_GATE, ROUTE_IDX, ROUTE_RANK = 0, TOP_K, 2 * TOP_K


def _strict_lower(n):
    return jnp.asarray(np.tril(np.ones((n, n), bool), -1), BF16)


def _proj_out_body(x_ref, oa_ref, ob_ref, oc_ref, w_ref, g_ref, b_ref, rw_ref, rb_ref, lower_ref,
                   x1_ref, route_ref, count_ref):
    mix = _dot(oa_ref[...].astype(BF16), w_ref[0:256, :])
    mix += _dot(ob_ref[...].astype(BF16), w_ref[256:512, :])
    mix += _dot(oc_ref[...].astype(BF16), w_ref[512:1024, :])
    x1 = _layer_norm_f32(DEEPNORM_ALPHA * x_ref[...] + mix, g_ref[...], b_ref[...])
    x1_ref[...] = x1
    logits = _mm3(x1, rw_ref[...]) + rb_ref[...]

    @pl.when(pl.program_id(0) == 0)
    def _():
        count_ref[...] = jnp.zeros_like(count_ref)

    lane = lax.broadcasted_iota(jnp.int32, logits.shape, 1)
    lg = jnp.where(lane < N_EXPERTS, logits, -jnp.inf)
    vals, idxs = [], []
    for _ in range(TOP_K):
        m = jnp.max(lg, axis=1, keepdims=True)
        idx = jnp.min(jnp.where(lg == m, lane, LANE), axis=1, keepdims=True)
        vals.append(m)
        idxs.append(idx)
        lg = jnp.where(lane == idx, -jnp.inf, lg)
    exps = [jnp.exp(v - vals[0]) for v in vals]
    denom = exps[0] + exps[1] + exps[2] + exps[3]
    hot = [lane == idx for idx in idxs]
    multi = jnp.where(hot[0] | hot[1] | hot[2] | hot[3], 1.0, 0.0)
    before = _dot(lower_ref[...], multi.astype(BF16)) + count_ref[...]
    route = jnp.zeros(logits.shape, F32)
    for k in range(TOP_K):
        rank = jnp.sum(jnp.where(hot[k], before, 0.0), axis=1, keepdims=True)
        route = jnp.where(lane == ROUTE_GATE + k, exps[k] / denom, route)
        route = jnp.where(lane == ROUTE_IDX + k, idxs[k].astype(F32), route)
        route = jnp.where(lane == ROUTE_RANK + k, rank, route)
    route_ref[...] = route
    count_ref[...] += jnp.sum(multi, axis=0, keepdims=True)


def _proj_out_call(x, o_a, o_b, o_c, w_bf16, g, b, router_w, router_b):
    n, d = x.shape
    bm = _row_block(n)
    rw = jnp.pad(router_w, ((0, 0), (0, LANE - N_EXPERTS)))
    rb = jnp.pad(router_b, (0, LANE - N_EXPERTS)).reshape(1, LANE)
    row = lambda w: pl.BlockSpec((bm, w), lambda i: (i, 0))
    full = lambda r, c: pl.BlockSpec((r, c), lambda i: (0, 0))
    return pl.pallas_call(
        _proj_out_body, grid=(n // bm,),
        in_specs=[row(d), row(SB_WIDTH), row(GDN_WIDTH), row(SSD_WIDTH), full(d, d), full(1, d), full(1, d),
                  full(d, LANE), full(1, LANE), full(bm, bm)],
        out_specs=[row(d), row(LANE), full(1, LANE)],
        out_shape=[jax.ShapeDtypeStruct((n, d), F32), jax.ShapeDtypeStruct((n, LANE), F32),
                   jax.ShapeDtypeStruct((1, LANE), F32)],
        compiler_params=_params(("arbitrary",)), name="proj_out",
    )(x, o_a, o_b, o_c, w_bf16, g.reshape(1, d), b.reshape(1, d), rw, rb, _strict_lower(bm))


def _moe_body(be_ref, nb_ref, rs_ref, rows_ref, wg_ref, bg_ref, wu_ref, bu_ref, wd_ref, bd_ref, o_ref, wg_s, wu_s, wd_s):
    i = pl.program_id(0)
    prev = be_ref[jnp.maximum(i - 1, 0)]

    @pl.when(jnp.logical_or(i == 0, be_ref[i] != prev))
    def _():
        wg_s[...] = wg_ref[...].astype(BF16)
        wu_s[...] = wu_ref[...].astype(BF16)
        wd_s[...] = wd_ref[...].astype(BF16)

    @pl.when(i < nb_ref[0])
    def _():
        xb = rows_ref[...].astype(BF16)
        glu = jnp.minimum(_dot(xb, wg_s[...]) + bg_ref[...], SWIGLU_LIMIT)
        lin = jnp.clip(_dot(xb, wu_s[...]) + bu_ref[...], -SWIGLU_LIMIT, SWIGLU_LIMIT)
        act = glu * jax.nn.sigmoid(SWIGLU_ALPHA * glu) * (lin + 1.0)
        o_ref[...] = _dot(act.astype(BF16), wd_s[...]) + bd_ref[...]

    @pl.when(i >= nb_ref[0])
    def _():
        o_ref[...] = jnp.zeros_like(o_ref)


def _moe_call(rows_in, block_expert, n_used, row_start, w_gate, b_gate, w_up, b_up, w_down, b_down, layer, blk):
    d = rows_in.shape[1]
    n_blocks = block_expert.shape[0]
    n_rows = n_blocks * blk
    wspec = lambda r, c: pl.BlockSpec((None, None, r, c), lambda i, be, nb, rs: (layer, be[i], 0, 0))
    grid_spec = pltpu.PrefetchScalarGridSpec(
        num_scalar_prefetch=3, grid=(n_blocks,),
        in_specs=[pl.BlockSpec((pl.Element(blk), pl.Element(d)), lambda i, be, nb, rs: (rs[i] * ROW_ALIGN, 0)),
                  wspec(d, D_EXPERT), wspec(1, D_EXPERT), wspec(d, D_EXPERT), wspec(1, D_EXPERT),
                  wspec(D_EXPERT, d), wspec(1, d)],
        out_specs=pl.BlockSpec((blk, d), lambda i, be, nb, rs: (i, 0)),
        scratch_shapes=[pltpu.VMEM((d, D_EXPERT), BF16), pltpu.VMEM((d, D_EXPERT), BF16),
                        pltpu.VMEM((D_EXPERT, d), BF16)])
    depth = w_gate.shape[0]
    return pl.pallas_call(
        _moe_body, grid_spec=grid_spec,
        out_shape=jax.ShapeDtypeStruct((n_rows, d), F32),
        compiler_params=_params(("arbitrary",)), name="moe_experts",
    )(block_expert, n_used, row_start, rows_in, w_gate, b_gate.reshape(depth, N_EXPERTS, 1, D_EXPERT),
      w_up, b_up.reshape(depth, N_EXPERTS, 1, D_EXPERT), w_down, b_down.reshape(depth, N_EXPERTS, 1, d))


def _route(route, counts, blk):
    n_tok = route.shape[0]
    gate = route[:, ROUTE_GATE:ROUTE_GATE + TOP_K]
    expert = route[:, ROUTE_IDX:ROUTE_IDX + TOP_K].astype(jnp.int32)
    rank = route[:, ROUTE_RANK:ROUTE_RANK + TOP_K].astype(jnp.int32)
    counts = counts[0, :N_EXPERTS].astype(jnp.int32)
    n_assign = n_tok * TOP_K
    n_blocks = -(-(n_assign + N_EXPERTS * (blk - 1 + ROW_ALIGN - 1)) // blk)
    start = jnp.cumsum(counts) - counts
    lead = start % ROW_ALIGN
    padded = (lead + counts + blk - 1) // blk * blk
    padded_end = jnp.cumsum(padded)
    padded_start = padded_end - padded
    per_expert = lambda table: jnp.sum(
        jnp.where(expert[..., None] == jnp.arange(N_EXPERTS, dtype=jnp.int32), table, 0), axis=-1, dtype=jnp.int32)
    dest = per_expert(padded_start + lead) + rank
    token = jnp.broadcast_to(jnp.arange(n_tok, dtype=jnp.int32)[:, None], (n_tok, TOP_K))
    block_first = jnp.arange(n_blocks, dtype=jnp.int32) * blk
    block_expert = jnp.minimum(jnp.sum(padded_end[None, :] <= block_first[:, None], axis=1, dtype=jnp.int32),
                               N_EXPERTS - 1)
    if n_assign <= SMALL_ROUTE_ASSIGNMENTS:
        position = (per_expert(start) + rank).reshape(1, -1)
        hit = position == jnp.arange(n_assign, dtype=jnp.int32)[:, None]
        token_sorted = jnp.sum(jnp.where(hit, token.reshape(1, -1), 0), axis=1, dtype=jnp.int32)
    else:
        _, token_sorted = lax.sort((dest.reshape(-1), token.reshape(-1)), num_keys=1)
    src = jnp.concatenate([token_sorted, jnp.zeros((blk,), jnp.int32)])
    dest_t = dest.T.reshape(-1)
    n_used = (padded_end[-1] // blk).astype(jnp.int32)
    row_start = jnp.where(jnp.arange(n_blocks, dtype=jnp.int32) < n_used,
                          (start - lead)[block_expert] + block_first - padded_start[block_expert], 0)
    return gate, src, dest_t, block_expert, n_used.reshape(1), row_start // ROW_ALIGN


def _post_body(x1_ref, r0_ref, r1_ref, r2_ref, r3_ref, gate_ref, p_ref, g_ref, b_ref, wg_ref, bg_ref, wp_ref, o_ref):
    gate = gate_ref[...]
    ffn = gate[:, 0:1] * r0_ref[...]
    for k, r_ref in ((1, r1_ref), (2, r2_ref), (3, r3_ref)):
        ffn += gate[:, k:k + 1] * r_ref[...]
    x2 = _layer_norm_f32(DEEPNORM_ALPHA * x1_ref[...] + ffn, g_ref[...], b_ref[...])
    gate_logit = _dot(x2.astype(BF16), wg_ref[...]) + bg_ref[...]
    emb = _dot(p_ref[...].astype(BF16), wp_ref[...])
    o_ref[...] = x2 + jax.nn.sigmoid(gate_logit) * emb


def _post_call(x1, rows_k, gate, p, g, b, wg_bf16, bg, wp_bf16):
    n, d = x1.shape
    bm = _row_block(n)
    nb = n // bm
    row = lambda w: pl.BlockSpec((bm, w), lambda i: (i, 0))
    plane = lambda k: pl.BlockSpec((bm, d), lambda i: (k * nb + i, 0))
    full = lambda r, c: pl.BlockSpec((r, c), lambda i: (0, 0))
    return pl.pallas_call(
        _post_body, grid=(nb,),
        in_specs=[row(d)] + [plane(k) for k in range(TOP_K)] + [row(TOP_K), row(PLE_DIM), full(1, d), full(1, d),
                                                                 full(d, d), full(1, d), full(PLE_DIM, d)],
        out_specs=row(d),
        out_shape=jax.ShapeDtypeStruct((n, d), F32),
        compiler_params=_params(("parallel",)), name="post_ffn",
    )(x1, rows_k, rows_k, rows_k, rows_k, gate, p, g.reshape(1, d), b.reshape(1, d), wg_bf16, bg.reshape(1, d),
      wp_bf16)


def kernel(x_prompt, x_sample, cache_k_a, cache_v_a, state_conv_b, state_delta_b, state_conv_c, state_ssm_c, page_table, p_prompt, p_sample, ln_in_g, ln_in_b, w_in, sb_bias, w_conv_b, a_log_b, dt_bias_b, g_norm_b, w_conv_c, b_conv_c, a_log_c, dt_bias_c, d_skip_c, g_norm_c, w_out, ln1_g, ln1_b, router_w, router_b, w_gate, b_gate, w_up, b_up, w_down, b_down, ln2_g, ln2_b, ple_gate_w, ple_gate_b, ple_proj_w):
    depth = w_in.shape[0]
    pad_cols =U_WIDTH - sum(hi - lo for lo, hi in _IN_SEGMENTS)
    w_in_r = jnp.concatenate([w_in[:, :, lo:hi] for lo, hi in _IN_SEGMENTS]
                             + [jnp.zeros(w_in.shape[:2] + (pad_cols,), w_in.dtype)], axis=-1).astype(BF16)
    w_out_b = w_out.astype(BF16)
    ple_gate_w_b = ple_gate_w.astype(BF16)
    ple_proj_w_b = ple_proj_w.astype(BF16)
    n_pool = cache_k_a.shape[1]
    cache_k = cache_k_a.transpose(0, 1, 3, 4, 2).reshape(depth, n_pool, SB_WIDTH, PAGE_SIZE)
    cache_v = cache_v_a.transpose(0, 1, 3, 4, 2).reshape(depth, n_pool, SB_WIDTH, PAGE_SIZE)

    def mixers_and_router(i, x, bsz, t, sample):
        n = bsz * t
        u = _proj_in_call(x, w_in_r[i])
        k_a = u[:, OFF_K:OFF_K + SB_WIDTH].reshape(bsz, t, SB_HEADS, HEAD_DIM)
        v_a = u[:, OFF_V:OFF_V + SB_WIDTH].reshape(bsz, t, SB_HEADS, HEAD_DIM)
        if sample:
            o_a = _sb_decode_call(u[:, OFF_Q:OFF_Q + SB_WIDTH], sb_bias[i], cache_k, cache_v, page_table, i)
            conv_b, delta_b, conv_c, ssm_c = state_conv_b[i], state_delta_b[i], state_conv_c[i], state_ssm_c[i]
            rows_per_seq = SSD_CHUNK
            u_mix = jnp.pad(u.reshape(bsz, t, U_WIDTH), ((0, 0), (0, rows_per_seq - t), (0, 0)))
            u_mix = u_mix.reshape(bsz * rows_per_seq, U_WIDTH)
        else:
            o_a = _sb_prefill_call(u, sb_bias[i], bsz, t)
            conv_b = jnp.zeros((bsz, CONV_WIDTH - 1, GDN_CONV_DIM), F32)
            delta_b = jnp.zeros((bsz, GDN_HEADS, GDN_DK, HEAD_DIM), F32)
            conv_c = jnp.zeros((bsz, CONV_WIDTH - 1, SSD_XBC), F32)
            ssm_c = jnp.zeros((bsz, SSD_HEADS, HEAD_DIM, SSD_STATE), F32)
            rows_per_seq = t
            u_mix = u
        o_b, conv_b_new, delta_new = _gdn_call(u_mix, bsz, rows_per_seq, t, conv_b, delta_b, w_conv_b[i],
                                               a_log_b[i], dt_bias_b[i], g_norm_b[i])
        o_c, conv_c_new, ssm_new = _ssd_call(u_mix, bsz, rows_per_seq, t, conv_c, ssm_c, w_conv_c[i], b_conv_c[i],
                                             a_log_c[i], dt_bias_c[i], d_skip_c[i], g_norm_c[i])
        if sample:
            o_b = o_b.reshape(bsz, -1, GDN_WIDTH)[:, :t].reshape(n, GDN_WIDTH)
            o_c = o_c.reshape(bsz, -1, SSD_WIDTH)[:, :t].reshape(n, SSD_WIDTH)
        x1, route, counts = _proj_out_call(x, o_a, o_b, o_c, w_out_b[i], ln1_g[i], ln1_b[i], router_w[i], router_b[i])
        blk = MOE_BLOCK_PROMPT if n * TOP_K >= MOE_BLOCK_PROMPT * N_EXPERTS else MOE_BLOCK_SMALL
        gate, src, dest_t, block_expert, n_used, row_start = _route(route, counts, blk)
        return (x1, gate, dest_t, block_expert, n_used, row_start, blk, x1[src]), (k_a, v_a, conv_b_new, delta_new,
                                                                                   conv_c_new, ssm_new)

    def experts(i, routed):
        x1, gate, dest_t, block_expert, n_used, row_start, blk, rows_in = routed
        rows_out = _moe_call(rows_in, block_expert, n_used, row_start, w_gate, b_gate, w_up, b_up, w_down, b_down, i,
                             blk)
        return rows_out[dest_t]

    def combine(i, routed, rows_k, p):
        x1, gate = routed[0], routed[1]
        return _post_call(x1, rows_k, gate, p[i].reshape(x1.shape[0], PLE_DIM), ln2_g[i], ln2_b[i], ple_gate_w_b[i],
                          ple_gate_b[i], ple_proj_w_b[i])

    groups = [(x_prompt, p_prompt, False), (x_sample, p_sample, True)]
    xs = [_ln_call(x.reshape(-1, x.shape[-1]), ln_in_g, ln_in_b) for x, _, _ in groups]
    states = [[], []]
    for i in range(depth):
        routed = []
        for gi, (x, _, sample) in enumerate(groups):
            r, st = mixers_and_router(i, xs[gi], x.shape[0], x.shape[1], sample)
            routed.append(r)
            states[gi].append(st)
        rows_k = [experts(i, r) for r in routed]
        xs = [combine(i, routed[gi], rows_k[gi], groups[gi][1]) for gi in range(len(groups))]
    y_prompt, y_sample = (xs[gi].reshape(groups[gi][0].shape) for gi in range(len(groups)))
    new_prompt, new_sample = ([jnp.stack([s[j] for s in st]) for j in range(6)] for st in states)
    return (y_prompt, y_sample, *new_prompt, *new_sample)
```
